```python
import math
import jax
import jax.numpy as jnp
from jax import lax
import numpy as np

D_MODEL = 1024
BATCH = 32
SEQ = 256
DEPTH = 4
DEC_BATCH = 4
DEC_SEQ = 2048
PAST_LEN = 512

GRID_W = 64
MIX_W = D_MODEL
GROUP_W = MIX_W // 4
N_IN_BLOCKS = 10
D_FF = 4 * D_MODEL
EPS = 1e-6
F_MIN = 1e-30
ADA_CHUNKS = 6
CONV_K = 31
CONV_PAD = CONV_K // 2
GM_CHUNK = 128
GM_HEADS = 4
GM_HD = GROUP_W // GM_HEADS
HG_HEADS = 4
HG_DK = GROUP_W // HG_HEADS
HG_CHUNK = 64
S5_IN = 16
S5_GROUPS = GROUP_W // S5_IN
S5_P = 64
POS_BASE = 10000.0

kernel_name = 'hybrid_diffusion_ctx_prefix_step'


def rms_norm(x, g):
    xf = x.astype(jnp.float32)
    y = xf * lax.rsqrt(jnp.mean(jnp.square(xf), axis=-1, keepdims=True) + EPS)
    return (y * g.astype(jnp.float32)).astype(x.dtype)


def layer_norm(x, g, b):
    xf = x.astype(jnp.float32)
    mu = jnp.mean(xf, axis=-1, keepdims=True)
    var = jnp.mean(jnp.square(xf - mu), axis=-1, keepdims=True)
    y = (xf - mu) * lax.rsqrt(var + EPS)
    return (y * g.astype(jnp.float32) + b.astype(jnp.float32)).astype(x.dtype)


def grid_pos_embed(n_tokens, dtype):
    rows = n_tokens // GRID_W
    r, col = jnp.meshgrid(jnp.arange(rows, dtype=jnp.float32), jnp.arange(GRID_W, dtype=jnp.float32), indexing='ij')
    r = r.reshape(-1)
    col = col.reshape(-1)
    quarter = D_MODEL // 4
    freq = jnp.exp(-math.log(POS_BASE) * jnp.arange(quarter, dtype=jnp.float32) / quarter)
    ar = r[:, None] * freq
    ac = col[:, None] * freq
    return jnp.concatenate([jnp.sin(ar), jnp.cos(ar), jnp.sin(ac), jnp.cos(ac)], axis=-1).astype(dtype)


def conformer_conv(a_val, a_gate, w, b, ln_g, ln_b):
    u = a_val * jax.nn.sigmoid(a_gate)
    u = lax.conv_general_dilated(u, w[:, None, :].astype(u.dtype), window_strides=(1,),
                                 padding=[(CONV_PAD, CONV_PAD)], dimension_numbers=('NWC', 'WIO', 'NWC'),
                                 feature_group_count=GROUP_W)
    return jax.nn.silu(layer_norm(u + b, ln_g, ln_b))


def chunk_gmlp(u, v, g, ws, bs):
    bsz, t, _ = u.shape
    n = t // GM_CHUNK
    vh = rms_norm(v, g).reshape(bsz, n, GM_CHUNK, GM_HEADS, GM_HD)
    sv = jnp.einsum('hts,bnshd->bnthd', ws, vh) + jnp.transpose(bs)[None, None, :, :, None]
    return u * sv.reshape(bsz, t, GROUP_W)


def hgrn_scan(q, k, v, log_f, s0):
    bsz, t, h, _ = q.shape
    n = t // HG_CHUNK

    def chunks(a):
        return jnp.moveaxis(a.reshape(bsz, n, HG_CHUNK, h, a.shape[-1]), 1, 0)

    lower = jnp.tril(jnp.ones((HG_CHUNK, HG_CHUNK), dtype=bool))[None, :, :, None, None]

    def step(S, inp):
        qc, kc, vc, lf = inp
        b = jnp.cumsum(lf, axis=1)
        b_end = b[:, -1]
        inter = jnp.einsum('blhk,bhkv->blhv', qc * jnp.exp(b), S)
        diff = jnp.minimum(b[:, :, None] - b[:, None, :], 0.0)
        decay = jnp.where(lower, jnp.exp(diff), 0.0)
        scores = jnp.einsum('bthk,bshk,btshk->bhts', qc, kc, decay)
        intra = jnp.einsum('bhts,bshv->bthv', scores, vc)
        S = jnp.exp(b_end)[..., None] * S + jnp.einsum('bshk,bshv->bhkv', kc * jnp.exp(b_end[:, None] - b), vc)
        return S, inter + intra

    s_end, o = lax.scan(step, s0, (chunks(q), chunks(k), chunks(v), chunks(log_f)))
    return jnp.moveaxis(o, 0, 1).reshape(bsz, t, h, v.shape[-1]), s_end


def hgrn_mixer(zq, zi, zg, zf_fwd, zf_bwd, lb_fwd, lb_bwd, norm_g, s0):
    bsz, t, _ = zq.shape

    def heads(a):
        return a.astype(jnp.float32).reshape(bsz, t, HG_HEADS, HG_DK)

    q = jax.nn.silu(heads(zq))
    v = heads(zi)

    def forget(zf, lb):
        lb = lb.astype(jnp.float32).reshape(HG_HEADS, HG_DK)
        f = lb + (1.0 - lb) * jax.nn.sigmoid(heads(zf))
        return jnp.log(jnp.maximum(f, F_MIN)), 1.0 - f

    lf_f, k_f = forget(zf_fwd, lb_fwd)
    lf_b, k_b = forget(zf_bwd, lb_bwd)
    s0 = s0.astype(jnp.float32)
    o_f, s_f = hgrn_scan(q, k_f, v, lf_f, s0[:, 0])
    o_b, s_b = hgrn_scan(jnp.flip(q, 1), jnp.flip(k_b, 1), jnp.flip(v, 1), jnp.flip(lf_b, 1), s0[:, 1])
    o = o_f + jnp.flip(o_b, 1)
    o = rms_norm(o, norm_g.reshape(HG_HEADS, HG_DK)).reshape(bsz, t, GROUP_W)
    out = o * jax.nn.silu(zg.astype(jnp.float32))
    return out.astype(zq.dtype), jnp.stack([s_f, s_b], axis=1)


def s5_scan(u, a_re, a_im, log_dt, b_re, b_im, c_re, c_im, s0_re, s0_im):
    A = lax.complex(a_re, a_im)
    dt = jnp.exp(log_dt)[:, None]
    a_bar = jnp.exp(A * dt)
    b_bar = ((a_bar - 1.0) / A)[..., None] * lax.complex(b_re, b_im)
    bu = jnp.einsum('gpc,btgc->btgp', b_bar, u.astype(jnp.complex64))
    a_seq = jnp.broadcast_to(a_bar, bu.shape)

    def compose(earlier, later):
        return later[0] * earlier[0], later[0] * earlier[1] + later[1]

    a_cum, h = lax.associative_scan(compose, (a_seq, bu), axis=1)
    h = h + a_cum * lax.complex(s0_re, s0_im)[:, None]
    y = jnp.einsum('gcp,btgp->btgc', lax.complex(c_re, c_im), h).real
    return y, h[:, -1]


def s5_mixer(zu, a_re, a_im, log_dt, b_re, b_im, c_re, c_im, d, glu_w, glu_b, s0_re, s0_im):
    bsz, t, _ = zu.shape
    f32 = lambda a: a.astype(jnp.float32)
    u = f32(zu).reshape(bsz, t, S5_GROUPS, S5_IN)

    def run(i, uu):
        return s5_scan(uu, f32(a_re[i]), f32(a_im[i]), f32(log_dt[i]), f32(b_re[i]), f32(b_im[i]),
                       f32(c_re[i]), f32(c_im[i]), f32(s0_re[:, i]), f32(s0_im[:, i]))

    y_f, h_f = run(0, u)
    y_b, h_b = run(1, jnp.flip(u, 1))
    y = (y_f + jnp.flip(y_b, 1)).reshape(bsz, t, GROUP_W) + f32(d) * f32(zu)
    y = jax.nn.gelu(y)
    out = y * jax.nn.sigmoid(jnp.dot(y, f32(glu_w)) + f32(glu_b))
    h_end = jnp.stack([h_f, h_b], axis=1)
    return out.astype(zu.dtype), jnp.real(h_end), jnp.imag(h_end)


def token_mixing(h, hg_s0, s5_s0_re, s5_s0_im, P, l):
    z = jnp.dot(h, P['w_in'][l])
    a_val, a_gate, g_u, g_v, h_q, h_i, h_g, h_ff, h_fb, s5_u = jnp.split(z, N_IN_BLOCKS, axis=-1)
    o_conv = conformer_conv(a_val, a_gate, P['conv_w'][l], P['conv_b'][l], P['conv_ln_g'][l], P['conv_ln_b'][l])
    o_gm = chunk_gmlp(g_u, g_v, P['gmlp_norm_g'][l], P['gmlp_ws'][l], P['gmlp_bs'][l])
    o_hg, hg_end = hgrn_mixer(h_q, h_i, h_g, h_ff, h_fb, P['hgrn_lb'][0, l], P['hgrn_lb'][1, l],
                              P['hgrn_norm_g'][l], hg_s0)
    o_s5, s5_re, s5_im = s5_mixer(s5_u, P['s5_a_re'][l], P['s5_a_im'][l], P['s5_log_dt'][l], P['s5_b_re'][l],
                                  P['s5_b_im'][l], P['s5_c_re'][l], P['s5_c_im'][l], P['s5_d'][l],
                                  P['s5_glu_w'][l], P['s5_glu_b'][l], s5_s0_re, s5_s0_im)
    mixed = jnp.concatenate([o_conv, o_gm, o_hg, o_s5], axis=-1)
    return jnp.dot(mixed, P['w_out'][l]), hg_end, s5_re, s5_im


def trunk_layer(x, cond, hg_s0, s5_s0_re, s5_s0_im, P, l):
    ada = jnp.dot(jax.nn.silu(cond), P['ada_w'][l]) + P['ada_b'][l]
    sh1, sc1, g1, sh2, sc2, g2 = jnp.split(ada[:, None, :], ADA_CHUNKS, axis=-1)
    h = rms_norm(x, P['norm1_g'][l]) * (1.0 + sc1) + sh1
    mix, hg_end, s5_re, s5_im = token_mixing(h, hg_s0, s5_s0_re, s5_s0_im, P, l)
    x = x + g1 * mix
    h = rms_norm(x, P['norm2_g'][l]) * (1.0 + sc2) + sh2
    x = x + g2 * jnp.dot(jnp.square(jax.nn.relu(jnp.dot(h, P['mlp_w1'][l]))), P['mlp_w2'][l])
    return x, hg_end, s5_re, s5_im


def setup_inputs(seed: int = 0) -> dict:
    key = jax.random.key(seed)
    keys = list(jax.random.split(key, 48))

    def nrm(shape, scale):
        return scale * jax.random.normal(keys.pop(), shape, jnp.float32)

    def gain(shape):
        return 1.0 + nrm(shape, 0.02)

    L = DEPTH
    inputs = {}
    inputs['x_prompt'] = nrm((BATCH, SEQ, D_MODEL), 1.0)
    inputs['x_sample'] = nrm((DEC_BATCH, DEC_SEQ, D_MODEL), 1.0)
    inputs['state_hgrn'] = nrm((DEC_BATCH, L, 2, HG_HEADS, HG_DK, HG_DK), 0.5)
    inputs['state_s5_re'] = nrm((DEC_BATCH, L, 2, S5_GROUPS, S5_P), 0.3)
    inputs['state_s5_im'] = nrm((DEC_BATCH, L, 2, S5_GROUPS, S5_P), 0.3)
    inputs['c'] = nrm((DEC_BATCH, D_MODEL), 1.0)
    inputs['c_ctx'] = nrm((D_MODEL,), 1.0)
    inputs['norm1_g'] = gain((L, D_MODEL))
    inputs['norm2_g'] = gain((L, D_MODEL))
    inputs['ada_w'] = nrm((L, D_MODEL, ADA_CHUNKS * D_MODEL), 0.5 * D_MODEL ** -0.5)
    inputs['ada_b'] = nrm((L, ADA_CHUNKS * D_MODEL), 0.02)
    inputs['w_in'] = nrm((L, D_MODEL, N_IN_BLOCKS * GROUP_W), D_MODEL ** -0.5)
    inputs['conv_w'] = nrm((L, CONV_K, GROUP_W), CONV_K ** -0.5)
    inputs['conv_b'] = nrm((L, GROUP_W), 0.02)
    inputs['conv_ln_g'] = gain((L, GROUP_W))
    inputs['conv_ln_b'] = nrm((L, GROUP_W), 0.02)
    inputs['gmlp_norm_g'] = gain((L, GROUP_W))
    inputs['gmlp_ws'] = nrm((L, GM_HEADS, GM_CHUNK, GM_CHUNK), GM_CHUNK ** -0.5)
    inputs['gmlp_bs'] = nrm((L, GM_HEADS, GM_CHUNK), 0.02)
    inputs['hgrn_lb_logits'] = nrm((2, L, GROUP_W), 0.1)
    inputs['hgrn_norm_g'] = gain((L, GROUP_W))
    inputs['s5_a_re'] = -0.5 + nrm((L, 2, S5_GROUPS, S5_P), 0.01)
    inputs['s5_a_im'] = math.pi * jnp.arange(S5_P, dtype=jnp.float32) + nrm((L, 2, S5_GROUPS, S5_P), 0.01)
    inputs['s5_log_dt'] = jax.random.uniform(keys.pop(), (L, 2, S5_GROUPS), jnp.float32,
                                             math.log(1e-3), math.log(1e-1))
    inputs['s5_b_re'] = nrm((L, 2, S5_GROUPS, S5_P, S5_IN), (2 * S5_IN) ** -0.5)
    inputs['s5_b_im'] = nrm((L, 2, S5_GROUPS, S5_P, S5_IN), (2 * S5_IN) ** -0.5)
    inputs['s5_c_re'] = nrm((L, 2, S5_GROUPS, S5_IN, S5_P), (2 * S5_P) ** -0.5)
    inputs['s5_c_im'] = nrm((L, 2, S5_GROUPS, S5_IN, S5_P), (2 * S5_P) ** -0.5)
    inputs['s5_d'] = nrm((L, GROUP_W), 0.5)
    inputs['s5_glu_w'] = nrm((L, GROUP_W, GROUP_W), GROUP_W ** -0.5)
    inputs['s5_glu_b'] = nrm((L, GROUP_W), 0.02)
    inputs['w_out'] = nrm((L, MIX_W, D_MODEL), MIX_W ** -0.5)
    inputs['mlp_w1'] = nrm((L, D_MODEL, D_FF), D_MODEL ** -0.5)
    inputs['mlp_w2'] = nrm((L, D_FF, D_MODEL), D_FF ** -0.5)
    inputs['final_norm_g'] = gain((D_MODEL,))
    return inputs


def reference(x_prompt, x_sample, state_hgrn, state_s5_re, state_s5_im, c, c_ctx,
              norm1_g, norm2_g, ada_w, ada_b, w_in, conv_w, conv_b, conv_ln_g, conv_ln_b,
              gmlp_norm_g, gmlp_ws, gmlp_bs, hgrn_lb_logits, hgrn_norm_g,
              s5_a_re, s5_a_im, s5_log_dt, s5_b_re, s5_b_im, s5_c_re, s5_c_im, s5_d, s5_glu_w, s5_glu_b,
              w_out, mlp_w1, mlp_w2, final_norm_g):
    lb_soft = jax.nn.softmax(hgrn_lb_logits.astype(jnp.float32), axis=1)
    hgrn_lb = jnp.cumsum(lb_soft, axis=1) - lb_soft[:, :1]
    P = {'norm1_g': norm1_g, 'norm2_g': norm2_g, 'ada_w': ada_w, 'ada_b': ada_b, 'w_in': w_in,
         'conv_w': conv_w, 'conv_b': conv_b, 'conv_ln_g': conv_ln_g, 'conv_ln_b': conv_ln_b,
         'gmlp_norm_g': gmlp_norm_g, 'gmlp_ws': gmlp_ws, 'gmlp_bs': gmlp_bs,
         'hgrn_lb': hgrn_lb, 'hgrn_norm_g': hgrn_norm_g,
         's5_a_re': s5_a_re, 's5_a_im': s5_a_im, 's5_log_dt': s5_log_dt, 's5_b_re': s5_b_re,
         's5_b_im': s5_b_im, 's5_c_re': s5_c_re, 's5_c_im': s5_c_im, 's5_d': s5_d,
         's5_glu_w': s5_glu_w, 's5_glu_b': s5_glu_b, 'w_out': w_out, 'mlp_w1': mlp_w1, 'mlp_w2': mlp_w2}

    xp = x_prompt
    bp = xp.shape[0]
    hg_zero = jnp.zeros((bp, 2, HG_HEADS, HG_DK, HG_DK), jnp.float32)
    s5_zero = jnp.zeros((bp, 2, S5_GROUPS, S5_P), jnp.float32)
    cond_ctx = c_ctx[None, :]
    hg_states, s5_re_states, s5_im_states = [], [], []
    for l in range(DEPTH):
        xp, hg_end, s5_re, s5_im = trunk_layer(xp, cond_ctx, hg_zero, s5_zero, s5_zero, P, l)
        hg_states.append(hg_end)
        s5_re_states.append(s5_re)
        s5_im_states.append(s5_im)
    y_prompt = rms_norm(xp, final_norm_g)
    new_state_hgrn = jnp.stack(hg_states, axis=1).astype(x_prompt.dtype)
    new_state_s5_re = jnp.stack(s5_re_states, axis=1).astype(x_prompt.dtype)
    new_state_s5_im = jnp.stack(s5_im_states, axis=1).astype(x_prompt.dtype)

    xs = x_sample + grid_pos_embed(x_sample.shape[1], x_sample.dtype)[None]
    for l in range(DEPTH):
        xs, _, _, _ = trunk_layer(xs, c, state_hgrn[:, l], state_s5_re[:, l], state_s5_im[:, l], P, l)
    y_sample = rms_norm(xs, final_norm_g)
    return (y_prompt, y_sample, new_state_hgrn, new_state_s5_re, new_state_s5_im)
```

```python
import functools
import math

import numpy as np
import jax
import jax.numpy as jnp
from jax import lax
from jax.experimental import pallas as pl
from jax.experimental.pallas import tpu as pltpu

D_MODEL = 1024
DEPTH = 4
GRID_W = 64
GROUP_W = 256
N_IN_BLOCKS = 10
D_FF = 4 * D_MODEL
EPS = 1e-6
F_MIN = 1e-30
ADA_CHUNKS = 6
CONV_K = 31
CONV_PAD = CONV_K // 2
GM_CHUNK = 128
GM_HEADS = 4
HG_HEADS = 4
HG_DK = 64
HG_CHUNK = 64
HG_LEVELS = 6
S5_IN = 16
S5_GROUPS = 16
S5_P = 64
S5_STATE = S5_GROUPS * S5_P
POS_BASE = 10000.0

SUBLANE = 8
COND_ROWS = 8
VMEM_LIMIT = 56 * 1024 * 1024

F32 = jnp.float32
BF16 = jnp.bfloat16


def _sigmoid(x):
    return 1.0 / (1.0 + jnp.exp(-x))


def _silu(x):
    return x * _sigmoid(x)


def _params(n_parallel=1):
    return pltpu.CompilerParams(dimension_semantics=("arbitrary",) * n_parallel,
                                vmem_limit_bytes=VMEM_LIMIT)


def _split3(x):
    hi = x.astype(BF16)
    r = x - hi.astype(F32)
    mid = r.astype(BF16)
    lo = (r - mid.astype(F32)).astype(BF16)
    return hi, mid, lo


def _dot(a, b):
    return jnp.dot(a, b, preferred_element_type=F32)


def _dot_nt(a, b):
    return lax.dot_general(a, b, (((1,), (1,)), ((), ())), preferred_element_type=F32)


def _dot_tn(a, b):
    return lax.dot_general(a, b, (((0,), (0,)), ((), ())), preferred_element_type=F32)


ADA_BLOCK = 1536


def _ada_kernel(cond_ref, w_ref, b_ref, o_ref):
    s = _silu(cond_ref[...])
    o_ref[...] = _dot(s.astype(BF16), w_ref[...].astype(BF16)) + b_ref[...]


def _ada_call(cond, ada_w, ada_b):
    n_out = ADA_CHUNKS * D_MODEL
    return pl.pallas_call(
        _ada_kernel,
        grid=(DEPTH, n_out // ADA_BLOCK),
        in_specs=[pl.BlockSpec((COND_ROWS, D_MODEL), lambda l, j: (0, 0)),
                  pl.BlockSpec((None, D_MODEL, ADA_BLOCK), lambda l, j: (l, 0, j)),
                  pl.BlockSpec((None, 1, ADA_BLOCK), lambda l, j: (l, 0, j))],
        out_specs=pl.BlockSpec((None, COND_ROWS, ADA_BLOCK), lambda l, j: (l, 0, j)),
        out_shape=jax.ShapeDtypeStruct((DEPTH, COND_ROWS, n_out), F32),
        compiler_params=_params(2),
        name="ada",
    )(cond, ada_w, ada_b.reshape(DEPTH, 1, n_out))


IN_TILE = 512


def _inproj_kernel(x_ref, ada_ref, g_ref, w_ref, z_ref):
    x = x_ref[...]
    ada = ada_ref[...]
    sh1 = ada[:, 0:D_MODEL]
    sc1 = ada[:, D_MODEL:2 * D_MODEL]
    ms = jnp.mean(x * x, axis=-1, keepdims=True)
    h = x * lax.rsqrt(ms + EPS) * (g_ref[...] * (1.0 + sc1)) + sh1
    z_ref[...] = _dot(h.astype(BF16), w_ref[...])


def _inproj_call(x, ada4, g1, w_in, layer, cond_row):
    n_tok = x.shape[0]
    zw = N_IN_BLOCKS * GROUP_W
    return pl.pallas_call(
        _inproj_kernel,
        grid=(n_tok // IN_TILE,),
        in_specs=[pl.BlockSpec((IN_TILE, D_MODEL), lambda i: (i, 0)),
                  pl.BlockSpec((None, None, 1, ADA_CHUNKS * D_MODEL),
                               lambda i: (layer, cond_row(i * IN_TILE), 0, 0)),
                  pl.BlockSpec((None, 1, D_MODEL), lambda i: (layer, 0, 0)),
                  pl.BlockSpec((None, D_MODEL, zw), lambda i: (layer, 0, 0))],
        out_specs=pl.BlockSpec((IN_TILE, zw), lambda i: (i, 0)),
        out_shape=jax.ShapeDtypeStruct((n_tok, zw), F32),
        compiler_params=_params(1),
        name="inproj",
    )(x, ada4, g1, w_in)


CONV_TILE = 32
CONV_HALO = 16


def _conv_kernel(z_ref, w_ref, b_ref, lng_ref, lnb_ref, o_ref, pad_ref, *, seq):
    zeros = jnp.zeros((CONV_HALO, GROUP_W), F32)
    pad_ref[0:CONV_HALO, :] = zeros
    pad_ref[CONV_HALO + seq:2 * CONV_HALO + seq, :] = zeros
    pad_ref[CONV_HALO:CONV_HALO + seq, :] = z_ref[:, 0:GROUP_W] * _sigmoid(z_ref[:, GROUP_W:2 * GROUP_W])
    w = w_ref[...]
    first = CONV_HALO - CONV_PAD

    def tile(i, carry):
        r0 = pl.multiple_of(i * CONV_TILE, CONV_TILE)
        win = pad_ref[pl.ds(r0, CONV_TILE + 2 * CONV_HALO), :]
        acc = jnp.zeros((CONV_TILE, GROUP_W), F32)
        for k in range(CONV_K):
            acc = acc + w[k:k + 1, :] * win[first + k:first + k + CONV_TILE, :]
        c = acc + b_ref[...]
        mu = jnp.mean(c, axis=-1, keepdims=True)
        cc = c - mu
        var = jnp.mean(cc * cc, axis=-1, keepdims=True)
        y = cc * lax.rsqrt(var + EPS) * lng_ref[...] + lnb_ref[...]
        o_ref[pl.ds(r0, CONV_TILE), :] = _silu(y)
        return carry

    lax.fori_loop(0, seq // CONV_TILE, tile, 0)


def _conv_call(z, conv_w, conv_b, ln_g, ln_b, layer, n_seq, seq):
    vec = pl.BlockSpec((None, 1, GROUP_W), lambda s: (layer, 0, 0))
    return pl.pallas_call(
        functools.partial(_conv_kernel, seq=seq),
        grid=(n_seq,),
        in_specs=[pl.BlockSpec((seq, 2 * GROUP_W), lambda s: (s, 0)),
                  pl.BlockSpec((None, CONV_K, GROUP_W), lambda s: (layer, 0, 0)),
                  vec, vec, vec],
        out_specs=pl.BlockSpec((seq, GROUP_W), lambda s: (s, 0)),
        out_shape=jax.ShapeDtypeStruct((n_seq * seq, GROUP_W), F32),
        scratch_shapes=[pltpu.VMEM((seq + 2 * CONV_HALO, GROUP_W), F32)],
        compiler_params=_params(1),
        name="conv",
    )(z, conv_w, conv_b, ln_g, ln_b)


def _head_mask(rows_per_head, cols_per_head, n_rows, n_cols):
    r = lax.broadcasted_iota(jnp.int32, (n_rows, n_cols), 0) // rows_per_head
    c = lax.broadcasted_iota(jnp.int32, (n_rows, n_cols), 1) // cols_per_head
    return r == c


def _gmlp_kernel(z_ref, g_ref, ws_ref, bs_ref, o_ref, *, seq):
    hd = GROUP_W // GM_HEADS
    mask = _head_mask(GM_CHUNK, hd, GM_HEADS * GM_CHUNK, GROUP_W)

    def chunk(i, carry):
        r0 = pl.multiple_of(i * GM_CHUNK, GM_CHUNK)
        u = z_ref[pl.ds(r0, GM_CHUNK), 0:GROUP_W]
        v = z_ref[pl.ds(r0, GM_CHUNK), GROUP_W:2 * GROUP_W]
        vn = v * lax.rsqrt(jnp.mean(v * v, axis=-1, keepdims=True) + EPS) * g_ref[...]
        stack = jnp.where(mask, jnp.concatenate([vn] * GM_HEADS, axis=0), 0.0).astype(BF16)
        sv = _dot(ws_ref[...], stack) + bs_ref[...]
        o_ref[pl.ds(r0, GM_CHUNK), :] = u * sv
        return carry

    lax.fori_loop(0, seq // GM_CHUNK, chunk, 0)


def _gmlp_call(z, norm_g, ws_cat, bs_full, layer, n_seq, seq):
    return pl.pallas_call(
        functools.partial(_gmlp_kernel, seq=seq),
        grid=(n_seq,),
        in_specs=[pl.BlockSpec((seq, 2 * GROUP_W), lambda s: (s, 1)),
                  pl.BlockSpec((None, 1, GROUP_W), lambda s: (layer, 0, 0)),
                  pl.BlockSpec((None, GM_CHUNK, GM_HEADS * GM_CHUNK), lambda s: (layer, 0, 0)),
                  pl.BlockSpec((None, GM_CHUNK, GROUP_W), lambda s: (layer, 0, 0))],
        out_specs=pl.BlockSpec((seq, GROUP_W), lambda s: (s, 0)),
        out_shape=jax.ShapeDtypeStruct((n_seq * seq, GROUP_W), F32),
        compiler_params=_params(1),
        name="gmlp",
    )(z, norm_g, ws_cat, bs_full)


def _hgrn_constants():
    n = HG_CHUNK
    blocks = []
    t = np.arange(n)
    cum = (t[None, :] <= t[:, None]).astype(np.float32)
    rem = (t[None, :] > t[:, None]).astype(np.float32)
    blocks += [cum, rem]
    upper = np.zeros((HG_LEVELS, n), np.float32)
    for lvl in range(HG_LEVELS):
        m = 1 << lvl
        e = np.zeros((n, n), np.float32)
        for row in range(n):
            r0 = (row & ~(2 * m - 1)) + m - 1
            if row & m:
                e[row, r0 + 1:row + 1] = 1.0
                upper[lvl, row] = 1.0
            else:
                e[row, row + 1:r0 + 1] = 1.0
        blocks.append(e)
    m_v = np.stack(blocks)
    m_all = np.stack([m_v, m_v[:, ::-1, ::-1]])
    up = np.stack([upper, upper[:, ::-1]])
    mq = np.repeat(up[..., None], GROUP_W, axis=-1)
    blk = np.zeros((HG_LEVELS, n, n), np.float32)
    for lvl in range(HG_LEVELS):
        b = t >> (lvl + 1)
        blk[lvl] = (b[:, None] == b[None, :])
    bmask = np.tile(blk, (1, 1, HG_HEADS))
    return (m_all.reshape(2, (2 + HG_LEVELS) * n, n), mq.astype(np.float32), bmask)


_HG_MALL, _HG_MQ, _HG_BMASK = _hgrn_constants()


def _hgrn_kernel(zq_ref, zi_ref, zg_ref, zff_ref, zfb_ref, lb_ref, ng_ref, s0_ref,
                 mall_ref, mq_ref, bmask_ref, o_ref, s_out_ref, ob_ref, s_ref, *, seq):
    n = HG_CHUNK
    n_chunks = seq // n
    hmask = _head_mask(HG_DK, HG_DK, GROUP_W, GROUP_W)
    hmask_bf = jnp.where(hmask, 1.0, 0.0).astype(BF16)

    def block_diag(a):
        return jnp.where(hmask, jnp.concatenate([a] * HG_HEADS, axis=0), 0.0).astype(BF16)

    def run(direction, zf_ref, emit):
        lb = lb_ref[direction:direction + 1, :]
        s_ref[...] = s0_ref[direction]
        mall = mall_ref[direction]
        end_row = n - 1 if direction == 0 else 0

        def chunk(i, carry):
            c = i if direction == 0 else n_chunks - 1 - i
            r0 = pl.multiple_of(c * n, n)
            rows = pl.ds(r0, n)
            q = _silu(zq_ref[rows, :])
            v = zi_ref[rows, :]
            f = lb + (1.0 - lb) * _sigmoid(zf_ref[rows, :])
            lf = jnp.log(jnp.maximum(f, F_MIN))
            k = 1.0 - f
            l1, l2, l3 = _split3(lf)
            w = jnp.exp(_dot(mall, l1) + _dot(mall, l2) + _dot(mall, l3))
            w_cum = w[0:n]
            w_rem = w[n:2 * n]
            w_end = w[end_row:end_row + 1]
            st = s_ref[...]
            inter = _dot_nt((q * w_cum).astype(BF16), st.astype(BF16))
            scores = jnp.zeros((n, GROUP_W), F32)
            for lvl in range(HG_LEVELS):
                wl = w[(2 + lvl) * n:(3 + lvl) * n]
                mq = mq_ref[direction, lvl]
                ql = (q * wl * mq).astype(BF16)
                kl = k * wl * (1.0 - mq)
                scores = scores + bmask_ref[lvl] * _dot_nt(ql, block_diag(kl))
            diag = _dot((q * k).astype(BF16), hmask_bf)
            intra = _dot(scores.astype(BF16), block_diag(v)) + diag * v
            o = inter + intra
            upd = _dot_tn(v.astype(BF16), (k * w_rem).astype(BF16))
            s_ref[...] = st * w_end + jnp.where(hmask, upd, 0.0)
            emit(rows, o)
            return carry

        lax.fori_loop(0, n_chunks, chunk, 0)
        s_out_ref[direction] = s_ref[...]

    def emit_backward(rows, o):
        ob_ref[rows, :] = o

    def emit_forward(rows, o):
        o = o + ob_ref[rows, :]
        hi, mid, _ = _split3(o * o)
        ms = (_dot(hi, hmask_bf) + _dot(mid, hmask_bf)) * (1.0 / HG_DK)
        y = o * lax.rsqrt(ms + EPS) * ng_ref[...]
        o_ref[rows, :] = y * _silu(zg_ref[rows, :])

    run(1, zfb_ref, emit_backward)
    run(0, zff_ref, emit_forward)


def _hgrn_call(z, lb, norm_g, s0t, layer, n_seq, seq):
    def zcol(j):
        return pl.BlockSpec((seq, GROUP_W), lambda s: (s, j))

    n_rows = (2 + HG_LEVELS) * HG_CHUNK
    full = lambda shape: pl.BlockSpec(shape, lambda s: (0,) * len(shape))
    return pl.pallas_call(
        functools.partial(_hgrn_kernel, seq=seq),
        grid=(n_seq,),
        in_specs=[zcol(4), zcol(5), zcol(6), zcol(7), zcol(8),
                  pl.BlockSpec((None, 2, GROUP_W), lambda s: (layer, 0, 0)),
                  pl.BlockSpec((None, 1, GROUP_W), lambda s: (layer, 0, 0)),
                  pl.BlockSpec((None, 2, GROUP_W, GROUP_W), lambda s: (s, 0, 0, 0)),
                  full((2, n_rows, HG_CHUNK)),
                  full((2, HG_LEVELS, HG_CHUNK, GROUP_W)),
                  full((HG_LEVELS, HG_CHUNK, GROUP_W))],
        out_specs=[pl.BlockSpec((seq, GROUP_W), lambda s: (s, 0)),
                   pl.BlockSpec((None, 2, GROUP_W, GROUP_W), lambda s: (s, 0, 0, 0))],
        out_shape=[jax.ShapeDtypeStruct((n_seq * seq, GROUP_W), F32),
                   jax.ShapeDtypeStruct((n_seq, 2, GROUP_W, GROUP_W), F32)],
        scratch_shapes=[pltpu.VMEM((seq, GROUP_W), F32), pltpu.VMEM((GROUP_W, GROUP_W), F32)],
        compiler_params=_params(1),
        name="hgrn",
    )(z, z, z, z, z, lb, norm_g, s0t,
      jnp.asarray(_HG_MALL, BF16), jnp.asarray(_HG_MQ), jnp.asarray(_HG_BMASK))


def _hgrn_state_to_kernel(s):
    st = jnp.swapaxes(s.astype(F32), -1, -2)
    eye = jnp.eye(HG_HEADS, dtype=F32)
    full = jnp.einsum('ndhvk,hg->ndhvgk', st, eye)
    return full.reshape(s.shape[0], 2, GROUP_W, GROUP_W)


def _hgrn_state_from_kernel(st):
    a = st.reshape(st.shape[0], 2, HG_HEADS, HG_DK, HG_HEADS, HG_DK)
    d = jnp.stack([a[:, :, h, :, h, :] for h in range(HG_HEADS)], axis=2)
    return jnp.swapaxes(d, -1, -2)


S5_TIME_TILE = 256
S5_STEPS = (1, 2, 4)
S5_TABLES = 2 * (len(S5_STEPS) + 1)


def _s5_kernel(u_ref, bre_ref, bim_ref, cre_ref, cim_ref, tab_ref, s0re_ref, s0im_ref,
               d_ref, gw_ref, gb_ref, o_ref, hre_ref, him_ref, sre_ref, sim_ref, y_ref, *, seq):
    n_tiles = seq // S5_TIME_TILE
    n_blocks = S5_TIME_TILE // SUBLANE

    def run(direction, first):
        last_row = SUBLANE - 1 if direction == 0 else 0

        def block(j, carry):
            jj = j if direction == 0 else n_blocks - 1 - j
            rows = pl.ds(pl.multiple_of(jj * SUBLANE, SUBLANE), SUBLANE)
            hr = sre_ref[rows, :]
            hi = sim_ref[rows, :]
            for s, step in enumerate(S5_STEPS):
                shift = step if direction == 0 else SUBLANE - step
                rr = pltpu.roll(hr, shift, 0)
                ri = pltpu.roll(hi, shift, 0)
                ar = tab_ref[direction, 2 * s]
                ai = tab_ref[direction, 2 * s + 1]
                hr, hi = hr + ar * rr - ai * ri, hi + ar * ri + ai * rr
            pr = tab_ref[direction, 2 * len(S5_STEPS)]
            pi = tab_ref[direction, 2 * len(S5_STEPS) + 1]
            cr, ci = carry
            hr, hi = hr + pr * cr - pi * ci, hi + pr * ci + pi * cr
            sre_ref[rows, :] = hr
            sim_ref[rows, :] = hi
            return (jnp.broadcast_to(hr[last_row:last_row + 1, :], (SUBLANE, S5_STATE)),
                    jnp.broadcast_to(hi[last_row:last_row + 1, :], (SUBLANE, S5_STATE)))

        carry = (jnp.broadcast_to(s0re_ref[direction], (SUBLANE, S5_STATE)),
                 jnp.broadcast_to(s0im_ref[direction], (SUBLANE, S5_STATE)))
        for ti in range(n_tiles):
            tt = ti if direction == 0 else n_tiles - 1 - ti
            rows = slice(tt * S5_TIME_TILE, (tt + 1) * S5_TIME_TILE)
            ub = u_ref[rows, :].astype(BF16)
            sre_ref[...] = _dot(ub, bre_ref[direction])
            sim_ref[...] = _dot(ub, bim_ref[direction])
            carry = lax.fori_loop(0, n_blocks, block, carry)
            y = (_dot(sre_ref[...].astype(BF16), cre_ref[direction])
                 + _dot(sim_ref[...].astype(BF16), cim_ref[direction]))
            if first:
                y_ref[rows, :] = y
            else:
                y_ref[rows, :] = y_ref[rows, :] + y
        hre_ref[direction] = carry[0][0:1, :]
        him_ref[direction] = carry[1][0:1, :]

    run(0, True)
    run(1, False)
    for ti in range(n_tiles):
        rows = slice(ti * S5_TIME_TILE, (ti + 1) * S5_TIME_TILE)
        y = jax.nn.gelu(y_ref[rows, :] + d_ref[...] * u_ref[rows, :])
        gate = _sigmoid(_dot(y.astype(BF16), gw_ref[...]) + gb_ref[...])
        o_ref[rows, :] = y * gate


def _s5_call(z, consts, s0re, s0im, d, glu_w, glu_b, layer, n_seq, seq):
    bre, bim, cre, cim, tab = consts
    lay = lambda shape: pl.BlockSpec((None,) + shape, lambda s: (layer,) + (0,) * len(shape))
    state = pl.BlockSpec((None, 2, 1, S5_STATE), lambda s: (s, 0, 0, 0))
    return pl.pallas_call(
        functools.partial(_s5_kernel, seq=seq),
        grid=(n_seq,),
        in_specs=[pl.BlockSpec((seq, GROUP_W), lambda s: (s, N_IN_BLOCKS - 1)),
                  lay((2, GROUP_W, S5_STATE)), lay((2, GROUP_W, S5_STATE)),
                  lay((2, S5_STATE, GROUP_W)), lay((2, S5_STATE, GROUP_W)),
                  lay((2, S5_TABLES, SUBLANE, S5_STATE)),
                  state, state,
                  lay((1, GROUP_W)), lay((GROUP_W, GROUP_W)), lay((1, GROUP_W))],
        out_specs=[pl.BlockSpec((seq, GROUP_W), lambda s: (s, 0)), state, state],
        out_shape=[jax.ShapeDtypeStruct((n_seq * seq, GROUP_W), F32),
                   jax.ShapeDtypeStruct((n_seq, 2, 1, S5_STATE), F32),
                   jax.ShapeDtypeStruct((n_seq, 2, 1, S5_STATE), F32)],
        scratch_shapes=[pltpu.VMEM((S5_TIME_TILE, S5_STATE), F32),
                        pltpu.VMEM((S5_TIME_TILE, S5_STATE), F32),
                        pltpu.VMEM((seq, GROUP_W), F32)],
        compiler_params=_params(1),
        name="s5",
    )(z, bre, bim, cre, cim, tab, s0re, s0im, d, glu_w, glu_b)


def _s5_constants(a_re, a_im, log_dt, b_re, b_im, c_re, c_im):
    a = lax.complex(a_re.astype(F32), a_im.astype(F32))
    dt = jnp.exp(log_dt.astype(F32))[..., None]
    a_bar = jnp.exp(a * dt)
    b_bar = ((a_bar - 1.0) / a)[..., None] * lax.complex(b_re.astype(F32), b_im.astype(F32))
    eye = jnp.eye(S5_GROUPS, dtype=F32)

    def in_mat(b):
        return jnp.einsum('ldgpc,gh->ldgchp', b, eye).reshape(DEPTH, 2, GROUP_W, S5_STATE).astype(BF16)

    def out_mat(c):
        return jnp.einsum('ldgcp,gh->ldgphc', c, eye).reshape(DEPTH, 2, S5_STATE, GROUP_W).astype(BF16)

    a1 = a_bar.reshape(DEPTH, 2, S5_STATE)
    pows = [a1]
    for _ in range(SUBLANE - 1):
        pows.append(pows[-1] * a1)
    row = jnp.arange(SUBLANE)
    tabs = []
    for direction in range(2):
        t = []
        for step in S5_STEPS:
            live = (row >= step) if direction == 0 else (row < SUBLANE - step)
            t.append(jnp.where(live[None, :, None], pows[step - 1][:, direction, None, :], 0.0))
        order = range(SUBLANE) if direction == 0 else range(SUBLANE - 1, -1, -1)
        t.append(jnp.stack([pows[i][:, direction] for i in order], axis=1))
        planes = []
        for x in t:
            planes += [jnp.real(x), jnp.imag(x)]
        tabs.append(jnp.stack(planes, axis=1))
    tab = jnp.stack(tabs, axis=1).astype(F32)
    return (in_mat(jnp.real(b_bar)), in_mat(jnp.imag(b_bar)),
            out_mat(c_re.astype(F32)), out_mat(-c_im.astype(F32)), tab)


OUT_TILE = 256
FF_BLOCK = 1024


def _outproj_kernel(x_ref, oa_ref, ob_ref, oc_ref, od_ref, ada_ref, g2_ref, wo_ref, w1_ref, w2_ref,
                    gf_ref, *out_refs, final):
    ada = ada_ref[...]
    chunk = lambda j: ada[:, j * D_MODEL:(j + 1) * D_MODEL]
    gate1, sh2, sc2, gate2 = chunk(2), chunk(3), chunk(4), chunk(5)
    mixed = jnp.concatenate([oa_ref[...], ob_ref[...], oc_ref[...], od_ref[...]], axis=-1).astype(BF16)
    x = x_ref[...] + gate1 * _dot(mixed, wo_ref[...])
    ms = jnp.mean(x * x, axis=-1, keepdims=True)
    h = (x * lax.rsqrt(ms + EPS) * (g2_ref[...] * (1.0 + sc2)) + sh2).astype(BF16)
    acc = jnp.zeros((OUT_TILE, D_MODEL), F32)
    for j in range(D_FF // FF_BLOCK):
        cols = slice(j * FF_BLOCK, (j + 1) * FF_BLOCK)
        a = jnp.maximum(_dot(h, w1_ref[:, cols]), 0.0)
        acc = acc + _dot((a * a).astype(BF16), w2_ref[cols, :])
    x = x + gate2 * acc
    out_refs[0][...] = x
    if final:
        ms = jnp.mean(x * x, axis=-1, keepdims=True)
        out_refs[1][...] = x * lax.rsqrt(ms + EPS) * gf_ref[...]


def _outproj_call(x, mix, ada4, g2, w_out, w1, w2, gf, layer, cond_row, final):
    n_tok = x.shape[0]
    tok = pl.BlockSpec((OUT_TILE, D_MODEL), lambda i: (i, 0))
    grp = pl.BlockSpec((OUT_TILE, GROUP_W), lambda i: (i, 0))
    lay = lambda shape: pl.BlockSpec((None,) + shape, lambda i: (layer,) + (0,) * len(shape))
    n_out = 2 if final else 1
    res = pl.pallas_call(
        functools.partial(_outproj_kernel, final=final),
        grid=(n_tok // OUT_TILE,),
        in_specs=[tok, grp, grp, grp, grp,
                  pl.BlockSpec((None, None, 1, ADA_CHUNKS * D_MODEL),
                               lambda i: (layer, cond_row(i * OUT_TILE), 0, 0)),
                  lay((1, D_MODEL)), lay((D_MODEL, D_MODEL)), lay((D_MODEL, D_FF)), lay((D_FF, D_MODEL)),
                  pl.BlockSpec((1, D_MODEL), lambda i: (0, 0))],
        out_specs=[tok] * n_out,
        out_shape=[jax.ShapeDtypeStruct((n_tok, D_MODEL), F32)] * n_out,
        compiler_params=_params(1),
        name="outproj",
    )(x, *mix, ada4, g2, w_out, w1, w2, gf)
    return res


def _grid_pos_embed(n_tokens):
    rows = n_tokens // GRID_W
    r, col = jnp.meshgrid(jnp.arange(rows, dtype=F32), jnp.arange(GRID_W, dtype=F32), indexing='ij')
    r = r.reshape(-1)
    col = col.reshape(-1)
    quarter = D_MODEL // 4
    freq = jnp.exp(-math.log(POS_BASE) * jnp.arange(quarter, dtype=F32) / quarter)
    ar = r[:, None] * freq
    ac = col[:, None] * freq
    return jnp.concatenate([jnp.sin(ar), jnp.cos(ar), jnp.sin(ac), jnp.cos(ac)], axis=-1)


def _run_stream(x, n_seq, seq, cond_row, hg_s0, s5_s0re, s5_s0im, P):
    hg_states, s5_re, s5_im = [], [], []
    y = None
    for l in range(DEPTH):
        z = _inproj_call(x, P['ada4'], P['norm1_g'], P['w_in'], l, cond_row)
        o_conv = _conv_call(z, P['conv_w'], P['conv_b'], P['conv_ln_g'], P['conv_ln_b'], l, n_seq, seq)
        o_gm = _gmlp_call(z, P['gmlp_norm_g'], P['gmlp_ws'], P['gmlp_bs'], l, n_seq, seq)
        o_hg, hg_end = _hgrn_call(z, P['hgrn_lb'], P['hgrn_norm_g'], hg_s0[l], l, n_seq, seq)
        o_s5, h_re, h_im = _s5_call(z, P['s5'], s5_s0re[l], s5_s0im[l], P['s5_d'], P['s5_glu_w'],
                                    P['s5_glu_b'], l, n_seq, seq)
        final = l == DEPTH - 1
        res = _outproj_call(x, (o_conv, o_gm, o_hg, o_s5), P['ada4'], P['norm2_g'], P['w_out'],
                            P['mlp_w1'], P['mlp_w2'], P['final_norm_g'], l, cond_row, final)
        x = res[0]
        if final:
            y = res[1]
        hg_states.append(hg_end)
        s5_re.append(h_re)
        s5_im.append(h_im)
    return y, hg_states, s5_re, s5_im


def kernel(x_prompt, x_sample, state_hgrn, state_s5_re, state_s5_im, c, c_ctx, norm1_g, norm2_g, ada_w, ada_b, w_in, conv_w, conv_b, conv_ln_g, conv_ln_b, gmlp_norm_g, gmlp_ws, gmlp_bs, hgrn_lb_logits, hgrn_norm_g, s5_a_re, s5_a_im, s5_log_dt, s5_b_re, s5_b_im, s5_c_re, s5_c_im, s5_d, s5_glu_w, s5_glu_b, w_out, mlp_w1, mlp_w2, final_norm_g):
    n_ctx, seq_ctx, _ = x_prompt.shape
    n_lat, seq_lat, _ = x_sample.shape
    assert n_lat + 1 <= COND_ROWS
    assert seq_ctx % IN_TILE == 0 or IN_TILE % seq_ctx == 0

    lb_soft = jax.nn.softmax(hgrn_lb_logits.astype(F32), axis=1)
    hgrn_lb = jnp.cumsum(lb_soft, axis=1) - lb_soft[:, :1]
    vec = lambda a: a.astype(F32).reshape(DEPTH, 1, -1)
    hd = GROUP_W // GM_HEADS
    P = {
        'norm1_g': vec(norm1_g), 'norm2_g': vec(norm2_g),
        'w_in': w_in.astype(BF16), 'w_out': w_out.astype(BF16),
        'mlp_w1': mlp_w1.astype(BF16), 'mlp_w2': mlp_w2.astype(BF16),
        'conv_w': conv_w.astype(F32), 'conv_b': vec(conv_b), 'conv_ln_g': vec(conv_ln_g),
        'conv_ln_b': vec(conv_ln_b),
        'gmlp_norm_g': vec(gmlp_norm_g),
        'gmlp_ws': jnp.transpose(gmlp_ws, (0, 2, 1, 3)).reshape(DEPTH, GM_CHUNK, GM_HEADS * GM_CHUNK).astype(BF16),
        'gmlp_bs': jnp.repeat(jnp.transpose(gmlp_bs.astype(F32), (0, 2, 1)), hd, axis=2),
        'hgrn_lb': jnp.transpose(hgrn_lb, (1, 0, 2)), 'hgrn_norm_g': vec(hgrn_norm_g),
        's5': _s5_constants(s5_a_re, s5_a_im, s5_log_dt, s5_b_re, s5_b_im, s5_c_re, s5_c_im),
        's5_d': vec(s5_d), 's5_glu_w': s5_glu_w.astype(BF16), 's5_glu_b': vec(s5_glu_b),
        'final_norm_g': final_norm_g.astype(F32).reshape(1, D_MODEL),
    }
    cond = jnp.zeros((COND_ROWS, D_MODEL), F32).at[0].set(c_ctx.astype(F32)).at[1:1 + n_lat].set(c.astype(F32))
    ada = _ada_call(cond, ada_w.astype(F32), ada_b.astype(F32))
    P['ada4'] = ada.reshape(DEPTH, COND_ROWS, 1, ADA_CHUNKS * D_MODEL)

    zeros_hg = jnp.zeros((n_ctx, 2, GROUP_W, GROUP_W), F32)
    zeros_s5 = jnp.zeros((n_ctx, 2, 1, S5_STATE), F32)
    y_ctx, hg_ctx, re_ctx, im_ctx = _run_stream(
        x_prompt.astype(F32).reshape(n_ctx * seq_ctx, D_MODEL), n_ctx, seq_ctx, lambda r: 0,
        [zeros_hg] * DEPTH, [zeros_s5] * DEPTH, [zeros_s5] * DEPTH, P)

    xs = x_sample.astype(F32) + _grid_pos_embed(seq_lat)[None]
    hg0 = [_hgrn_state_to_kernel(state_hgrn[:, l]) for l in range(DEPTH)]
    re0 = [state_s5_re[:, l].astype(F32).reshape(n_lat, 2, 1, S5_STATE) for l in range(DEPTH)]
    im0 = [state_s5_im[:, l].astype(F32).reshape(n_lat, 2, 1, S5_STATE) for l in range(DEPTH)]
    y_lat, _, _, _ = _run_stream(xs.reshape(n_lat * seq_lat, D_MODEL), n_lat, seq_lat,
                                 lambda r: 1 + r // seq_lat, hg0, re0, im0, P)

    dt = x_prompt.dtype
    new_hg = jnp.stack([_hgrn_state_from_kernel(s) for s in hg_ctx], axis=1).astype(dt)
    new_re = jnp.stack([s.reshape(n_ctx, 2, S5_GROUPS, S5_P) for s in re_ctx], axis=1).astype(dt)
    new_im = jnp.stack([s.reshape(n_ctx, 2, S5_GROUPS, S5_P) for s in im_ctx], axis=1).astype(dt)
    return (y_ctx.reshape(n_ctx, seq_ctx, D_MODEL).astype(dt),
            y_lat.reshape(n_lat, seq_lat, D_MODEL).astype(x_sample.dtype), new_hg, new_re, new_im)
```

```python
import functools
import math

import numpy as np
import jax
import jax.numpy as jnp
from jax import lax
from jax.experimental import pallas as pl
from jax.experimental.pallas import tpu as pltpu

D_MODEL = 1024
DEPTH = 4
GRID_W = 64
GROUP_W = 256
N_IN_BLOCKS = 10
D_FF = 4 * D_MODEL
EPS = 1e-6
F_MIN = 1e-30
ADA_CHUNKS = 6
CONV_K = 31
CONV_PAD = CONV_K // 2
GM_CHUNK = 128
GM_HEADS = 4
HG_HEADS = 4
HG_DK = 64
HG_CHUNK = 64
HG_LEVELS = 6
S5_IN = 16
S5_GROUPS = 16
S5_P = 64
S5_STATE = S5_GROUPS * S5_P
POS_BASE = 10000.0

SUBLANE = 8
COND_ROWS = 8
VMEM_LIMIT = 56 * 1024 * 1024

F32 = jnp.float32
BF16 = jnp.bfloat16


def _sigmoid(x):
    return 1.0 / (1.0 + jnp.exp(-x))


def _silu(x):
    return x * _sigmoid(x)


def _params(n_parallel=1):
    return pltpu.CompilerParams(dimension_semantics=("arbitrary",) * n_parallel,
                                vmem_limit_bytes=VMEM_LIMIT)


def _split2(x):
    hi = x.astype(BF16)
    lo = (x - hi.astype(F32)).astype(BF16)
    return hi, lo


def _dot(a, b):
    return jnp.dot(a, b, preferred_element_type=F32)


def _dot_nt(a, b):
    return lax.dot_general(a, b, (((1,), (1,)), ((), ())), preferred_element_type=F32)


def _dot_tn(a, b):
    return lax.dot_general(a, b, (((0,), (0,)), ((), ())), preferred_element_type=F32)


ADA_BLOCK = 1536


def _ada_kernel(cond_ref, w_ref, b_ref, o_ref):
    s = _silu(cond_ref[...])
    o_ref[...] = _dot(s.astype(BF16), w_ref[...].astype(BF16)) + b_ref[...]


def _ada_call(cond, ada_w, ada_b):
    n_out = ADA_CHUNKS * D_MODEL
    return pl.pallas_call(
        _ada_kernel,
        grid=(DEPTH, n_out // ADA_BLOCK),
        in_specs=[pl.BlockSpec((COND_ROWS, D_MODEL), lambda l, j: (0, 0)),
                  pl.BlockSpec((None, D_MODEL, ADA_BLOCK), lambda l, j: (l, 0, j)),
                  pl.BlockSpec((None, 1, ADA_BLOCK), lambda l, j: (l, 0, j))],
        out_specs=pl.BlockSpec((None, COND_ROWS, ADA_BLOCK), lambda l, j: (l, 0, j)),
        out_shape=jax.ShapeDtypeStruct((DEPTH, COND_ROWS, n_out), F32),
        compiler_params=_params(2),
        name="ada",
    )(cond, ada_w, ada_b.reshape(DEPTH, 1, n_out))


IN_TILE = 512
Z_MAIN_W = (N_IN_BLOCKS - 1) * GROUP_W


def _inproj_kernel(x_ref, ada_ref, g_ref, w_ref, z_ref, zs_ref, zsb_ref):
    x = x_ref[...]
    ada = ada_ref[...]
    sh1 = ada[:, 0:D_MODEL]
    sc1 = ada[:, D_MODEL:2 * D_MODEL]
    ms = jnp.mean(x * x, axis=-1, keepdims=True)
    h = x * lax.rsqrt(ms + EPS) * (g_ref[...] * (1.0 + sc1)) + sh1
    z = _dot(h.astype(BF16), w_ref[...])
    z_ref[...] = z[:, 0:Z_MAIN_W]
    zs = z[:, Z_MAIN_W:]
    zs_ref[...] = zs
    zsb_ref[...] = zs.astype(BF16)


def _inproj_call(x, ada4, g1, w_in, layer, cond_row):
    n_tok = x.shape[0]
    zw = N_IN_BLOCKS * GROUP_W
    tile = lambda w: pl.BlockSpec((IN_TILE, w), lambda i: (i, 0))
    return pl.pallas_call(
        _inproj_kernel,
        grid=(n_tok // IN_TILE,),
        in_specs=[pl.BlockSpec((IN_TILE, D_MODEL), lambda i: (i, 0)),
                  pl.BlockSpec((None, None, 1, ADA_CHUNKS * D_MODEL),
                               lambda i: (layer, cond_row(i * IN_TILE), 0, 0)),
                  pl.BlockSpec((None, 1, D_MODEL), lambda i: (layer, 0, 0)),
                  pl.BlockSpec((None, D_MODEL, zw), lambda i: (layer, 0, 0))],
        out_specs=[tile(Z_MAIN_W), tile(GROUP_W), tile(GROUP_W)],
        out_shape=[jax.ShapeDtypeStruct((n_tok, Z_MAIN_W), F32),
                   jax.ShapeDtypeStruct((n_tok, GROUP_W), F32),
                   jax.ShapeDtypeStruct((n_tok, GROUP_W), BF16)],
        compiler_params=_params(1),
        name="inproj",
    )(x, ada4, g1, w_in)


CONV_TILE = 32
CONV_HALO = 16
CONV_FILL = 128
CONV_NORM_TILE = 256


def _conv_kernel(z_ref, w_ref, b_ref, lng_ref, lnb_ref, o_ref, pad_ref, *, seq):
    tail = seq + 2 * CONV_HALO - (seq + SUBLANE)
    for r in range(SUBLANE):
        pad_ref[r, 0:CONV_HALO, :] = jnp.zeros((CONV_HALO, GROUP_W), F32)
        pad_ref[r, seq + SUBLANE:seq + 2 * CONV_HALO, :] = jnp.zeros((tail, GROUP_W), F32)
    for t0 in range(0, seq, CONV_FILL):
        u = z_ref[t0:t0 + CONV_FILL, 0:GROUP_W] * _sigmoid(z_ref[t0:t0 + CONV_FILL, GROUP_W:2 * GROUP_W])
        for r in range(SUBLANE):
            pad_ref[r, CONV_HALO - r + t0:CONV_HALO - r + t0 + CONV_FILL, :] = u
    first = CONV_HALO - CONV_PAD

    def tile(i, carry):
        r0 = pl.multiple_of(i * CONV_TILE, CONV_TILE)
        acc = jnp.zeros((CONV_TILE, GROUP_W), F32)
        for k in range(CONV_K):
            r = (first + k) % SUBLANE
            base = first + k - r
            wk = jnp.concatenate([w_ref[k]] * (CONV_TILE // SUBLANE), axis=0)
            acc = acc + wk * pad_ref[r, pl.ds(r0 + base, CONV_TILE), :]
        o_ref[pl.ds(r0, CONV_TILE), :] = acc + b_ref[...]
        return carry

    lax.fori_loop(0, seq // CONV_TILE, tile, 0)

    def norm(i, carry):
        rows = pl.ds(pl.multiple_of(i * CONV_NORM_TILE, CONV_NORM_TILE), CONV_NORM_TILE)
        c = o_ref[rows, :]
        mu = jnp.mean(c, axis=-1, keepdims=True)
        cc = c - mu
        var = jnp.mean(cc * cc, axis=-1, keepdims=True)
        y = cc * lax.rsqrt(var + EPS) * lng_ref[...] + lnb_ref[...]
        o_ref[rows, :] = _silu(y)
        return carry

    lax.fori_loop(0, seq // CONV_NORM_TILE, norm, 0)


def _conv_call(z, conv_w, conv_b, ln_g, ln_b, layer, n_seq, seq):
    vec = pl.BlockSpec((None, 1, GROUP_W), lambda s: (layer, 0, 0))
    return pl.pallas_call(
        functools.partial(_conv_kernel, seq=seq),
        grid=(n_seq,),
        in_specs=[pl.BlockSpec((seq, 2 * GROUP_W), lambda s: (s, 0)),
                  pl.BlockSpec((None, CONV_K, SUBLANE, GROUP_W), lambda s: (layer, 0, 0, 0)),
                  vec, vec, vec],
        out_specs=pl.BlockSpec((seq, GROUP_W), lambda s: (s, 0)),
        out_shape=jax.ShapeDtypeStruct((n_seq * seq, GROUP_W), F32),
        scratch_shapes=[pltpu.VMEM((SUBLANE, seq + 2 * CONV_HALO, GROUP_W), F32)],
        compiler_params=_params(1),
        name="conv",
    )(z, conv_w, conv_b, ln_g, ln_b)


def _head_mask(rows_per_head, cols_per_head, n_rows, n_cols):
    r = lax.broadcasted_iota(jnp.int32, (n_rows, n_cols), 0) // rows_per_head
    c = lax.broadcasted_iota(jnp.int32, (n_rows, n_cols), 1) // cols_per_head
    return r == c


def _gmlp_kernel(z_ref, g_ref, ws_ref, bs_ref, o_ref, *, seq):
    hd = GROUP_W // GM_HEADS
    mask = _head_mask(GM_CHUNK, hd, GM_HEADS * GM_CHUNK, GROUP_W)

    def chunk(i, carry):
        r0 = pl.multiple_of(i * GM_CHUNK, GM_CHUNK)
        u = z_ref[pl.ds(r0, GM_CHUNK), 0:GROUP_W]
        v = z_ref[pl.ds(r0, GM_CHUNK), GROUP_W:2 * GROUP_W]
        vn = v * lax.rsqrt(jnp.mean(v * v, axis=-1, keepdims=True) + EPS) * g_ref[...]
        stack = jnp.where(mask, jnp.concatenate([vn] * GM_HEADS, axis=0), 0.0).astype(BF16)
        sv = _dot(ws_ref[...], stack) + bs_ref[...]
        o_ref[pl.ds(r0, GM_CHUNK), :] = u * sv
        return carry

    lax.fori_loop(0, seq // GM_CHUNK, chunk, 0)


def _gmlp_call(z, norm_g, ws_cat, bs_full, layer, n_seq, seq):
    return pl.pallas_call(
        functools.partial(_gmlp_kernel, seq=seq),
        grid=(n_seq,),
        in_specs=[pl.BlockSpec((seq, 2 * GROUP_W), lambda s: (s, 1)),
                  pl.BlockSpec((None, 1, GROUP_W), lambda s: (layer, 0, 0)),
                  pl.BlockSpec((None, GM_CHUNK, GM_HEADS * GM_CHUNK), lambda s: (layer, 0, 0)),
                  pl.BlockSpec((None, GM_CHUNK, GROUP_W), lambda s: (layer, 0, 0))],
        out_specs=pl.BlockSpec((seq, GROUP_W), lambda s: (s, 0)),
        out_shape=jax.ShapeDtypeStruct((n_seq * seq, GROUP_W), F32),
        compiler_params=_params(1),
        name="gmlp",
    )(z, norm_g, ws_cat, bs_full)


def _hgrn_constants():
    n = HG_CHUNK
    blocks = []
    t = np.arange(n)
    cum = (t[None, :] <= t[:, None]).astype(np.float32)
    rem = (t[None, :] > t[:, None]).astype(np.float32)
    blocks += [cum, rem]
    upper = np.zeros((HG_LEVELS, n), np.float32)
    for lvl in range(HG_LEVELS):
        m = 1 << lvl
        e = np.zeros((n, n), np.float32)
        for row in range(n):
            r0 = (row & ~(2 * m - 1)) + m - 1
            if row & m:
                e[row, r0 + 1:row + 1] = 1.0
                upper[lvl, row] = 1.0
            else:
                e[row, row + 1:r0 + 1] = 1.0
        blocks.append(e)
    m_v = np.stack(blocks)
    m_all = np.stack([m_v, m_v[:, ::-1, ::-1]])
    up = np.stack([upper, upper[:, ::-1]])
    mq = np.repeat(up[..., None], GROUP_W, axis=-1)
    blk = np.zeros((HG_LEVELS, n, n), np.float32)
    for lvl in range(HG_LEVELS):
        b = t >> (lvl + 1)
        blk[lvl] = (b[:, None] == b[None, :])
    bmask = np.tile(blk, (1, 1, HG_HEADS))
    return (m_all.reshape(2, (2 + HG_LEVELS) * n, n), mq.astype(np.float32), bmask)


_HG_MALL, _HG_MQ, _HG_BMASK = _hgrn_constants()


def _hgrn_kernel(zq_ref, zi_ref, zg_ref, zff_ref, zfb_ref, lb_ref, ng_ref, s0_ref,
                 mall_ref, mq_ref, bmask_ref, o_ref, s_out_ref, ob_ref, *, seq):
    n = HG_CHUNK
    n_chunks = seq // n
    hmask = _head_mask(HG_DK, HG_DK, GROUP_W, GROUP_W)
    hmask_bf = jnp.where(hmask, 1.0, 0.0).astype(BF16)

    def block_diag(a):
        return jnp.where(hmask, jnp.concatenate([a] * HG_HEADS, axis=0), 0.0).astype(BF16)

    def chunk_step(direction, c, zf_ref, dst_ref):
        lb = lb_ref[direction:direction + 1, :]
        mall = mall_ref[direction]
        end_row = n - 1 if direction == 0 else 0
        rows = pl.ds(pl.multiple_of(c * n, n), n)
        q = _silu(zq_ref[rows, :])
        v = zi_ref[rows, :]
        f = lb + (1.0 - lb) * _sigmoid(zf_ref[rows, :])
        lf = jnp.log(jnp.maximum(f, F_MIN))
        k = 1.0 - f
        l1, l2 = _split2(lf)
        w = jnp.exp(_dot(mall, l1) + _dot(mall, l2))
        w_cum = w[0:n]
        w_rem = w[n:2 * n]
        w_end = w[end_row:end_row + 1]
        st = s_out_ref[direction]
        inter = _dot_nt((q * w_cum).astype(BF16), st.astype(BF16))
        scores = jnp.zeros((n, GROUP_W), F32)
        for lvl in range(HG_LEVELS):
            wl = w[(2 + lvl) * n:(3 + lvl) * n]
            mq = mq_ref[direction, lvl]
            ql = (q * wl * mq).astype(BF16)
            kl = k * wl * (1.0 - mq)
            scores = scores + bmask_ref[lvl] * _dot_nt(ql, block_diag(kl))
        diag = _dot((q * k).astype(BF16), hmask_bf)
        intra = _dot(scores.astype(BF16), block_diag(v)) + diag * v
        upd = _dot_tn(v.astype(BF16), (k * w_rem).astype(BF16))
        s_out_ref[direction] = st * w_end + jnp.where(hmask, upd, 0.0)
        dst_ref[rows, :] = inter + intra

    s_out_ref[...] = s0_ref[...]

    def both(i, carry):
        chunk_step(0, i, zff_ref, o_ref)
        chunk_step(1, n_chunks - 1 - i, zfb_ref, ob_ref)
        return carry

    lax.fori_loop(0, n_chunks, both, 0)

    def finish(i, carry):
        rows = pl.ds(pl.multiple_of(i * n, n), n)
        o = o_ref[rows, :] + ob_ref[rows, :]
        hi, mid = _split2(o * o)
        ms = (_dot(hi, hmask_bf) + _dot(mid, hmask_bf)) * (1.0 / HG_DK)
        y = o * lax.rsqrt(ms + EPS) * ng_ref[...]
        o_ref[rows, :] = y * _silu(zg_ref[rows, :])
        return carry

    lax.fori_loop(0, n_chunks, finish, 0)


def _hgrn_call(z, lb, norm_g, s0t, layer, n_seq, seq):
    def zcol(j):
        return pl.BlockSpec((seq, GROUP_W), lambda s: (s, j))

    n_rows = (2 + HG_LEVELS) * HG_CHUNK
    full = lambda shape: pl.BlockSpec(shape, lambda s: (0,) * len(shape))
    return pl.pallas_call(
        functools.partial(_hgrn_kernel, seq=seq),
        grid=(n_seq,),
        in_specs=[zcol(4), zcol(5), zcol(6), zcol(7), zcol(8),
                  pl.BlockSpec((None, 2, GROUP_W), lambda s: (layer, 0, 0)),
                  pl.BlockSpec((None, 1, GROUP_W), lambda s: (layer, 0, 0)),
                  pl.BlockSpec((None, 2, GROUP_W, GROUP_W), lambda s: (s, 0, 0, 0)),
                  full((2, n_rows, HG_CHUNK)),
                  full((2, HG_LEVELS, HG_CHUNK, GROUP_W)),
                  full((HG_LEVELS, HG_CHUNK, GROUP_W))],
        out_specs=[pl.BlockSpec((seq, GROUP_W), lambda s: (s, 0)),
                   pl.BlockSpec((None, 2, GROUP_W, GROUP_W), lambda s: (s, 0, 0, 0))],
        out_shape=[jax.ShapeDtypeStruct((n_seq * seq, GROUP_W), F32),
                   jax.ShapeDtypeStruct((n_seq, 2, GROUP_W, GROUP_W), F32)],
        scratch_shapes=[pltpu.VMEM((seq, GROUP_W), F32)],
        compiler_params=_params(1),
        name="hgrn",
    )(z, z, z, z, z, lb, norm_g, s0t,
      jnp.asarray(_HG_MALL, BF16), jnp.asarray(_HG_MQ), jnp.asarray(_HG_BMASK))


def _hgrn_state_to_kernel(s):
    st = jnp.swapaxes(s.astype(F32), -1, -2)
    eye = jnp.eye(HG_HEADS, dtype=F32)
    full = jnp.einsum('ndhvk,hg->ndhvgk', st, eye)
    return full.reshape(s.shape[0], 2, GROUP_W, GROUP_W)


def _hgrn_state_from_kernel(st):
    a = st.reshape(st.shape[0], 2, HG_HEADS, HG_DK, HG_HEADS, HG_DK)
    d = jnp.stack([a[:, :, h, :, h, :] for h in range(HG_HEADS)], axis=2)
    return jnp.swapaxes(d, -1, -2)


S5_BLOCK = 8
S5_ROW_W = S5_BLOCK * GROUP_W
S5_XW = 4 * S5_STATE
S5_COL_TILE = 1024
S5_OUT_TILE = 512
S5_STEPS = (1, 2, 4)
S5_TABLES = 2 * (len(S5_STEPS) + 1)


def _s5_in_kernel(u_ref, w_ref, x_ref):
    x_ref[...] = _dot(u_ref[...], w_ref[...])


def _s5_in_call(ub, wx, layer):
    n_blk = ub.shape[0]
    return pl.pallas_call(
        _s5_in_kernel,
        grid=(S5_XW // S5_COL_TILE,),
        in_specs=[pl.BlockSpec((n_blk, S5_ROW_W), lambda j: (0, 0)),
                  pl.BlockSpec((None, S5_ROW_W, S5_COL_TILE), lambda j: (layer, 0, j))],
        out_specs=pl.BlockSpec((n_blk, S5_COL_TILE), lambda j: (0, j)),
        out_shape=jax.ShapeDtypeStruct((n_blk, S5_XW), F32),
        compiler_params=_params(1),
        name="s5_in",
    )(ub, wx)


def _s5_scan_kernel(x_ref, tab_ref, s0re_ref, s0im_ref, h_ref, hre_ref, him_ref, *, n_blk):
    n_pairs = n_blk // (2 * SUBLANE)
    row = lax.broadcasted_iota(jnp.int32, (SUBLANE, S5_STATE), 0)
    n_steps = len(S5_STEPS)

    for direction in range(2):
        c_re = 2 * direction * S5_STATE
        c_im = c_re + S5_STATE
        edge = 0 if direction == 0 else SUBLANE - 1
        last = SUBLANE - 1 - edge
        unit = 0 if direction == 0 else SUBLANE - 1

        def tile(rows, carry):
            hc_r, hc_i, px_r, px_i = carry
            xr = x_ref[rows, c_re:c_re + S5_STATE]
            xi = x_ref[rows, c_im:c_im + S5_STATE]
            shift = 1 if direction == 0 else SUBLANE - 1
            hr = jnp.where(row == edge, px_r, pltpu.roll(xr, shift, 0))
            hi = jnp.where(row == edge, px_i, pltpu.roll(xi, shift, 0))
            for s, step in enumerate(S5_STEPS):
                sh = step if direction == 0 else SUBLANE - step
                rr = pltpu.roll(hr, sh, 0)
                ri = pltpu.roll(hi, sh, 0)
                ar = tab_ref[direction, 2 * s]
                ai = tab_ref[direction, 2 * s + 1]
                hr, hi = hr + ar * rr - ai * ri, hi + ar * ri + ai * rr
            pr = tab_ref[direction, 2 * n_steps]
            pi = tab_ref[direction, 2 * n_steps + 1]
            hr, hi = hr + pr * hc_r - pi * hc_i, hi + pr * hc_i + pi * hc_r
            bc = lambda a: jnp.broadcast_to(a[last:last + 1, :], (SUBLANE, S5_STATE))
            return hr, hi, (bc(hr), bc(hi), bc(xr), bc(xi))

        def pair(j, carry):
            jj = j if direction == 0 else n_pairs - 1 - j
            base = pl.multiple_of(jj * 2 * SUBLANE, 2 * SUBLANE)
            offs = (0, SUBLANE) if direction == 0 else (SUBLANE, 0)
            out = {}
            for off in offs:
                hr, hi, carry = tile(pl.ds(base + off, SUBLANE), carry)
                out[off] = (hr, hi)
            both = pl.ds(base, 2 * SUBLANE)
            h_ref[both, c_re:c_re + S5_STATE] = jnp.concatenate([out[0][0], out[SUBLANE][0]], 0).astype(BF16)
            h_ref[both, c_im:c_im + S5_STATE] = jnp.concatenate([out[0][1], out[SUBLANE][1]], 0).astype(BF16)
            return carry

        zero = jnp.zeros((SUBLANE, S5_STATE), F32)
        init = (zero, zero,
                jnp.broadcast_to(s0re_ref[direction], (SUBLANE, S5_STATE)),
                jnp.broadcast_to(s0im_ref[direction], (SUBLANE, S5_STATE)))
        hc_r, hc_i, px_r, px_i = lax.fori_loop(0, n_pairs, pair, init)
        a_r = tab_ref[direction, 2 * n_steps][unit:unit + 1, :]
        a_i = tab_ref[direction, 2 * n_steps + 1][unit:unit + 1, :]
        hre_ref[direction] = a_r * hc_r[0:1, :] - a_i * hc_i[0:1, :] + px_r[0:1, :]
        him_ref[direction] = a_r * hc_i[0:1, :] + a_i * hc_r[0:1, :] + px_i[0:1, :]


def _s5_scan_call(x, tab, s0re, s0im, layer, n_seq, n_blk):
    state = pl.BlockSpec((None, 2, 1, S5_STATE), lambda s: (s, 0, 0, 0))
    return pl.pallas_call(
        functools.partial(_s5_scan_kernel, n_blk=n_blk),
        grid=(n_seq,),
        in_specs=[pl.BlockSpec((n_blk, S5_XW), lambda s: (s, 0)),
                  pl.BlockSpec((None, 2, S5_TABLES, SUBLANE, S5_STATE), lambda s: (layer, 0, 0, 0, 0)),
                  state, state],
        out_specs=[pl.BlockSpec((n_blk, S5_XW), lambda s: (s, 0)), state, state],
        out_shape=[jax.ShapeDtypeStruct((n_seq * n_blk, S5_XW), BF16),
                   jax.ShapeDtypeStruct((n_seq, 2, 1, S5_STATE), F32),
                   jax.ShapeDtypeStruct((n_seq, 2, 1, S5_STATE), F32)],
        compiler_params=_params(1),
        name="s5_scan",
    )(x, tab, s0re, s0im)


def _s5_out_kernel(u_ref, h_ref, wt_ref, wc_ref, y_ref):
    y_ref[...] = _dot(u_ref[...], wt_ref[...]) + _dot(h_ref[...], wc_ref[...])


def _s5_out_call(ub, h, wt, wc, layer):
    n_blk = ub.shape[0]
    return pl.pallas_call(
        _s5_out_kernel,
        grid=(S5_ROW_W // S5_OUT_TILE,),
        in_specs=[pl.BlockSpec((n_blk, S5_ROW_W), lambda j: (0, 0)),
                  pl.BlockSpec((n_blk, S5_XW), lambda j: (0, 0)),
                  pl.BlockSpec((None, S5_ROW_W, S5_OUT_TILE), lambda j: (layer, 0, j)),
                  pl.BlockSpec((None, S5_XW, S5_OUT_TILE), lambda j: (layer, 0, j))],
        out_specs=pl.BlockSpec((n_blk, S5_OUT_TILE), lambda j: (0, j)),
        out_shape=jax.ShapeDtypeStruct((n_blk, S5_ROW_W), F32),
        compiler_params=_params(1),
        name="s5_out",
    )(ub, h, wt, wc)


def _s5_mix(zs_bf, consts, s0re, s0im, layer, n_seq, seq):
    wx, wt, wc, tab = consts
    n_blk = seq // S5_BLOCK
    ub = zs_bf.reshape(n_seq * n_blk, S5_ROW_W)
    x = _s5_in_call(ub, wx, layer)
    h, h_re, h_im = _s5_scan_call(x, tab, s0re, s0im, layer, n_seq, n_blk)
    y = _s5_out_call(ub, h, wt, wc, layer)
    return y.reshape(n_seq * seq, GROUP_W), h_re, h_im


def _s5_constants(a_re, a_im, log_dt, b_re, b_im, c_re, c_im):
    n = S5_BLOCK
    a = lax.complex(a_re.astype(F32), a_im.astype(F32))
    dt = jnp.exp(log_dt.astype(F32))[..., None]
    a_bar = jnp.exp(a * dt)
    b_bar = ((a_bar - 1.0) / a)[..., None] * lax.complex(b_re.astype(F32), b_im.astype(F32))
    c = lax.complex(c_re.astype(F32), c_im.astype(F32))
    pw = [jnp.ones_like(a_bar)]
    for _ in range(n):
        pw.append(pw[-1] * a_bar)
    pw = jnp.stack(pw, axis=2)
    eye = jnp.eye(S5_GROUPS, dtype=F32)
    idx = jnp.arange(n)

    w_f = pw[:, 0, ::-1][:, 1:, :, :, None] * b_bar[:, 0, None]
    w_b = pw[:, 1, :n, :, :, None] * b_bar[:, 1, None]
    def x_mat(w):
        return jnp.einsum('ljgpc,gh->ljgchp', w, eye).reshape(DEPTH, S5_ROW_W, S5_STATE)
    wx = jnp.concatenate([x_mat(jnp.real(w_f)), x_mat(jnp.imag(w_f)),
                          x_mat(jnp.real(w_b)), x_mat(jnp.imag(w_b))], axis=-1).astype(BF16)

    o_f = c[:, 0, None] * pw[:, 0, 1:, :, None, :]
    o_b = c[:, 1, None] * pw[:, 1, ::-1][:, :n, :, None, :]
    def c_mat(o):
        return jnp.einsum('ligcp,gh->lgpihc', o, eye).reshape(DEPTH, S5_STATE, S5_ROW_W)
    wc = jnp.concatenate([c_mat(jnp.real(o_f)), c_mat(-jnp.imag(o_f)),
                          c_mat(jnp.real(o_b)), c_mat(-jnp.imag(o_b))], axis=1).astype(BF16)

    k_f = jnp.real(jnp.einsum('lgcp,ltgp,lgpd->ltgdc', c[:, 0], pw[:, 0, :n], b_bar[:, 0]))
    k_b = jnp.real(jnp.einsum('lgcp,ltgp,lgpd->ltgdc', c[:, 1], pw[:, 1, :n], b_bar[:, 1]))
    lag = idx[None, :] - idx[:, None]
    t_f = jnp.where((lag >= 0)[None, :, :, None, None, None], k_f[:, jnp.clip(lag, 0, n - 1)], 0.0)
    t_b = jnp.where((lag <= 0)[None, :, :, None, None, None], k_b[:, jnp.clip(-lag, 0, n - 1)], 0.0)
    wt = jnp.einsum('ljigdc,gh->ljgdihc', t_f + t_b, eye).reshape(DEPTH, S5_ROW_W, S5_ROW_W).astype(BF16)

    a_blk = pw[:, :, n].reshape(DEPTH, 2, S5_STATE)
    pows = [a_blk]
    for _ in range(SUBLANE - 1):
        pows.append(pows[-1] * a_blk)
    row = jnp.arange(SUBLANE)
    tabs = []
    for direction in range(2):
        t = []
        for step in S5_STEPS:
            live = (row >= step) if direction == 0 else (row < SUBLANE - step)
            t.append(jnp.where(live[None, :, None], pows[step - 1][:, direction, None, :], 0.0))
        order = range(SUBLANE) if direction == 0 else range(SUBLANE - 1, -1, -1)
        t.append(jnp.stack([pows[i][:, direction] for i in order], axis=1))
        planes = []
        for x in t:
            planes += [jnp.real(x), jnp.imag(x)]
        tabs.append(jnp.stack(planes, axis=1))
    tab = jnp.stack(tabs, axis=1).astype(F32)
    return wx, wt, wc, tab


OUT_TILE = 256
FF_BLOCK = 1024


def _outproj_kernel(x_ref, oa_ref, ob_ref, oc_ref, ys_ref, zs_ref, sd_ref, gw_ref, gb_ref,
                    ada_ref, g2_ref, wo_ref, w1_ref, w2_ref, gf_ref, *out_refs, final):
    ada = ada_ref[...]
    chunk = lambda j: ada[:, j * D_MODEL:(j + 1) * D_MODEL]
    gate1, sh2, sc2, gate2 = chunk(2), chunk(3), chunk(4), chunk(5)
    ys = jax.nn.gelu(ys_ref[...] + sd_ref[...] * zs_ref[...])
    od = ys * _sigmoid(_dot(ys.astype(BF16), gw_ref[...]) + gb_ref[...])
    mixed = jnp.concatenate([oa_ref[...], ob_ref[...], oc_ref[...], od], axis=-1).astype(BF16)
    x = x_ref[...] + gate1 * _dot(mixed, wo_ref[...])
    ms = jnp.mean(x * x, axis=-1, keepdims=True)
    h = (x * lax.rsqrt(ms + EPS) * (g2_ref[...] * (1.0 + sc2)) + sh2).astype(BF16)
    acc = jnp.zeros((OUT_TILE, D_MODEL), F32)
    for j in range(D_FF // FF_BLOCK):
        cols = slice(j * FF_BLOCK, (j + 1) * FF_BLOCK)
        a = jnp.maximum(_dot(h, w1_ref[:, cols]), 0.0)
        acc = acc + _dot((a * a).astype(BF16), w2_ref[cols, :])
    x = x + gate2 * acc
    out_refs[0][...] = x
    if final:
        ms = jnp.mean(x * x, axis=-1, keepdims=True)
        out_refs[1][...] = x * lax.rsqrt(ms + EPS) * gf_ref[...]


def _outproj_call(x, mix, s5_par, ada4, g2, w_out, w1, w2, gf, layer, cond_row, final):
    n_tok = x.shape[0]
    tok = pl.BlockSpec((OUT_TILE, D_MODEL), lambda i: (i, 0))
    grp = pl.BlockSpec((OUT_TILE, GROUP_W), lambda i: (i, 0))
    lay = lambda shape: pl.BlockSpec((None,) + shape, lambda i: (layer,) + (0,) * len(shape))
    n_out = 2 if final else 1
    res = pl.pallas_call(
        functools.partial(_outproj_kernel, final=final),
        grid=(n_tok // OUT_TILE,),
        in_specs=[tok, grp, grp, grp, grp, grp,
                  lay((1, GROUP_W)), lay((GROUP_W, GROUP_W)), lay((1, GROUP_W)),
                  pl.BlockSpec((None, None, 1, ADA_CHUNKS * D_MODEL),
                               lambda i: (layer, cond_row(i * OUT_TILE), 0, 0)),
                  lay((1, D_MODEL)), lay((D_MODEL, D_MODEL)), lay((D_MODEL, D_FF)), lay((D_FF, D_MODEL)),
                  pl.BlockSpec((1, D_MODEL), lambda i: (0, 0))],
        out_specs=[tok] * n_out,
        out_shape=[jax.ShapeDtypeStruct((n_tok, D_MODEL), F32)] * n_out,
        compiler_params=_params(1),
        name="outproj",
    )(x, *mix, *s5_par, ada4, g2, w_out, w1, w2, gf)
    return res


def _grid_pos_embed(n_tokens):
    rows = n_tokens // GRID_W
    r, col = jnp.meshgrid(jnp.arange(rows, dtype=F32), jnp.arange(GRID_W, dtype=F32), indexing='ij')
    r = r.reshape(-1)
    col = col.reshape(-1)
    quarter = D_MODEL // 4
    freq = jnp.exp(-math.log(POS_BASE) * jnp.arange(quarter, dtype=F32) / quarter)
    ar = r[:, None] * freq
    ac = col[:, None] * freq
    return jnp.concatenate([jnp.sin(ar), jnp.cos(ar), jnp.sin(ac), jnp.cos(ac)], axis=-1)


def _run_stream(x, n_seq, seq, cond_row, hg_s0, s5_s0re, s5_s0im, P):
    hg_states, s5_re, s5_im = [], [], []
    y = None
    for l in range(DEPTH):
        z, zs, zs_bf = _inproj_call(x, P['ada4'], P['norm1_g'], P['w_in'], l, cond_row)
        o_conv = _conv_call(z, P['conv_w'], P['conv_b'], P['conv_ln_g'], P['conv_ln_b'], l, n_seq, seq)
        o_gm = _gmlp_call(z, P['gmlp_norm_g'], P['gmlp_ws'], P['gmlp_bs'], l, n_seq, seq)
        o_hg, hg_end = _hgrn_call(z, P['hgrn_lb'], P['hgrn_norm_g'], hg_s0[l], l, n_seq, seq)
        y_s5, h_re, h_im = _s5_mix(zs_bf, P['s5'], s5_s0re[l], s5_s0im[l], l, n_seq, seq)
        final = l == DEPTH - 1
        res = _outproj_call(x, (o_conv, o_gm, o_hg, y_s5, zs), (P['s5_d'], P['s5_glu_w'], P['s5_glu_b']),
                            P['ada4'], P['norm2_g'], P['w_out'],
                            P['mlp_w1'], P['mlp_w2'], P['final_norm_g'], l, cond_row, final)
        x = res[0]
        if final:
            y = res[1]
        hg_states.append(hg_end)
        s5_re.append(h_re)
        s5_im.append(h_im)
    return y, hg_states, s5_re, s5_im


def kernel(x_prompt, x_sample, state_hgrn, state_s5_re, state_s5_im, c, c_ctx, norm1_g, norm2_g, ada_w, ada_b, w_in, conv_w, conv_b, conv_ln_g, conv_ln_b, gmlp_norm_g, gmlp_ws, gmlp_bs, hgrn_lb_logits, hgrn_norm_g, s5_a_re, s5_a_im, s5_log_dt, s5_b_re, s5_b_im, s5_c_re, s5_c_im, s5_d, s5_glu_w, s5_glu_b, w_out, mlp_w1, mlp_w2, final_norm_g):
    n_ctx, seq_ctx, _ = x_prompt.shape
    n_lat, seq_lat, _ = x_sample.shape
    assert n_lat + 1 <= COND_ROWS
    assert seq_ctx % IN_TILE == 0 or IN_TILE % seq_ctx == 0

    lb_soft = jax.nn.softmax(hgrn_lb_logits.astype(F32), axis=1)
    hgrn_lb = jnp.cumsum(lb_soft, axis=1) - lb_soft[:, :1]
    vec = lambda a: a.astype(F32).reshape(DEPTH, 1, -1)
    hd = GROUP_W // GM_HEADS
    P = {
        'norm1_g': vec(norm1_g), 'norm2_g': vec(norm2_g),
        'w_in': w_in.astype(BF16), 'w_out': w_out.astype(BF16),
        'mlp_w1': mlp_w1.astype(BF16), 'mlp_w2': mlp_w2.astype(BF16),
        'conv_w': jnp.broadcast_to(conv_w.astype(F32)[:, :, None, :], (DEPTH, CONV_K, SUBLANE, GROUP_W)), 'conv_b': vec(conv_b), 'conv_ln_g': vec(conv_ln_g),
        'conv_ln_b': vec(conv_ln_b),
        'gmlp_norm_g': vec(gmlp_norm_g),
        'gmlp_ws': jnp.transpose(gmlp_ws, (0, 2, 1, 3)).reshape(DEPTH, GM_CHUNK, GM_HEADS * GM_CHUNK).astype(BF16),
        'gmlp_bs': jnp.repeat(jnp.transpose(gmlp_bs.astype(F32), (0, 2, 1)), hd, axis=2),
        'hgrn_lb': jnp.transpose(hgrn_lb, (1, 0, 2)), 'hgrn_norm_g': vec(hgrn_norm_g),
        's5': _s5_constants(s5_a_re, s5_a_im, s5_log_dt, s5_b_re, s5_b_im, s5_c_re, s5_c_im),
        's5_d': vec(s5_d), 's5_glu_w': s5_glu_w.astype(BF16), 's5_glu_b': vec(s5_glu_b),
        'final_norm_g': final_norm_g.astype(F32).reshape(1, D_MODEL),
    }
    cond = jnp.zeros((COND_ROWS, D_MODEL), F32).at[0].set(c_ctx.astype(F32)).at[1:1 + n_lat].set(c.astype(F32))
    ada = _ada_call(cond, ada_w.astype(F32), ada_b.astype(F32))
    P['ada4'] = ada.reshape(DEPTH, COND_ROWS, 1, ADA_CHUNKS * D_MODEL)

    zeros_hg = jnp.zeros((n_ctx, 2, GROUP_W, GROUP_W), F32)
    zeros_s5 = jnp.zeros((n_ctx, 2, 1, S5_STATE), F32)
    y_ctx, hg_ctx, re_ctx, im_ctx = _run_stream(
        x_prompt.astype(F32).reshape(n_ctx * seq_ctx, D_MODEL), n_ctx, seq_ctx, lambda r: 0,
        [zeros_hg] * DEPTH, [zeros_s5] * DEPTH, [zeros_s5] * DEPTH, P)

    xs = x_sample.astype(F32) + _grid_pos_embed(seq_lat)[None]
    hg0 = [_hgrn_state_to_kernel(state_hgrn[:, l]) for l in range(DEPTH)]
    re0 = [state_s5_re[:, l].astype(F32).reshape(n_lat, 2, 1, S5_STATE) for l in range(DEPTH)]
    im0 = [state_s5_im[:, l].astype(F32).reshape(n_lat, 2, 1, S5_STATE) for l in range(DEPTH)]
    y_lat, _, _, _ = _run_stream(xs.reshape(n_lat * seq_lat, D_MODEL), n_lat, seq_lat,
                                 lambda r: 1 + r // seq_lat, hg0, re0, im0, P)

    dt = x_prompt.dtype
    new_hg = jnp.stack([_hgrn_state_from_kernel(s) for s in hg_ctx], axis=1).astype(dt)
    new_re = jnp.stack([s.reshape(n_ctx, 2, S5_GROUPS, S5_P) for s in re_ctx], axis=1).astype(dt)
    new_im = jnp.stack([s.reshape(n_ctx, 2, S5_GROUPS, S5_P) for s in im_ctx], axis=1).astype(dt)
    return (y_ctx.reshape(n_ctx, seq_ctx, D_MODEL).astype(dt),
            y_lat.reshape(n_lat, seq_lat, D_MODEL).astype(x_sample.dtype), new_hg, new_re, new_im)
```

```python
import functools
import math

import numpy as np
import jax
import jax.numpy as jnp
from jax import lax
from jax.experimental import pallas as pl
from jax.experimental.pallas import tpu as pltpu

D_MODEL = 1024
DEPTH = 4
GRID_W = 64
GROUP_W = 256
N_IN_BLOCKS = 10
D_FF = 4 * D_MODEL
EPS = 1e-6
F_MIN = 1e-30
ADA_CHUNKS = 6
CONV_K = 31
CONV_PAD = CONV_K // 2
GM_CHUNK = 128
GM_HEADS = 4
HG_HEADS = 4
HG_DK = 64
HG_CHUNK = 64
HG_LEVELS = 6
S5_IN = 16
S5_GROUPS = 16
S5_P = 64
S5_STATE = S5_GROUPS * S5_P
POS_BASE = 10000.0

SUBLANE = 8
COND_ROWS = 8
VMEM_LIMIT = 56 * 1024 * 1024

F32 = jnp.float32
BF16 = jnp.bfloat16


def _sigmoid(x):
    return 1.0 / (1.0 + jnp.exp(-x))


def _silu(x):
    return x * _sigmoid(x)


def _params(n_parallel=1):
    return pltpu.CompilerParams(dimension_semantics=("arbitrary",) * n_parallel,
                                vmem_limit_bytes=VMEM_LIMIT)


def _split2(x):
    hi = x.astype(BF16)
    lo = (x - hi.astype(F32)).astype(BF16)
    return hi, lo


def _dot(a, b):
    return jnp.dot(a, b, preferred_element_type=F32)


def _dot_nt(a, b):
    return lax.dot_general(a, b, (((1,), (1,)), ((), ())), preferred_element_type=F32)


def _dot_tn(a, b):
    return lax.dot_general(a, b, (((0,), (0,)), ((), ())), preferred_element_type=F32)


ADA_BLOCK = 1536


def _ada_kernel(cond_ref, w_ref, b_ref, o_ref):
    s = _silu(cond_ref[...])
    o_ref[...] = _dot(s.astype(BF16), w_ref[...].astype(BF16)) + b_ref[...]


def _ada_call(cond, ada_w, ada_b):
    n_out = ADA_CHUNKS * D_MODEL
    return pl.pallas_call(
        _ada_kernel,
        grid=(DEPTH, n_out // ADA_BLOCK),
        in_specs=[pl.BlockSpec((COND_ROWS, D_MODEL), lambda l, j: (0, 0)),
                  pl.BlockSpec((None, D_MODEL, ADA_BLOCK), lambda l, j: (l, 0, j)),
                  pl.BlockSpec((None, 1, ADA_BLOCK), lambda l, j: (l, 0, j))],
        out_specs=pl.BlockSpec((None, COND_ROWS, ADA_BLOCK), lambda l, j: (l, 0, j)),
        out_shape=jax.ShapeDtypeStruct((DEPTH, COND_ROWS, n_out), F32),
        compiler_params=_params(2),
        name="ada",
    )(cond, ada_w, ada_b.reshape(DEPTH, 1, n_out))


IN_TILE = 512
Z_MAIN_W = (N_IN_BLOCKS - 1) * GROUP_W


def _inproj_kernel(x_ref, ada_ref, g_ref, w_ref, z_ref, zs_ref, zsb_ref):
    x = x_ref[...]
    ada = ada_ref[...]
    sh1 = ada[:, 0:D_MODEL]
    sc1 = ada[:, D_MODEL:2 * D_MODEL]
    ms = jnp.mean(x * x, axis=-1, keepdims=True)
    h = x * lax.rsqrt(ms + EPS) * (g_ref[...] * (1.0 + sc1)) + sh1
    z = _dot(h.astype(BF16), w_ref[...])
    z_ref[...] = z[:, 0:Z_MAIN_W]
    zs = z[:, Z_MAIN_W:]
    zs_ref[...] = zs
    zsb_ref[...] = zs.astype(BF16)


def _inproj_call(x, ada4, g1, w_in, layer, cond_row):
    n_tok = x.shape[0]
    zw = N_IN_BLOCKS * GROUP_W
    tile = lambda w: pl.BlockSpec((IN_TILE, w), lambda i: (i, 0))
    return pl.pallas_call(
        _inproj_kernel,
        grid=(n_tok // IN_TILE,),
        in_specs=[pl.BlockSpec((IN_TILE, D_MODEL), lambda i: (i, 0)),
                  pl.BlockSpec((None, None, 1, ADA_CHUNKS * D_MODEL),
                               lambda i: (layer, cond_row(i * IN_TILE), 0, 0)),
                  pl.BlockSpec((None, 1, D_MODEL), lambda i: (layer, 0, 0)),
                  pl.BlockSpec((None, D_MODEL, zw), lambda i: (layer, 0, 0))],
        out_specs=[tile(Z_MAIN_W), tile(GROUP_W), tile(GROUP_W)],
        out_shape=[jax.ShapeDtypeStruct((n_tok, Z_MAIN_W), F32),
                   jax.ShapeDtypeStruct((n_tok, GROUP_W), F32),
                   jax.ShapeDtypeStruct((n_tok, GROUP_W), BF16)],
        compiler_params=_params(1),
        name="inproj",
    )(x, ada4, g1, w_in)


CONV_TILE = 32
CONV_HALO = 16
CONV_FILL = 128
CONV_NORM_TILE = 256


def _conv_kernel(z_ref, w_ref, b_ref, lng_ref, lnb_ref, o_ref, pad_ref, *, seq):
    tail = seq + 2 * CONV_HALO - (seq + SUBLANE)
    for r in range(SUBLANE):
        pad_ref[r, 0:CONV_HALO, :] = jnp.zeros((CONV_HALO, GROUP_W), F32)
        pad_ref[r, seq + SUBLANE:seq + 2 * CONV_HALO, :] = jnp.zeros((tail, GROUP_W), F32)
    for t0 in range(0, seq, CONV_FILL):
        u = z_ref[t0:t0 + CONV_FILL, 0:GROUP_W] * _sigmoid(z_ref[t0:t0 + CONV_FILL, GROUP_W:2 * GROUP_W])
        for r in range(SUBLANE):
            pad_ref[r, CONV_HALO - r + t0:CONV_HALO - r + t0 + CONV_FILL, :] = u
    first = CONV_HALO - CONV_PAD

    def tile(i, carry):
        r0 = pl.multiple_of(i * CONV_TILE, CONV_TILE)
        acc = jnp.zeros((CONV_TILE, GROUP_W), F32)
        for k in range(CONV_K):
            r = (first + k) % SUBLANE
            base = first + k - r
            wk = jnp.concatenate([w_ref[k]] * (CONV_TILE // SUBLANE), axis=0)
            acc = acc + wk * pad_ref[r, pl.ds(r0 + base, CONV_TILE), :]
        o_ref[pl.ds(r0, CONV_TILE), :] = acc + b_ref[...]
        return carry

    lax.fori_loop(0, seq // CONV_TILE, tile, 0)

    def norm(i, carry):
        rows = pl.ds(pl.multiple_of(i * CONV_NORM_TILE, CONV_NORM_TILE), CONV_NORM_TILE)
        c = o_ref[rows, :]
        mu = jnp.mean(c, axis=-1, keepdims=True)
        cc = c - mu
        var = jnp.mean(cc * cc, axis=-1, keepdims=True)
        y = cc * lax.rsqrt(var + EPS) * lng_ref[...] + lnb_ref[...]
        o_ref[rows, :] = _silu(y)
        return carry

    lax.fori_loop(0, seq // CONV_NORM_TILE, norm, 0)


def _conv_call(z, conv_w, conv_b, ln_g, ln_b, layer, n_seq, seq):
    vec = pl.BlockSpec((None, 1, GROUP_W), lambda s: (layer, 0, 0))
    return pl.pallas_call(
        functools.partial(_conv_kernel, seq=seq),
        grid=(n_seq,),
        in_specs=[pl.BlockSpec((seq, 2 * GROUP_W), lambda s: (s, 0)),
                  pl.BlockSpec((None, CONV_K, SUBLANE, GROUP_W), lambda s: (layer, 0, 0, 0)),
                  vec, vec, vec],
        out_specs=pl.BlockSpec((seq, GROUP_W), lambda s: (s, 0)),
        out_shape=jax.ShapeDtypeStruct((n_seq * seq, GROUP_W), F32),
        scratch_shapes=[pltpu.VMEM((SUBLANE, seq + 2 * CONV_HALO, GROUP_W), F32)],
        compiler_params=_params(1),
        name="conv",
    )(z, conv_w, conv_b, ln_g, ln_b)


def _head_mask(rows_per_head, cols_per_head, n_rows, n_cols):
    r = lax.broadcasted_iota(jnp.int32, (n_rows, n_cols), 0) // rows_per_head
    c = lax.broadcasted_iota(jnp.int32, (n_rows, n_cols), 1) // cols_per_head
    return r == c


def _gmlp_kernel(z_ref, g_ref, ws_ref, bs_ref, o_ref, *, seq):
    hd = GROUP_W // GM_HEADS
    mask = _head_mask(GM_CHUNK, hd, GM_HEADS * GM_CHUNK, GROUP_W)

    def chunk(i, carry):
        r0 = pl.multiple_of(i * GM_CHUNK, GM_CHUNK)
        u = z_ref[pl.ds(r0, GM_CHUNK), 0:GROUP_W]
        v = z_ref[pl.ds(r0, GM_CHUNK), GROUP_W:2 * GROUP_W]
        vn = v * lax.rsqrt(jnp.mean(v * v, axis=-1, keepdims=True) + EPS) * g_ref[...]
        stack = jnp.where(mask, jnp.concatenate([vn] * GM_HEADS, axis=0), 0.0).astype(BF16)
        sv = _dot(ws_ref[...], stack) + bs_ref[...]
        o_ref[pl.ds(r0, GM_CHUNK), :] = u * sv
        return carry

    lax.fori_loop(0, seq // GM_CHUNK, chunk, 0)


def _gmlp_call(z, norm_g, ws_cat, bs_full, layer, n_seq, seq):
    return pl.pallas_call(
        functools.partial(_gmlp_kernel, seq=seq),
        grid=(n_seq,),
        in_specs=[pl.BlockSpec((seq, 2 * GROUP_W), lambda s: (s, 1)),
                  pl.BlockSpec((None, 1, GROUP_W), lambda s: (layer, 0, 0)),
                  pl.BlockSpec((None, GM_CHUNK, GM_HEADS * GM_CHUNK), lambda s: (layer, 0, 0)),
                  pl.BlockSpec((None, GM_CHUNK, GROUP_W), lambda s: (layer, 0, 0))],
        out_specs=pl.BlockSpec((seq, GROUP_W), lambda s: (s, 0)),
        out_shape=jax.ShapeDtypeStruct((n_seq * seq, GROUP_W), F32),
        compiler_params=_params(1),
        name="gmlp",
    )(z, norm_g, ws_cat, bs_full)


def _hgrn_block_masks():
    t = np.arange(HG_CHUNK)
    blk = np.zeros((HG_LEVELS, HG_CHUNK, HG_CHUNK), np.float32)
    for lvl in range(HG_LEVELS):
        b = t >> (lvl + 1)
        blk[lvl] = (b[:, None] == b[None, :])
    return np.tile(blk, (1, 1, HG_HEADS))


_HG_BMASK = _hgrn_block_masks()
HG_NORM_TILE = 256


def _hgrn_kernel(zq_ref, zi_ref, zg_ref, zff_ref, zfb_ref, lb_ref, ng_ref, s0_ref,
                 bmask_ref, o_ref, s_out_ref, ob_ref, st_ref, *, seq):
    n = HG_CHUNK
    n_chunks = seq // n
    hmask = _head_mask(HG_DK, HG_DK, GROUP_W, GROUP_W)
    hmask_bf = jnp.where(hmask, 1.0, 0.0).astype(BF16)
    row = lax.broadcasted_iota(jnp.int32, (n, GROUP_W), 0)

    def block_diag(a):
        return jnp.where(hmask, jnp.concatenate([a] * HG_HEADS, axis=0), 0.0).astype(BF16)

    def spread_rows(a, first, period):
        return jnp.concatenate([jnp.broadcast_to(a[r:r + 1, :], (period, GROUP_W))
                                for r in range(first, n, period)], axis=0)

    def boundary(cum, lvl, direction):
        m = 1 << lvl
        at = m - 1 if direction == 0 else m
        if 2 * m >= SUBLANE:
            return spread_rows(cum, at, 2 * m)
        lo = spread_rows(cum, at, SUBLANE)
        hi = spread_rows(cum, at + 2 * m, SUBLANE)
        return jnp.where((row & (2 * m)) == 0, lo, hi)

    def chunk_step(direction, c, zf_ref, dst_ref):
        lb = lb_ref[direction:direction + 1, :]
        end_row = n - 1 if direction == 0 else 0
        pos = row if direction == 0 else n - 1 - row
        rows = pl.ds(pl.multiple_of(c * n, n), n)
        q = _silu(zq_ref[rows, :])
        v = zi_ref[rows, :]
        f = lb + (1.0 - lb) * _sigmoid(zf_ref[rows, :])
        k = 1.0 - f
        f = jnp.maximum(f, F_MIN)
        cum = jnp.log(f)
        d = 1
        while d < n:
            cum = cum + jnp.where(pos >= d, pltpu.roll(cum, d if direction == 0 else n - d, 0), 0.0)
            d *= 2
        total = cum[end_row:end_row + 1]
        w_cum = jnp.exp(cum)
        w_rem = jnp.exp(total - cum)
        w_end = jnp.exp(total)
        st = st_ref[direction]
        inter = _dot_nt((q * w_cum).astype(BF16), st.astype(BF16))
        scores = jnp.zeros((n, GROUP_W), F32)
        for lvl in range(HG_LEVELS):
            upper = (pos & (1 << lvl)) != 0
            if lvl == 0:
                wl = jnp.where(upper, f, 1.0)
            else:
                ref = boundary(cum, lvl, direction)
                wl = jnp.exp(jnp.where(upper, cum - ref, ref - cum))
            ql = jnp.where(upper, q * wl, 0.0).astype(BF16)
            kl = jnp.where(upper, 0.0, k * wl)
            scores = scores + bmask_ref[lvl] * _dot_nt(ql, block_diag(kl))
        diag = _dot((q * k).astype(BF16), hmask_bf)
        intra = _dot(scores.astype(BF16), block_diag(v)) + diag * v
        upd = _dot_tn(v.astype(BF16), (k * w_rem).astype(BF16))
        st_ref[direction] = st * w_end + jnp.where(hmask, upd, 0.0)
        dst_ref[rows, :] = inter + intra

    st_ref[...] = s0_ref[...]

    def both(i, carry):
        chunk_step(0, i, zff_ref, o_ref)
        chunk_step(1, n_chunks - 1 - i, zfb_ref, ob_ref)
        return carry

    lax.fori_loop(0, n_chunks, both, 0)

    pick = (lax.broadcasted_iota(jnp.int32, (GROUP_W, HG_DK), 0) % HG_DK
            == lax.broadcasted_iota(jnp.int32, (GROUP_W, HG_DK), 1))
    pick_bf = jnp.where(pick, 1.0, 0.0).astype(BF16)
    for direction in range(2):
        hi, lo = _split2(st_ref[direction])
        s_out_ref[direction] = _dot_tn(hi, pick_bf) + _dot_tn(lo, pick_bf)

    def finish(i, carry):
        rows = pl.ds(pl.multiple_of(i * HG_NORM_TILE, HG_NORM_TILE), HG_NORM_TILE)
        o = o_ref[rows, :] + ob_ref[rows, :]
        hi, mid = _split2(o * o)
        ms = (_dot(hi, hmask_bf) + _dot(mid, hmask_bf)) * (1.0 / HG_DK)
        y = o * lax.rsqrt(ms + EPS) * ng_ref[...]
        o_ref[rows, :] = y * _silu(zg_ref[rows, :])
        return carry

    lax.fori_loop(0, seq // HG_NORM_TILE, finish, 0)


def _hgrn_call(z, lb, norm_g, s0t, layer, n_seq, seq):
    def zcol(j):
        return pl.BlockSpec((seq, GROUP_W), lambda s: (s, j))

    full = lambda shape: pl.BlockSpec(shape, lambda s: (0,) * len(shape))
    return pl.pallas_call(
        functools.partial(_hgrn_kernel, seq=seq),
        grid=(n_seq,),
        in_specs=[zcol(4), zcol(5), zcol(6), zcol(7), zcol(8),
                  pl.BlockSpec((None, 2, GROUP_W), lambda s: (layer, 0, 0)),
                  pl.BlockSpec((None, 1, GROUP_W), lambda s: (layer, 0, 0)),
                  pl.BlockSpec((None, 2, GROUP_W, GROUP_W), lambda s: (s, 0, 0, 0)),
                  full((HG_LEVELS, HG_CHUNK, GROUP_W))],
        out_specs=[pl.BlockSpec((seq, GROUP_W), lambda s: (s, 0)),
                   pl.BlockSpec((None, 2, GROUP_W, HG_DK), lambda s: (s, 0, 0, 0))],
        out_shape=[jax.ShapeDtypeStruct((n_seq * seq, GROUP_W), F32),
                   jax.ShapeDtypeStruct((n_seq, 2, GROUP_W, HG_DK), F32)],
        scratch_shapes=[pltpu.VMEM((seq, GROUP_W), F32), pltpu.VMEM((2, GROUP_W, GROUP_W), F32)],
        compiler_params=_params(1),
        name="hgrn",
    )(z, z, z, z, z, lb, norm_g, s0t, jnp.asarray(_HG_BMASK))


def _hgrn_state_to_kernel(s):
    st = jnp.swapaxes(s.astype(F32), -1, -2)
    eye = jnp.eye(HG_HEADS, dtype=F32)
    full = jnp.einsum('ndhvk,hg->ndhvgk', st, eye)
    return full.reshape(s.shape[0], 2, GROUP_W, GROUP_W)


def _hgrn_state_from_kernel(st):
    return st.reshape(st.shape[0], 2, HG_HEADS, HG_DK, HG_DK)


S5_BLOCK = 8
S5_ROW_W = S5_BLOCK * GROUP_W
S5_XW = 4 * S5_STATE
S5_COL_TILE = 1024
S5_OUT_TILE = 512
S5_STEPS = (1, 2, 4)
S5_TABLES = 2 * (len(S5_STEPS) + 1)


def _s5_in_kernel(u_ref, w_ref, x_ref):
    x_ref[...] = _dot(u_ref[...], w_ref[...])


def _s5_in_call(ub, wx, layer):
    n_blk = ub.shape[0]
    return pl.pallas_call(
        _s5_in_kernel,
        grid=(S5_XW // S5_COL_TILE,),
        in_specs=[pl.BlockSpec((n_blk, S5_ROW_W), lambda j: (0, 0)),
                  pl.BlockSpec((None, S5_ROW_W, S5_COL_TILE), lambda j: (layer, 0, j))],
        out_specs=pl.BlockSpec((n_blk, S5_COL_TILE), lambda j: (0, j)),
        out_shape=jax.ShapeDtypeStruct((n_blk, S5_XW), F32),
        compiler_params=_params(1),
        name="s5_in",
    )(ub, wx)


def _s5_scan_kernel(x_ref, tab_ref, s0re_ref, s0im_ref, h_ref, hre_ref, him_ref, *, n_blk):
    n_pairs = n_blk // (2 * SUBLANE)
    row = lax.broadcasted_iota(jnp.int32, (SUBLANE, S5_STATE), 0)
    n_steps = len(S5_STEPS)

    for direction in range(2):
        c_re = 2 * direction * S5_STATE
        c_im = c_re + S5_STATE
        edge = 0 if direction == 0 else SUBLANE - 1
        last = SUBLANE - 1 - edge
        unit = 0 if direction == 0 else SUBLANE - 1

        def tile(rows, carry):
            hc_r, hc_i, px_r, px_i = carry
            xr = x_ref[rows, c_re:c_re + S5_STATE]
            xi = x_ref[rows, c_im:c_im + S5_STATE]
            shift = 1 if direction == 0 else SUBLANE - 1
            hr = jnp.where(row == edge, px_r, pltpu.roll(xr, shift, 0))
            hi = jnp.where(row == edge, px_i, pltpu.roll(xi, shift, 0))
            for s, step in enumerate(S5_STEPS):
                sh = step if direction == 0 else SUBLANE - step
                rr = pltpu.roll(hr, sh, 0)
                ri = pltpu.roll(hi, sh, 0)
                ar = tab_ref[direction, 2 * s]
                ai = tab_ref[direction, 2 * s + 1]
                hr, hi = hr + ar * rr - ai * ri, hi + ar * ri + ai * rr
            pr = tab_ref[direction, 2 * n_steps]
            pi = tab_ref[direction, 2 * n_steps + 1]
            hr, hi = hr + pr * hc_r - pi * hc_i, hi + pr * hc_i + pi * hc_r
            bc = lambda a: jnp.broadcast_to(a[last:last + 1, :], (SUBLANE, S5_STATE))
            return hr, hi, (bc(hr), bc(hi), bc(xr), bc(xi))

        def pair(j, carry):
            jj = j if direction == 0 else n_pairs - 1 - j
            base = pl.multiple_of(jj * 2 * SUBLANE, 2 * SUBLANE)
            offs = (0, SUBLANE) if direction == 0 else (SUBLANE, 0)
            out = {}
            for off in offs:
                hr, hi, carry = tile(pl.ds(base + off, SUBLANE), carry)
                out[off] = (hr, hi)
            both = pl.ds(base, 2 * SUBLANE)
            h_ref[both, c_re:c_re + S5_STATE] = jnp.concatenate([out[0][0], out[SUBLANE][0]], 0).astype(BF16)
            h_ref[both, c_im:c_im + S5_STATE] = jnp.concatenate([out[0][1], out[SUBLANE][1]], 0).astype(BF16)
            return carry

        zero = jnp.zeros((SUBLANE, S5_STATE), F32)
        init = (zero, zero,
                jnp.broadcast_to(s0re_ref[direction], (SUBLANE, S5_STATE)),
                jnp.broadcast_to(s0im_ref[direction], (SUBLANE, S5_STATE)))
        hc_r, hc_i, px_r, px_i = lax.fori_loop(0, n_pairs, pair, init)
        a_r = tab_ref[direction, 2 * n_steps][unit:unit + 1, :]
        a_i = tab_ref[direction, 2 * n_steps + 1][unit:unit + 1, :]
        hre_ref[direction] = a_r * hc_r[0:1, :] - a_i * hc_i[0:1, :] + px_r[0:1, :]
        him_ref[direction] = a_r * hc_i[0:1, :] + a_i * hc_r[0:1, :] + px_i[0:1, :]


def _s5_scan_call(x, tab, s0re, s0im, layer, n_seq, n_blk):
    state = pl.BlockSpec((None, 2, 1, S5_STATE), lambda s: (s, 0, 0, 0))
    return pl.pallas_call(
        functools.partial(_s5_scan_kernel, n_blk=n_blk),
        grid=(n_seq,),
        in_specs=[pl.BlockSpec((n_blk, S5_XW), lambda s: (s, 0)),
                  pl.BlockSpec((None, 2, S5_TABLES, SUBLANE, S5_STATE), lambda s: (layer, 0, 0, 0, 0)),
                  state, state],
        out_specs=[pl.BlockSpec((n_blk, S5_XW), lambda s: (s, 0)), state, state],
        out_shape=[jax.ShapeDtypeStruct((n_seq * n_blk, S5_XW), BF16),
                   jax.ShapeDtypeStruct((n_seq, 2, 1, S5_STATE), F32),
                   jax.ShapeDtypeStruct((n_seq, 2, 1, S5_STATE), F32)],
        compiler_params=_params(1),
        name="s5_scan",
    )(x, tab, s0re, s0im)


def _s5_out_kernel(u_ref, h_ref, wt_ref, wc_ref, y_ref):
    y_ref[...] = _dot(u_ref[...], wt_ref[...]) + _dot(h_ref[...], wc_ref[...])


def _s5_out_call(ub, h, wt, wc, layer):
    n_blk = ub.shape[0]
    return pl.pallas_call(
        _s5_out_kernel,
        grid=(S5_ROW_W // S5_OUT_TILE,),
        in_specs=[pl.BlockSpec((n_blk, S5_ROW_W), lambda j: (0, 0)),
                  pl.BlockSpec((n_blk, S5_XW), lambda j: (0, 0)),
                  pl.BlockSpec((None, S5_ROW_W, S5_OUT_TILE), lambda j: (layer, 0, j)),
                  pl.BlockSpec((None, S5_XW, S5_OUT_TILE), lambda j: (layer, 0, j))],
        out_specs=pl.BlockSpec((n_blk, S5_OUT_TILE), lambda j: (0, j)),
        out_shape=jax.ShapeDtypeStruct((n_blk, S5_ROW_W), F32),
        compiler_params=_params(1),
        name="s5_out",
    )(ub, h, wt, wc)


def _s5_mix(zs_bf, consts, s0re, s0im, layer, n_seq, seq):
    wx, wt, wc, tab = consts
    n_blk = seq // S5_BLOCK
    ub = zs_bf.reshape(n_seq * n_blk, S5_ROW_W)
    x = _s5_in_call(ub, wx, layer)
    h, h_re, h_im = _s5_scan_call(x, tab, s0re, s0im, layer, n_seq, n_blk)
    y = _s5_out_call(ub, h, wt, wc, layer)
    return y.reshape(n_seq * seq, GROUP_W), h_re, h_im


S5_EXPAND_ROWS = 512


def _expand_kernel(c_ref, t_ref, o_ref, *, row_div, col_div):
    rows, cols = o_ref.shape
    r = lax.broadcasted_iota(jnp.int32, (rows, cols), 0) + pl.program_id(1) * rows
    col = lax.broadcasted_iota(jnp.int32, (rows, cols), 1)
    same_group = (r // row_div) % S5_GROUPS == (col // col_div) % S5_GROUPS
    o_ref[...] = jnp.where(same_group, _dot(c_ref[...], t_ref[...]), 0.0).astype(BF16)


def _expand_call(compact, spread, row_div, col_div):
    n_rows, k = compact.shape[1:]
    n_cols = spread.shape[1]
    return pl.pallas_call(
        functools.partial(_expand_kernel, row_div=row_div, col_div=col_div),
        grid=(DEPTH, n_rows // S5_EXPAND_ROWS),
        in_specs=[pl.BlockSpec((None, S5_EXPAND_ROWS, k), lambda l, i: (l, i, 0)),
                  pl.BlockSpec((k, n_cols), lambda l, i: (0, 0))],
        out_specs=pl.BlockSpec((None, S5_EXPAND_ROWS, n_cols), lambda l, i: (l, i, 0)),
        out_shape=jax.ShapeDtypeStruct((DEPTH, n_rows, n_cols), BF16),
        compiler_params=_params(2),
        name="s5_expand",
    )(compact, spread)


def _s5_constants(a_re, a_im, log_dt, b_re, b_im, c_re, c_im):
    n = S5_BLOCK
    a = lax.complex(a_re.astype(F32), a_im.astype(F32))
    dt = jnp.exp(log_dt.astype(F32))[..., None]
    a_bar = jnp.exp(a * dt)
    b_bar = ((a_bar - 1.0) / a)[..., None] * lax.complex(b_re.astype(F32), b_im.astype(F32))
    c = lax.complex(c_re.astype(F32), c_im.astype(F32))
    pw = [jnp.ones_like(a_bar)]
    for _ in range(n):
        pw.append(pw[-1] * a_bar)
    pw = jnp.stack(pw, axis=2)
    idx = jnp.arange(n)
    kron = lambda *m: functools.reduce(np.kron, m)
    spread_state = jnp.asarray(kron(np.eye(4), np.ones((1, S5_GROUPS)), np.eye(S5_P)), BF16)
    spread_token = jnp.asarray(kron(np.eye(n), np.ones((1, S5_GROUPS)), np.eye(S5_IN)), BF16)

    w_f = pw[:, 0, ::-1][:, 1:, :, :, None] * b_bar[:, 0, None]
    w_b = pw[:, 1, :n, :, :, None] * b_bar[:, 1, None]
    w4 = jnp.stack([jnp.real(w_f), jnp.imag(w_f), jnp.real(w_b), jnp.imag(w_b)], axis=1)
    wx_c = jnp.transpose(w4, (0, 2, 3, 5, 1, 4)).reshape(DEPTH, S5_ROW_W, 4 * S5_P)
    wx = _expand_call(wx_c.astype(BF16), spread_state, S5_IN, S5_P)

    o_f = c[:, 0, None] * pw[:, 0, 1:, :, None, :]
    o_b = c[:, 1, None] * pw[:, 1, ::-1][:, :n, :, None, :]
    o4 = jnp.stack([jnp.real(o_f), -jnp.imag(o_f), jnp.real(o_b), -jnp.imag(o_b)], axis=1)
    wc_c = jnp.transpose(o4, (0, 1, 3, 5, 2, 4)).reshape(DEPTH, S5_XW, n * S5_IN)
    wc = _expand_call(wc_c.astype(BF16), spread_token, S5_P, S5_IN)

    k_f = jnp.real(jnp.einsum('lgcp,ltgp,lgpd->ltgdc', c[:, 0], pw[:, 0, :n], b_bar[:, 0]))
    k_b = jnp.real(jnp.einsum('lgcp,ltgp,lgpd->ltgdc', c[:, 1], pw[:, 1, :n], b_bar[:, 1]))
    lag = idx[None, :] - idx[:, None]
    t_f = jnp.where((lag >= 0)[None, :, :, None, None, None], k_f[:, jnp.clip(lag, 0, n - 1)], 0.0)
    t_b = jnp.where((lag <= 0)[None, :, :, None, None, None], k_b[:, jnp.clip(-lag, 0, n - 1)], 0.0)
    wt_c = jnp.transpose(t_f + t_b, (0, 1, 3, 4, 2, 5)).reshape(DEPTH, S5_ROW_W, n * S5_IN)
    wt = _expand_call(wt_c.astype(BF16), spread_token, S5_IN, S5_IN)

    a_blk = pw[:, :, n].reshape(DEPTH, 2, S5_STATE)
    pows = [a_blk]
    for _ in range(SUBLANE - 1):
        pows.append(pows[-1] * a_blk)
    row = jnp.arange(SUBLANE)
    tabs = []
    for direction in range(2):
        t = []
        for step in S5_STEPS:
            live = (row >= step) if direction == 0 else (row < SUBLANE - step)
            t.append(jnp.where(live[None, :, None], pows[step - 1][:, direction, None, :], 0.0))
        order = range(SUBLANE) if direction == 0 else range(SUBLANE - 1, -1, -1)
        t.append(jnp.stack([pows[i][:, direction] for i in order], axis=1))
        planes = []
        for x in t:
            planes += [jnp.real(x), jnp.imag(x)]
        tabs.append(jnp.stack(planes, axis=1))
    tab = jnp.stack(tabs, axis=1).astype(F32)
    return wx, wt, wc, tab


OUT_TILE = 256
FF_BLOCK = 1024


def _outproj_kernel(x_ref, oa_ref, ob_ref, oc_ref, ys_ref, zs_ref, sd_ref, gw_ref, gb_ref,
                    ada_ref, g2_ref, wo_ref, w1_ref, w2_ref, gf_ref, *out_refs, final):
    ada = ada_ref[...]
    chunk = lambda j: ada[:, j * D_MODEL:(j + 1) * D_MODEL]
    gate1, sh2, sc2, gate2 = chunk(2), chunk(3), chunk(4), chunk(5)
    ys = jax.nn.gelu(ys_ref[...] + sd_ref[...] * zs_ref[...])
    od = ys * _sigmoid(_dot(ys.astype(BF16), gw_ref[...]) + gb_ref[...])
    mixed = jnp.concatenate([oa_ref[...], ob_ref[...], oc_ref[...], od], axis=-1).astype(BF16)
    x = x_ref[...] + gate1 * _dot(mixed, wo_ref[...])
    ms = jnp.mean(x * x, axis=-1, keepdims=True)
    h = (x * lax.rsqrt(ms + EPS) * (g2_ref[...] * (1.0 + sc2)) + sh2).astype(BF16)
    acc = jnp.zeros((OUT_TILE, D_MODEL), F32)
    for j in range(D_FF // FF_BLOCK):
        cols = slice(j * FF_BLOCK, (j + 1) * FF_BLOCK)
        a = jnp.maximum(_dot(h, w1_ref[:, cols]), 0.0)
        acc = acc + _dot((a * a).astype(BF16), w2_ref[cols, :])
    x = x + gate2 * acc
    out_refs[0][...] = x
    if final:
        ms = jnp.mean(x * x, axis=-1, keepdims=True)
        out_refs[1][...] = x * lax.rsqrt(ms + EPS) * gf_ref[...]


def _outproj_call(x, mix, s5_par, ada4, g2, w_out, w1, w2, gf, layer, cond_row, final):
    n_tok = x.shape[0]
    tok = pl.BlockSpec((OUT_TILE, D_MODEL), lambda i: (i, 0))
    grp = pl.BlockSpec((OUT_TILE, GROUP_W), lambda i: (i, 0))
    lay = lambda shape: pl.BlockSpec((None,) + shape, lambda i: (layer,) + (0,) * len(shape))
    n_out = 2 if final else 1
    res = pl.pallas_call(
        functools.partial(_outproj_kernel, final=final),
        grid=(n_tok // OUT_TILE,),
        in_specs=[tok, grp, grp, grp, grp, grp,
                  lay((1, GROUP_W)), lay((GROUP_W, GROUP_W)), lay((1, GROUP_W)),
                  pl.BlockSpec((None, None, 1, ADA_CHUNKS * D_MODEL),
                               lambda i: (layer, cond_row(i * OUT_TILE), 0, 0)),
                  lay((1, D_MODEL)), lay((D_MODEL, D_MODEL)), lay((D_MODEL, D_FF)), lay((D_FF, D_MODEL)),
                  pl.BlockSpec((1, D_MODEL), lambda i: (0, 0))],
        out_specs=[tok] * n_out,
        out_shape=[jax.ShapeDtypeStruct((n_tok, D_MODEL), F32)] * n_out,
        compiler_params=_params(1),
        name="outproj",
    )(x, *mix, *s5_par, ada4, g2, w_out, w1, w2, gf)
    return res


def _grid_pos_embed(n_tokens):
    rows = n_tokens // GRID_W
    r, col = jnp.meshgrid(jnp.arange(rows, dtype=F32), jnp.arange(GRID_W, dtype=F32), indexing='ij')
    r = r.reshape(-1)
    col = col.reshape(-1)
    quarter = D_MODEL // 4
    freq = jnp.exp(-math.log(POS_BASE) * jnp.arange(quarter, dtype=F32) / quarter)
    ar = r[:, None] * freq
    ac = col[:, None] * freq
    return jnp.concatenate([jnp.sin(ar), jnp.cos(ar), jnp.sin(ac), jnp.cos(ac)], axis=-1)


def _run_stream(x, n_seq, seq, cond_row, hg_s0, s5_s0re, s5_s0im, P):
    hg_states, s5_re, s5_im = [], [], []
    y = None
    for l in range(DEPTH):
        z, zs, zs_bf = _inproj_call(x, P['ada4'], P['norm1_g'], P['w_in'], l, cond_row)
        o_conv = _conv_call(z, P['conv_w'], P['conv_b'], P['conv_ln_g'], P['conv_ln_b'], l, n_seq, seq)
        o_gm = _gmlp_call(z, P['gmlp_norm_g'], P['gmlp_ws'], P['gmlp_bs'], l, n_seq, seq)
        o_hg, hg_end = _hgrn_call(z, P['hgrn_lb'], P['hgrn_norm_g'], hg_s0[l], l, n_seq, seq)
        y_s5, h_re, h_im = _s5_mix(zs_bf, P['s5'], s5_s0re[l], s5_s0im[l], l, n_seq, seq)
        final = l == DEPTH - 1
        res = _outproj_call(x, (o_conv, o_gm, o_hg, y_s5, zs), (P['s5_d'], P['s5_glu_w'], P['s5_glu_b']),
                            P['ada4'], P['norm2_g'], P['w_out'],
                            P['mlp_w1'], P['mlp_w2'], P['final_norm_g'], l, cond_row, final)
        x = res[0]
        if final:
            y = res[1]
        hg_states.append(hg_end)
        s5_re.append(h_re)
        s5_im.append(h_im)
    return y, hg_states, s5_re, s5_im


def kernel(x_prompt, x_sample, state_hgrn, state_s5_re, state_s5_im, c, c_ctx, norm1_g, norm2_g, ada_w, ada_b, w_in, conv_w, conv_b, conv_ln_g, conv_ln_b, gmlp_norm_g, gmlp_ws, gmlp_bs, hgrn_lb_logits, hgrn_norm_g, s5_a_re, s5_a_im, s5_log_dt, s5_b_re, s5_b_im, s5_c_re, s5_c_im, s5_d, s5_glu_w, s5_glu_b, w_out, mlp_w1, mlp_w2, final_norm_g):
    n_ctx, seq_ctx, _ = x_prompt.shape
    n_lat, seq_lat, _ = x_sample.shape
    assert n_lat + 1 <= COND_ROWS
    assert seq_ctx % IN_TILE == 0 or IN_TILE % seq_ctx == 0

    lb_soft = jax.nn.softmax(hgrn_lb_logits.astype(F32), axis=1)
    hgrn_lb = jnp.cumsum(lb_soft, axis=1) - lb_soft[:, :1]
    vec = lambda a: a.astype(F32).reshape(DEPTH, 1, -1)
    hd = GROUP_W // GM_HEADS
    P = {
        'norm1_g': vec(norm1_g), 'norm2_g': vec(norm2_g),
        'w_in': w_in.astype(BF16), 'w_out': w_out.astype(BF16),
        'mlp_w1': mlp_w1.astype(BF16), 'mlp_w2': mlp_w2.astype(BF16),
        'conv_w': jnp.broadcast_to(conv_w.astype(F32)[:, :, None, :], (DEPTH, CONV_K, SUBLANE, GROUP_W)), 'conv_b': vec(conv_b), 'conv_ln_g': vec(conv_ln_g),
        'conv_ln_b': vec(conv_ln_b),
        'gmlp_norm_g': vec(gmlp_norm_g),
        'gmlp_ws': jnp.transpose(gmlp_ws, (0, 2, 1, 3)).reshape(DEPTH, GM_CHUNK, GM_HEADS * GM_CHUNK).astype(BF16),
        'gmlp_bs': jnp.repeat(jnp.transpose(gmlp_bs.astype(F32), (0, 2, 1)), hd, axis=2),
        'hgrn_lb': jnp.transpose(hgrn_lb, (1, 0, 2)), 'hgrn_norm_g': vec(hgrn_norm_g),
        's5': _s5_constants(s5_a_re, s5_a_im, s5_log_dt, s5_b_re, s5_b_im, s5_c_re, s5_c_im),
        's5_d': vec(s5_d), 's5_glu_w': s5_glu_w.astype(BF16), 's5_glu_b': vec(s5_glu_b),
        'final_norm_g': final_norm_g.astype(F32).reshape(1, D_MODEL),
    }
    cond = jnp.zeros((COND_ROWS, D_MODEL), F32).at[0].set(c_ctx.astype(F32)).at[1:1 + n_lat].set(c.astype(F32))
    ada = _ada_call(cond, ada_w.astype(F32), ada_b.astype(F32))
    P['ada4'] = ada.reshape(DEPTH, COND_ROWS, 1, ADA_CHUNKS * D_MODEL)

    zeros_hg = jnp.zeros((n_ctx, 2, GROUP_W, GROUP_W), F32)
    zeros_s5 = jnp.zeros((n_ctx, 2, 1, S5_STATE), F32)
    y_ctx, hg_ctx, re_ctx, im_ctx = _run_stream(
        x_prompt.astype(F32).reshape(n_ctx * seq_ctx, D_MODEL), n_ctx, seq_ctx, lambda r: 0,
        [zeros_hg] * DEPTH, [zeros_s5] * DEPTH, [zeros_s5] * DEPTH, P)

    xs = x_sample.astype(F32) + _grid_pos_embed(seq_lat)[None]
    hg0 = [_hgrn_state_to_kernel(state_hgrn[:, l]) for l in range(DEPTH)]
    re0 = [state_s5_re[:, l].astype(F32).reshape(n_lat, 2, 1, S5_STATE) for l in range(DEPTH)]
    im0 = [state_s5_im[:, l].astype(F32).reshape(n_lat, 2, 1, S5_STATE) for l in range(DEPTH)]
    y_lat, _, _, _ = _run_stream(xs.reshape(n_lat * seq_lat, D_MODEL), n_lat, seq_lat,
                                 lambda r: 1 + r // seq_lat, hg0, re0, im0, P)

    dt = x_prompt.dtype
    new_hg = jnp.stack([_hgrn_state_from_kernel(s) for s in hg_ctx], axis=1).astype(dt)
    new_re = jnp.stack([s.reshape(n_ctx, 2, S5_GROUPS, S5_P) for s in re_ctx], axis=1).astype(dt)
    new_im = jnp.stack([s.reshape(n_ctx, 2, S5_GROUPS, S5_P) for s in im_ctx], axis=1).astype(dt)
    return (y_ctx.reshape(n_ctx, seq_ctx, D_MODEL).astype(dt),
            y_lat.reshape(n_lat, seq_lat, D_MODEL).astype(x_sample.dtype), new_hg, new_re, new_im)
```

```python
import functools
import math

import numpy as np
import jax
import jax.numpy as jnp
from jax import lax
from jax.experimental import pallas as pl
from jax.experimental.pallas import tpu as pltpu

D_MODEL = 1024
DEPTH = 4
GRID_W = 64
GROUP_W = 256
N_IN_BLOCKS = 10
D_FF = 4 * D_MODEL
EPS = 1e-6
F_MIN = 1e-30
ADA_CHUNKS = 6
CONV_K = 31
CONV_PAD = CONV_K // 2
GM_CHUNK = 128
GM_HEADS = 4
HG_HEADS = 4
HG_DK = 64
HG_CHUNK = 64
HG_LEVELS = 6
S5_IN = 16
S5_GROUPS = 16
S5_P = 64
S5_STATE = S5_GROUPS * S5_P
POS_BASE = 10000.0

SUBLANE = 8
LANE = 128
COND_ROWS = 8
VMEM_LIMIT = 56 * 1024 * 1024

F32 = jnp.float32
BF16 = jnp.bfloat16


def _sigmoid(x):
    return 1.0 / (1.0 + jnp.exp(-x))


def _silu(x):
    return x * _sigmoid(x)


def _params(n_parallel=1):
    return pltpu.CompilerParams(dimension_semantics=("arbitrary",) * n_parallel,
                                vmem_limit_bytes=VMEM_LIMIT)


def _split2(x):
    hi = x.astype(BF16)
    lo = (x - hi.astype(F32)).astype(BF16)
    return hi, lo


def _dot(a, b):
    return jnp.dot(a, b, preferred_element_type=F32)


def _dot_nt(a, b):
    return lax.dot_general(a, b, (((1,), (1,)), ((), ())), preferred_element_type=F32)


def _dot_tn(a, b):
    return lax.dot_general(a, b, (((0,), (0,)), ((), ())), preferred_element_type=F32)


ADA_BLOCK = 1536


def _ada_kernel(cond_ref, w_ref, b_ref, o_ref):
    s = _silu(cond_ref[...])
    o_ref[...] = _dot(s.astype(BF16), w_ref[...].astype(BF16)) + b_ref[...]


def _ada_call(cond, ada_w, ada_b):
    n_out = ADA_CHUNKS * D_MODEL
    return pl.pallas_call(
        _ada_kernel,
        grid=(DEPTH, n_out // ADA_BLOCK),
        in_specs=[pl.BlockSpec((COND_ROWS, D_MODEL), lambda l, j: (0, 0)),
                  pl.BlockSpec((None, D_MODEL, ADA_BLOCK), lambda l, j: (l, 0, j)),
                  pl.BlockSpec((None, 1, ADA_BLOCK), lambda l, j: (l, 0, j))],
        out_specs=pl.BlockSpec((None, COND_ROWS, ADA_BLOCK), lambda l, j: (l, 0, j)),
        out_shape=jax.ShapeDtypeStruct((DEPTH, COND_ROWS, n_out), F32),
        compiler_params=_params(2),
        name="ada",
    )(cond, ada_w, ada_b.reshape(DEPTH, 1, n_out))


IN_TILE = 512
Z_MAIN_W = (N_IN_BLOCKS - 1) * GROUP_W


def _inproj_kernel(x_ref, ada_ref, g_ref, w_ref, z_ref, zs_ref, zsb_ref, half_ref):
    x = x_ref[...]
    ada = ada_ref[...]
    sh1 = ada[:, 0:D_MODEL]
    sc1 = ada[:, D_MODEL:2 * D_MODEL]
    ms = jnp.mean(x * x, axis=-1, keepdims=True)
    h = x * lax.rsqrt(ms + EPS) * (g_ref[...] * (1.0 + sc1)) + sh1
    z = _dot(h.astype(BF16), w_ref[...])
    z_ref[...] = z[:, 0:Z_MAIN_W]
    zs_ref[...] = z[:, Z_MAIN_W:]
    for h in range(GROUP_W // LANE):
        half_ref[h] = z[:, Z_MAIN_W + h * LANE:Z_MAIN_W + (h + 1) * LANE]
    for j in range(S5_BLOCK):
        for h in range(GROUP_W // LANE):
            c0 = j * GROUP_W + h * LANE
            zsb_ref[:, c0:c0 + LANE] = (
                half_ref[h, pl.ds(j, IN_TILE // S5_BLOCK, stride=S5_BLOCK), :].astype(BF16))


def _inproj_call(x, ada4, g1, w_in, layer, cond_row):
    n_tok = x.shape[0]
    zw = N_IN_BLOCKS * GROUP_W
    tile = lambda w: pl.BlockSpec((IN_TILE, w), lambda i: (i, 0))
    return pl.pallas_call(
        _inproj_kernel,
        grid=(n_tok // IN_TILE,),
        in_specs=[pl.BlockSpec((IN_TILE, D_MODEL), lambda i: (i, 0)),
                  pl.BlockSpec((None, None, 1, ADA_CHUNKS * D_MODEL),
                               lambda i: (layer, cond_row(i * IN_TILE), 0, 0)),
                  pl.BlockSpec((None, 1, D_MODEL), lambda i: (layer, 0, 0)),
                  pl.BlockSpec((None, D_MODEL, zw), lambda i: (layer, 0, 0))],
        out_specs=[tile(Z_MAIN_W), tile(GROUP_W),
                   pl.BlockSpec((IN_TILE // S5_BLOCK, S5_ROW_W), lambda i: (i, 0))],
        out_shape=[jax.ShapeDtypeStruct((n_tok, Z_MAIN_W), F32),
                   jax.ShapeDtypeStruct((n_tok, GROUP_W), F32),
                   jax.ShapeDtypeStruct((n_tok // S5_BLOCK, S5_ROW_W), BF16)],
        scratch_shapes=[pltpu.VMEM((GROUP_W // LANE, IN_TILE, LANE), F32)],
        compiler_params=_params(1),
        name="inproj",
    )(x, ada4, g1, w_in)


CONV_TILE = 32
CONV_HALO = 16
CONV_FILL = 128
CONV_NORM_TILE = 256


def _conv_kernel(z_ref, w_ref, b_ref, lng_ref, lnb_ref, o_ref, pad_ref, *, seq):
    tail = seq + 2 * CONV_HALO - (seq + SUBLANE)
    for r in range(SUBLANE):
        pad_ref[r, 0:CONV_HALO, :] = jnp.zeros((CONV_HALO, GROUP_W), F32)
        pad_ref[r, seq + SUBLANE:seq + 2 * CONV_HALO, :] = jnp.zeros((tail, GROUP_W), F32)
    for t0 in range(0, seq, CONV_FILL):
        u = z_ref[t0:t0 + CONV_FILL, 0:GROUP_W] * _sigmoid(z_ref[t0:t0 + CONV_FILL, GROUP_W:2 * GROUP_W])
        for r in range(SUBLANE):
            pad_ref[r, CONV_HALO - r + t0:CONV_HALO - r + t0 + CONV_FILL, :] = u
    first = CONV_HALO - CONV_PAD

    def tile(i, carry):
        r0 = pl.multiple_of(i * CONV_TILE, CONV_TILE)
        acc = jnp.zeros((CONV_TILE, GROUP_W), F32)
        for k in range(CONV_K):
            r = (first + k) % SUBLANE
            base = first + k - r
            wk = jnp.concatenate([w_ref[k]] * (CONV_TILE // SUBLANE), axis=0)
            acc = acc + wk * pad_ref[r, pl.ds(r0 + base, CONV_TILE), :]
        o_ref[pl.ds(r0, CONV_TILE), :] = acc + b_ref[...]
        return carry

    lax.fori_loop(0, seq // CONV_TILE, tile, 0)

    def norm(i, carry):
        rows = pl.ds(pl.multiple_of(i * CONV_NORM_TILE, CONV_NORM_TILE), CONV_NORM_TILE)
        c = o_ref[rows, :]
        mu = jnp.mean(c, axis=-1, keepdims=True)
        cc = c - mu
        var = jnp.mean(cc * cc, axis=-1, keepdims=True)
        y = cc * lax.rsqrt(var + EPS) * lng_ref[...] + lnb_ref[...]
        o_ref[rows, :] = _silu(y)
        return carry

    lax.fori_loop(0, seq // CONV_NORM_TILE, norm, 0)


def _conv_call(z, conv_w, conv_b, ln_g, ln_b, layer, n_seq, seq):
    vec = pl.BlockSpec((None, 1, GROUP_W), lambda s: (layer, 0, 0))
    return pl.pallas_call(
        functools.partial(_conv_kernel, seq=seq),
        grid=(n_seq,),
        in_specs=[pl.BlockSpec((seq, 2 * GROUP_W), lambda s: (s, 0)),
                  pl.BlockSpec((None, CONV_K, SUBLANE, GROUP_W), lambda s: (layer, 0, 0, 0)),
                  vec, vec, vec],
        out_specs=pl.BlockSpec((seq, GROUP_W), lambda s: (s, 0)),
        out_shape=jax.ShapeDtypeStruct((n_seq * seq, GROUP_W), F32),
        scratch_shapes=[pltpu.VMEM((SUBLANE, seq + 2 * CONV_HALO, GROUP_W), F32)],
        compiler_params=_params(1),
        name="conv",
    )(z, conv_w, conv_b, ln_g, ln_b)


def _head_mask(rows_per_head, cols_per_head, n_rows, n_cols):
    r = lax.broadcasted_iota(jnp.int32, (n_rows, n_cols), 0) // rows_per_head
    c = lax.broadcasted_iota(jnp.int32, (n_rows, n_cols), 1) // cols_per_head
    return r == c


GM_TILE = 1024


def _gmlp_kernel(z_ref, g_ref, ws_ref, bs_ref, o_ref):
    hd = GROUP_W // GM_HEADS
    mask = _head_mask(GM_CHUNK, hd, GM_HEADS * GM_CHUNK, GROUP_W)
    for r0 in range(0, GM_TILE, GM_CHUNK):
        u = z_ref[r0:r0 + GM_CHUNK, 0:GROUP_W]
        v = z_ref[r0:r0 + GM_CHUNK, GROUP_W:2 * GROUP_W]
        vn = v * lax.rsqrt(jnp.mean(v * v, axis=-1, keepdims=True) + EPS) * g_ref[...]
        stack = jnp.where(mask, jnp.concatenate([vn] * GM_HEADS, axis=0), 0.0).astype(BF16)
        sv = _dot(ws_ref[...], stack) + bs_ref[...]
        o_ref[r0:r0 + GM_CHUNK, :] = u * sv


def _gmlp_call(z, norm_g, ws_cat, bs_full, layer):
    n_tok = z.shape[0]
    return pl.pallas_call(
        _gmlp_kernel,
        grid=(n_tok // GM_TILE,),
        in_specs=[pl.BlockSpec((GM_TILE, 2 * GROUP_W), lambda i: (i, 1)),
                  pl.BlockSpec((None, 1, GROUP_W), lambda i: (layer, 0, 0)),
                  pl.BlockSpec((None, GM_CHUNK, GM_HEADS * GM_CHUNK), lambda i: (layer, 0, 0)),
                  pl.BlockSpec((None, GM_CHUNK, GROUP_W), lambda i: (layer, 0, 0))],
        out_specs=pl.BlockSpec((GM_TILE, GROUP_W), lambda i: (i, 0)),
        out_shape=jax.ShapeDtypeStruct((n_tok, GROUP_W), F32),
        compiler_params=_params(1),
        name="gmlp",
    )(z, norm_g, ws_cat, bs_full)


def _hgrn_block_masks():
    t = np.arange(HG_CHUNK)
    blk = np.zeros((HG_LEVELS, HG_CHUNK, HG_CHUNK), np.float32)
    for lvl in range(HG_LEVELS):
        b = t >> (lvl + 1)
        blk[lvl] = (b[:, None] == b[None, :])
    return np.tile(blk, (1, 1, HG_HEADS))


_HG_BMASK = _hgrn_block_masks()
HG_NORM_TILE = 256


def _hgrn_kernel(zq_ref, zi_ref, zg_ref, zff_ref, zfb_ref, lb_ref, ng_ref, bmask_ref, *rest, seq, has_state):
    s0_ref = rest[0] if has_state else None
    o_ref, s_out_ref, ob_ref, st_ref = rest[-4:]
    n = HG_CHUNK
    n_chunks = seq // n
    hmask = _head_mask(HG_DK, HG_DK, GROUP_W, GROUP_W)
    hmask_bf = jnp.where(hmask, 1.0, 0.0).astype(BF16)
    row = lax.broadcasted_iota(jnp.int32, (n, GROUP_W), 0)

    def block_diag(a):
        return jnp.where(hmask, jnp.concatenate([a] * HG_HEADS, axis=0), 0.0).astype(BF16)

    def spread_rows(a, first, period):
        return jnp.concatenate([jnp.broadcast_to(a[r:r + 1, :], (period, GROUP_W))
                                for r in range(first, n, period)], axis=0)

    def boundary(cum, lvl, direction):
        m = 1 << lvl
        at = m - 1 if direction == 0 else m
        if 2 * m >= SUBLANE:
            return spread_rows(cum, at, 2 * m)
        lo = spread_rows(cum, at, SUBLANE)
        hi = spread_rows(cum, at + 2 * m, SUBLANE)
        return jnp.where((row & (2 * m)) == 0, lo, hi)

    def chunk_step(direction, c, zf_ref, dst_ref):
        lb = lb_ref[direction:direction + 1, :]
        end_row = n - 1 if direction == 0 else 0
        pos = row if direction == 0 else n - 1 - row
        rows = pl.ds(pl.multiple_of(c * n, n), n)
        q = _silu(zq_ref[rows, :])
        v = zi_ref[rows, :]
        f = lb + (1.0 - lb) * _sigmoid(zf_ref[rows, :])
        k = 1.0 - f
        f = jnp.maximum(f, F_MIN)
        cum = jnp.log(f)
        d = 1
        while d < n:
            cum = cum + jnp.where(pos >= d, pltpu.roll(cum, d if direction == 0 else n - d, 0), 0.0)
            d *= 2
        total = cum[end_row:end_row + 1]
        w_cum = jnp.exp(cum)
        w_rem = jnp.exp(total - cum)
        w_end = jnp.exp(total)
        st = st_ref[direction]
        inter = _dot_nt((q * w_cum).astype(BF16), st.astype(BF16))
        scores = jnp.zeros((n, GROUP_W), F32)
        for lvl in range(HG_LEVELS):
            upper = (pos & (1 << lvl)) != 0
            if lvl == 0:
                wl = jnp.where(upper, f, 1.0)
            else:
                ref = boundary(cum, lvl, direction)
                wl = jnp.exp(jnp.where(upper, cum - ref, ref - cum))
            ql = jnp.where(upper, q * wl, 0.0).astype(BF16)
            kl = jnp.where(upper, 0.0, k * wl)
            scores = scores + bmask_ref[lvl] * _dot_nt(ql, block_diag(kl))
        diag = _dot((q * k).astype(BF16), hmask_bf)
        intra = _dot(scores.astype(BF16), block_diag(v)) + diag * v
        upd = _dot_tn(v.astype(BF16), (k * w_rem).astype(BF16))
        st_ref[direction] = st * w_end + jnp.where(hmask, upd, 0.0)
        dst_ref[rows, :] = inter + intra

    st_ref[...] = s0_ref[...] if has_state else jnp.zeros(st_ref.shape, F32)

    def both(i, carry):
        chunk_step(0, i, zff_ref, o_ref)
        chunk_step(1, n_chunks - 1 - i, zfb_ref, ob_ref)
        return carry

    lax.fori_loop(0, n_chunks, both, 0)

    pick = (lax.broadcasted_iota(jnp.int32, (GROUP_W, HG_DK), 0) % HG_DK
            == lax.broadcasted_iota(jnp.int32, (GROUP_W, HG_DK), 1))
    pick_bf = jnp.where(pick, 1.0, 0.0).astype(BF16)
    for direction in range(2):
        hi, lo = _split2(st_ref[direction])
        s_out_ref[direction] = _dot_tn(hi, pick_bf) + _dot_tn(lo, pick_bf)

    def finish(i, carry):
        rows = pl.ds(pl.multiple_of(i * HG_NORM_TILE, HG_NORM_TILE), HG_NORM_TILE)
        o = o_ref[rows, :] + ob_ref[rows, :]
        hi, mid = _split2(o * o)
        ms = (_dot(hi, hmask_bf) + _dot(mid, hmask_bf)) * (1.0 / HG_DK)
        y = o * lax.rsqrt(ms + EPS) * ng_ref[...]
        o_ref[rows, :] = y * _silu(zg_ref[rows, :])
        return carry

    lax.fori_loop(0, seq // HG_NORM_TILE, finish, 0)


def _hgrn_call(z, lb, norm_g, s0t, layer, n_seq, seq):
    def zcol(j):
        return pl.BlockSpec((seq, GROUP_W), lambda s: (s, j))

    full = lambda shape: pl.BlockSpec(shape, lambda s: (0,) * len(shape))
    has_state = s0t is not None
    state_specs = [pl.BlockSpec((None, 2, GROUP_W, GROUP_W), lambda s: (s, 0, 0, 0))] if has_state else []
    state_args = (s0t,) if has_state else ()
    return pl.pallas_call(
        functools.partial(_hgrn_kernel, seq=seq, has_state=has_state),
        grid=(n_seq,),
        in_specs=[zcol(4), zcol(5), zcol(6), zcol(7), zcol(8),
                  pl.BlockSpec((None, 2, GROUP_W), lambda s: (layer, 0, 0)),
                  pl.BlockSpec((None, 1, GROUP_W), lambda s: (layer, 0, 0)),
                  full((HG_LEVELS, HG_CHUNK, GROUP_W))] + state_specs,
        out_specs=[pl.BlockSpec((seq, GROUP_W), lambda s: (s, 0)),
                   pl.BlockSpec((None, 2, GROUP_W, HG_DK), lambda s: (s, 0, 0, 0))],
        out_shape=[jax.ShapeDtypeStruct((n_seq * seq, GROUP_W), F32),
                   jax.ShapeDtypeStruct((n_seq, 2, GROUP_W, HG_DK), F32)],
        scratch_shapes=[pltpu.VMEM((seq, GROUP_W), F32), pltpu.VMEM((2, GROUP_W, GROUP_W), F32)],
        compiler_params=_params(1),
        name="hgrn",
    )(z, z, z, z, z, lb, norm_g, jnp.asarray(_HG_BMASK), *state_args)


def _hgrn_state_to_kernel(s):
    st = jnp.swapaxes(s.astype(F32), -1, -2)
    eye = jnp.eye(HG_HEADS, dtype=F32)
    full = jnp.einsum('ndhvk,hg->ndhvgk', st, eye)
    return full.reshape(s.shape[0], 2, GROUP_W, GROUP_W)


def _hgrn_state_from_kernel(st):
    return st.reshape(st.shape[0], 2, HG_HEADS, HG_DK, HG_DK)


S5_BLOCK = 8
S5_ROW_W = S5_BLOCK * GROUP_W
S5_XW = 4 * S5_STATE
S5_COL_TILE = 1024
S5_OUT_TILE = 512
S5_STEPS = (1, 2, 4)
S5_TABLES = 2 * (len(S5_STEPS) + 1)


def _s5_in_kernel(u_ref, w_ref, x_ref):
    x_ref[...] = _dot(u_ref[...], w_ref[...])


def _s5_in_call(ub, wx, layer):
    n_blk = ub.shape[0]
    return pl.pallas_call(
        _s5_in_kernel,
        grid=(S5_XW // S5_COL_TILE,),
        in_specs=[pl.BlockSpec((n_blk, S5_ROW_W), lambda j: (0, 0)),
                  pl.BlockSpec((None, S5_ROW_W, S5_COL_TILE), lambda j: (layer, 0, j))],
        out_specs=pl.BlockSpec((n_blk, S5_COL_TILE), lambda j: (0, j)),
        out_shape=jax.ShapeDtypeStruct((n_blk, S5_XW), F32),
        compiler_params=_params(1),
        name="s5_in",
    )(ub, wx)


def _s5_scan_kernel(x_ref, tab_ref, s0re_ref, s0im_ref, h_ref, hre_ref, him_ref, *, n_blk):
    n_pairs = n_blk // (2 * SUBLANE)
    row = lax.broadcasted_iota(jnp.int32, (SUBLANE, S5_STATE), 0)
    n_steps = len(S5_STEPS)

    for direction in range(2):
        c_re = 2 * direction * S5_STATE
        c_im = c_re + S5_STATE
        edge = 0 if direction == 0 else SUBLANE - 1
        last = SUBLANE - 1 - edge
        unit = 0 if direction == 0 else SUBLANE - 1

        def tile(rows, carry):
            hc_r, hc_i, px_r, px_i = carry
            xr = x_ref[rows, c_re:c_re + S5_STATE]
            xi = x_ref[rows, c_im:c_im + S5_STATE]
            shift = 1 if direction == 0 else SUBLANE - 1
            hr = jnp.where(row == edge, px_r, pltpu.roll(xr, shift, 0))
            hi = jnp.where(row == edge, px_i, pltpu.roll(xi, shift, 0))
            for s, step in enumerate(S5_STEPS):
                sh = step if direction == 0 else SUBLANE - step
                rr = pltpu.roll(hr, sh, 0)
                ri = pltpu.roll(hi, sh, 0)
                ar = tab_ref[direction, 2 * s]
                ai = tab_ref[direction, 2 * s + 1]
                hr, hi = hr + ar * rr - ai * ri, hi + ar * ri + ai * rr
            pr = tab_ref[direction, 2 * n_steps]
            pi = tab_ref[direction, 2 * n_steps + 1]
            hr, hi = hr + pr * hc_r - pi * hc_i, hi + pr * hc_i + pi * hc_r
            bc = lambda a: jnp.broadcast_to(a[last:last + 1, :], (SUBLANE, S5_STATE))
            return hr, hi, (bc(hr), bc(hi), bc(xr), bc(xi))

        def pair(j, carry):
            jj = j if direction == 0 else n_pairs - 1 - j
            base = pl.multiple_of(jj * 2 * SUBLANE, 2 * SUBLANE)
            offs = (0, SUBLANE) if direction == 0 else (SUBLANE, 0)
            out = {}
            for off in offs:
                hr, hi, carry = tile(pl.ds(base + off, SUBLANE), carry)
                out[off] = (hr, hi)
            both = pl.ds(base, 2 * SUBLANE)
            h_ref[both, c_re:c_re + S5_STATE] = jnp.concatenate([out[0][0], out[SUBLANE][0]], 0).astype(BF16)
            h_ref[both, c_im:c_im + S5_STATE] = jnp.concatenate([out[0][1], out[SUBLANE][1]], 0).astype(BF16)
            return carry

        zero = jnp.zeros((SUBLANE, S5_STATE), F32)
        init = (zero, zero,
                jnp.broadcast_to(s0re_ref[direction], (SUBLANE, S5_STATE)),
                jnp.broadcast_to(s0im_ref[direction], (SUBLANE, S5_STATE)))
        hc_r, hc_i, px_r, px_i = lax.fori_loop(0, n_pairs, pair, init)
        a_r = tab_ref[direction, 2 * n_steps][unit:unit + 1, :]
        a_i = tab_ref[direction, 2 * n_steps + 1][unit:unit + 1, :]
        hre_ref[direction] = a_r * hc_r[0:1, :] - a_i * hc_i[0:1, :] + px_r[0:1, :]
        him_ref[direction] = a_r * hc_i[0:1, :] + a_i * hc_r[0:1, :] + px_i[0:1, :]


def _s5_scan_call(x, tab, s0re, s0im, layer, n_seq, n_blk):
    state = pl.BlockSpec((None, 2, 1, S5_STATE), lambda s: (s, 0, 0, 0))
    return pl.pallas_call(
        functools.partial(_s5_scan_kernel, n_blk=n_blk),
        grid=(n_seq,),
        in_specs=[pl.BlockSpec((n_blk, S5_XW), lambda s: (s, 0)),
                  pl.BlockSpec((None, 2, S5_TABLES, SUBLANE, S5_STATE), lambda s: (layer, 0, 0, 0, 0)),
                  state, state],
        out_specs=[pl.BlockSpec((n_blk, S5_XW), lambda s: (s, 0)), state, state],
        out_shape=[jax.ShapeDtypeStruct((n_seq * n_blk, S5_XW), BF16),
                   jax.ShapeDtypeStruct((n_seq, 2, 1, S5_STATE), F32),
                   jax.ShapeDtypeStruct((n_seq, 2, 1, S5_STATE), F32)],
        compiler_params=_params(1),
        name="s5_scan",
    )(x, tab, s0re, s0im)


def _s5_out_kernel(u_ref, h_ref, wt_ref, wc_ref, y_ref):
    y_ref[...] = _dot(u_ref[...], wt_ref[...]) + _dot(h_ref[...], wc_ref[...])


def _s5_out_call(ub, h, wt, wc, layer):
    n_blk = ub.shape[0]
    return pl.pallas_call(
        _s5_out_kernel,
        grid=(S5_ROW_W // S5_OUT_TILE,),
        in_specs=[pl.BlockSpec((n_blk, S5_ROW_W), lambda j: (0, 0)),
                  pl.BlockSpec((n_blk, S5_XW), lambda j: (0, 0)),
                  pl.BlockSpec((None, S5_ROW_W, S5_OUT_TILE), lambda j: (layer, 0, j)),
                  pl.BlockSpec((None, S5_XW, S5_OUT_TILE), lambda j: (layer, 0, j))],
        out_specs=pl.BlockSpec((n_blk, S5_OUT_TILE), lambda j: (0, j)),
        out_shape=jax.ShapeDtypeStruct((n_blk, S5_ROW_W), F32),
        compiler_params=_params(1),
        name="s5_out",
    )(ub, h, wt, wc)


def _s5_mix(ub, consts, s0re, s0im, layer, n_seq, seq):
    wx, wt, wc, tab = consts
    n_blk = seq // S5_BLOCK
    x = _s5_in_call(ub, wx, layer)
    h, h_re, h_im = _s5_scan_call(x, tab, s0re, s0im, layer, n_seq, n_blk)
    y = _s5_out_call(ub, h, wt, wc, layer)
    return y, h_re, h_im


S5_EXPAND_ROWS = 512


def _expand_kernel(c_ref, t_ref, o_ref, *, row_div, col_div):
    rows, cols = o_ref.shape
    r = lax.broadcasted_iota(jnp.int32, (rows, cols), 0) + pl.program_id(1) * rows
    col = lax.broadcasted_iota(jnp.int32, (rows, cols), 1)
    same_group = (r // row_div) % S5_GROUPS == (col // col_div) % S5_GROUPS
    o_ref[...] = jnp.where(same_group, _dot(c_ref[...], t_ref[...]), 0.0).astype(BF16)


def _expand_call(compact, spread, row_div, col_div):
    n_rows, k = compact.shape[1:]
    n_cols = spread.shape[1]
    return pl.pallas_call(
        functools.partial(_expand_kernel, row_div=row_div, col_div=col_div),
        grid=(DEPTH, n_rows // S5_EXPAND_ROWS),
        in_specs=[pl.BlockSpec((None, S5_EXPAND_ROWS, k), lambda l, i: (l, i, 0)),
                  pl.BlockSpec((k, n_cols), lambda l, i: (0, 0))],
        out_specs=pl.BlockSpec((None, S5_EXPAND_ROWS, n_cols), lambda l, i: (l, i, 0)),
        out_shape=jax.ShapeDtypeStruct((DEPTH, n_rows, n_cols), BF16),
        compiler_params=_params(2),
        name="s5_expand",
    )(compact, spread)


def _s5_constants(a_re, a_im, log_dt, b_re, b_im, c_re, c_im):
    n = S5_BLOCK
    a = lax.complex(a_re.astype(F32), a_im.astype(F32))
    dt = jnp.exp(log_dt.astype(F32))[..., None]
    a_bar = jnp.exp(a * dt)
    b_bar = ((a_bar - 1.0) / a)[..., None] * lax.complex(b_re.astype(F32), b_im.astype(F32))
    c = lax.complex(c_re.astype(F32), c_im.astype(F32))
    pw = [jnp.ones_like(a_bar)]
    for _ in range(n):
        pw.append(pw[-1] * a_bar)
    pw = jnp.stack(pw, axis=2)
    idx = jnp.arange(n)
    kron = lambda *m: functools.reduce(np.kron, m)
    spread_state = jnp.asarray(kron(np.eye(4), np.ones((1, S5_GROUPS)), np.eye(S5_P)), BF16)
    spread_token = jnp.asarray(kron(np.eye(n), np.ones((1, S5_GROUPS)), np.eye(S5_IN)), BF16)

    w_f = pw[:, 0, ::-1][:, 1:, :, :, None] * b_bar[:, 0, None]
    w_b = pw[:, 1, :n, :, :, None] * b_bar[:, 1, None]
    w4 = jnp.stack([jnp.real(w_f), jnp.imag(w_f), jnp.real(w_b), jnp.imag(w_b)], axis=1)
    wx_c = jnp.transpose(w4, (0, 2, 3, 5, 1, 4)).reshape(DEPTH, S5_ROW_W, 4 * S5_P)
    wx = _expand_call(wx_c.astype(BF16), spread_state, S5_IN, S5_P)

    o_f = c[:, 0, None] * pw[:, 0, 1:, :, None, :]
    o_b = c[:, 1, None] * pw[:, 1, ::-1][:, :n, :, None, :]
    o4 = jnp.stack([jnp.real(o_f), -jnp.imag(o_f), jnp.real(o_b), -jnp.imag(o_b)], axis=1)
    wc_c = jnp.transpose(o4, (0, 1, 3, 5, 2, 4)).reshape(DEPTH, S5_XW, n * S5_IN)
    wc = _expand_call(wc_c.astype(BF16), spread_token, S5_P, S5_IN)

    k_f = jnp.real(jnp.einsum('lgcp,ltgp,lgpd->ltgdc', c[:, 0], pw[:, 0, :n], b_bar[:, 0]))
    k_b = jnp.real(jnp.einsum('lgcp,ltgp,lgpd->ltgdc', c[:, 1], pw[:, 1, :n], b_bar[:, 1]))
    lag = idx[None, :] - idx[:, None]
    t_f = jnp.where((lag >= 0)[None, :, :, None, None, None], k_f[:, jnp.clip(lag, 0, n - 1)], 0.0)
    t_b = jnp.where((lag <= 0)[None, :, :, None, None, None], k_b[:, jnp.clip(-lag, 0, n - 1)], 0.0)
    wt_c = jnp.transpose(t_f + t_b, (0, 1, 3, 4, 2, 5)).reshape(DEPTH, S5_ROW_W, n * S5_IN)
    wt = _expand_call(wt_c.astype(BF16), spread_token, S5_IN, S5_IN)

    a_blk = pw[:, :, n].reshape(DEPTH, 2, S5_STATE)
    pows = [a_blk]
    for _ in range(SUBLANE - 1):
        pows.append(pows[-1] * a_blk)
    row = jnp.arange(SUBLANE)
    tabs = []
    for direction in range(2):
        t = []
        for step in S5_STEPS:
            live = (row >= step) if direction == 0 else (row < SUBLANE - step)
            t.append(jnp.where(live[None, :, None], pows[step - 1][:, direction, None, :], 0.0))
        order = range(SUBLANE) if direction == 0 else range(SUBLANE - 1, -1, -1)
        t.append(jnp.stack([pows[i][:, direction] for i in order], axis=1))
        planes = []
        for x in t:
            planes += [jnp.real(x), jnp.imag(x)]
        tabs.append(jnp.stack(planes, axis=1))
    tab = jnp.stack(tabs, axis=1).astype(F32)
    return wx, wt, wc, tab


OUT_TILE = 256
FF_BLOCK = 1024


def _outproj_kernel(x_ref, oa_ref, ob_ref, oc_ref, ys_ref, zs_ref, sd_ref, gw_ref, gb_ref,
                    ada_ref, g2_ref, wo_ref, w1_ref, w2_ref, gf_ref, *rest, final):
    out_refs, tok_ref = rest[:-1], rest[-1]
    ada = ada_ref[...]
    chunk = lambda j: ada[:, j * D_MODEL:(j + 1) * D_MODEL]
    gate1, sh2, sc2, gate2 = chunk(2), chunk(3), chunk(4), chunk(5)
    for j in range(S5_BLOCK):
        for h in range(GROUP_W // LANE):
            c0 = j * GROUP_W + h * LANE
            tok_ref[h, pl.ds(j, OUT_TILE // S5_BLOCK, stride=S5_BLOCK), :] = ys_ref[:, c0:c0 + LANE]
    ssm = jnp.concatenate([tok_ref[h] for h in range(GROUP_W // LANE)], axis=-1)
    ys = jax.nn.gelu(ssm + sd_ref[...] * zs_ref[...])
    od = ys * _sigmoid(_dot(ys.astype(BF16), gw_ref[...]) + gb_ref[...])
    mixed = jnp.concatenate([oa_ref[...], ob_ref[...], oc_ref[...], od], axis=-1).astype(BF16)
    x = x_ref[...] + gate1 * _dot(mixed, wo_ref[...])
    ms = jnp.mean(x * x, axis=-1, keepdims=True)
    h = (x * lax.rsqrt(ms + EPS) * (g2_ref[...] * (1.0 + sc2)) + sh2).astype(BF16)
    acc = jnp.zeros((OUT_TILE, D_MODEL), F32)
    for j in range(D_FF // FF_BLOCK):
        cols = slice(j * FF_BLOCK, (j + 1) * FF_BLOCK)
        a = jnp.maximum(_dot(h, w1_ref[:, cols]), 0.0)
        acc = acc + _dot((a * a).astype(BF16), w2_ref[cols, :])
    x = x + gate2 * acc
    out_refs[0][...] = x
    if final:
        ms = jnp.mean(x * x, axis=-1, keepdims=True)
        out_refs[1][...] = x * lax.rsqrt(ms + EPS) * gf_ref[...]


def _outproj_call(x, mix, s5_par, ada4, g2, w_out, w1, w2, gf, layer, cond_row, final):
    n_tok = x.shape[0]
    tok = pl.BlockSpec((OUT_TILE, D_MODEL), lambda i: (i, 0))
    grp = pl.BlockSpec((OUT_TILE, GROUP_W), lambda i: (i, 0))
    lay = lambda shape: pl.BlockSpec((None,) + shape, lambda i: (layer,) + (0,) * len(shape))
    n_out = 2 if final else 1
    res = pl.pallas_call(
        functools.partial(_outproj_kernel, final=final),
        grid=(n_tok // OUT_TILE,),
        in_specs=[tok, grp, grp, grp,
                  pl.BlockSpec((OUT_TILE // S5_BLOCK, S5_ROW_W), lambda i: (i, 0)), grp,
                  lay((1, GROUP_W)), lay((GROUP_W, GROUP_W)), lay((1, GROUP_W)),
                  pl.BlockSpec((None, None, 1, ADA_CHUNKS * D_MODEL),
                               lambda i: (layer, cond_row(i * OUT_TILE), 0, 0)),
                  lay((1, D_MODEL)), lay((D_MODEL, D_MODEL)), lay((D_MODEL, D_FF)), lay((D_FF, D_MODEL)),
                  pl.BlockSpec((1, D_MODEL), lambda i: (0, 0))],
        out_specs=[tok] * n_out,
        out_shape=[jax.ShapeDtypeStruct((n_tok, D_MODEL), F32)] * n_out,
        scratch_shapes=[pltpu.VMEM((GROUP_W // LANE, OUT_TILE, LANE), F32)],
        compiler_params=_params(1),
        name="outproj",
    )(x, *mix, *s5_par, ada4, g2, w_out, w1, w2, gf)
    return res


def _grid_pos_embed(n_tokens):
    rows = n_tokens // GRID_W
    r, col = jnp.meshgrid(jnp.arange(rows, dtype=F32), jnp.arange(GRID_W, dtype=F32), indexing='ij')
    r = r.reshape(-1)
    col = col.reshape(-1)
    quarter = D_MODEL // 4
    freq = jnp.exp(-math.log(POS_BASE) * jnp.arange(quarter, dtype=F32) / quarter)
    ar = r[:, None] * freq
    ac = col[:, None] * freq
    return jnp.concatenate([jnp.sin(ar), jnp.cos(ar), jnp.sin(ac), jnp.cos(ac)], axis=-1)


def _layer(x, l, n_seq, seq, cond_row, hg_s0, s5_s0re, s5_s0im, P, final):
    z, zs, zs_blk = _inproj_call(x, P['ada4'], P['norm1_g'], P['w_in'], l, cond_row)
    o_conv = _conv_call(z, P['conv_w'], P['conv_b'], P['conv_ln_g'], P['conv_ln_b'], l, n_seq, seq)
    o_gm = _gmlp_call(z, P['gmlp_norm_g'], P['gmlp_ws'], P['gmlp_bs'], l)
    o_hg, hg_end = _hgrn_call(z, P['hgrn_lb'], P['hgrn_norm_g'], hg_s0, l, n_seq, seq)
    y_s5, h_re, h_im = _s5_mix(zs_blk, P['s5'], s5_s0re, s5_s0im, l, n_seq, seq)
    res = _outproj_call(x, (o_conv, o_gm, o_hg, y_s5, zs), (P['s5_d'], P['s5_glu_w'], P['s5_glu_b']),
                        P['ada4'], P['norm2_g'], P['w_out'],
                        P['mlp_w1'], P['mlp_w2'], P['final_norm_g'], l, cond_row, final)
    return res, hg_end, h_re, h_im


def _prepare(w, hgrn_lb):
    vec = lambda a: a.astype(F32).reshape(DEPTH, 1, -1)
    hd = GROUP_W // GM_HEADS
    return {
        'norm1_g': vec(w['norm1_g']), 'norm2_g': vec(w['norm2_g']),
        'w_in': w['w_in'].astype(BF16), 'w_out': w['w_out'].astype(BF16),
        'mlp_w1': w['mlp_w1'].astype(BF16), 'mlp_w2': w['mlp_w2'].astype(BF16),
        'conv_w': jnp.broadcast_to(w['conv_w'].astype(F32)[:, :, None, :], (DEPTH, CONV_K, SUBLANE, GROUP_W)),
        'conv_b': vec(w['conv_b']), 'conv_ln_g': vec(w['conv_ln_g']), 'conv_ln_b': vec(w['conv_ln_b']),
        'gmlp_norm_g': vec(w['gmlp_norm_g']),
        'gmlp_ws': jnp.transpose(w['gmlp_ws'], (0, 2, 1, 3)).reshape(
            DEPTH, GM_CHUNK, GM_HEADS * GM_CHUNK).astype(BF16),
        'gmlp_bs': jnp.repeat(jnp.transpose(w['gmlp_bs'].astype(F32), (0, 2, 1)), hd, axis=2),
        'hgrn_lb': jnp.transpose(hgrn_lb, (1, 0, 2)), 'hgrn_norm_g': vec(w['hgrn_norm_g']),
        's5': _s5_constants(w['s5_a_re'], w['s5_a_im'], w['s5_log_dt'], w['s5_b_re'], w['s5_b_im'],
                            w['s5_c_re'], w['s5_c_im']),
        's5_d': vec(w['s5_d']), 's5_glu_w': w['s5_glu_w'].astype(BF16), 's5_glu_b': vec(w['s5_glu_b']),
        'final_norm_g': w['final_norm_g'].astype(F32).reshape(1, D_MODEL),
    }


def _run_stream(x, n_seq, seq, cond_row, hg_s0, s5_s0re, s5_s0im, P):
    hg_states, s5_re, s5_im = [], [], []
    y = None
    for l in range(DEPTH):
        final = l == DEPTH - 1
        res, hg_end, h_re, h_im = _layer(x, l, n_seq, seq, cond_row, hg_s0[l], s5_s0re[l], s5_s0im[l], P, final)
        x = res[0]
        if final:
            y = res[1]
        hg_states.append(hg_end)
        s5_re.append(h_re)
        s5_im.append(h_im)
    return y, hg_states, s5_re, s5_im


def kernel(x_prompt, x_sample, state_hgrn, state_s5_re, state_s5_im, c, c_ctx, norm1_g, norm2_g, ada_w, ada_b, w_in, conv_w, conv_b, conv_ln_g, conv_ln_b, gmlp_norm_g, gmlp_ws, gmlp_bs, hgrn_lb_logits, hgrn_norm_g, s5_a_re, s5_a_im, s5_log_dt, s5_b_re, s5_b_im, s5_c_re, s5_c_im, s5_d, s5_glu_w, s5_glu_b, w_out, mlp_w1, mlp_w2, final_norm_g):
    n_ctx, seq_ctx, _ = x_prompt.shape
    n_lat, seq_lat, _ = x_sample.shape
    assert n_lat + 1 <= COND_ROWS
    assert seq_ctx % IN_TILE == 0 or IN_TILE % seq_ctx == 0

    lb_soft = jax.nn.softmax(hgrn_lb_logits.astype(F32), axis=1)
    hgrn_lb = jnp.cumsum(lb_soft, axis=1) - lb_soft[:, :1]
    P = _prepare(dict(
        norm1_g=norm1_g, norm2_g=norm2_g, w_in=w_in, w_out=w_out, mlp_w1=mlp_w1, mlp_w2=mlp_w2,
        conv_w=conv_w, conv_b=conv_b, conv_ln_g=conv_ln_g, conv_ln_b=conv_ln_b,
        gmlp_norm_g=gmlp_norm_g, gmlp_ws=gmlp_ws, gmlp_bs=gmlp_bs, hgrn_norm_g=hgrn_norm_g,
        s5_a_re=s5_a_re, s5_a_im=s5_a_im, s5_log_dt=s5_log_dt, s5_b_re=s5_b_re, s5_b_im=s5_b_im,
        s5_c_re=s5_c_re, s5_c_im=s5_c_im, s5_d=s5_d, s5_glu_w=s5_glu_w, s5_glu_b=s5_glu_b,
        final_norm_g=final_norm_g), hgrn_lb)
    cond =jnp.zeros((COND_ROWS, D_MODEL), F32).at[0].set(c_ctx.astype(F32)).at[1:1 + n_lat].set(c.astype(F32))
    ada = _ada_call(cond, ada_w.astype(F32), ada_b.astype(F32))
    P['ada4'] = ada.reshape(DEPTH, COND_ROWS, 1, ADA_CHUNKS * D_MODEL)

    zeros_s5 = jnp.zeros((n_ctx, 2, 1, S5_STATE), F32)
    y_ctx, hg_ctx, re_ctx, im_ctx = _run_stream(
        x_prompt.astype(F32).reshape(n_ctx * seq_ctx, D_MODEL), n_ctx, seq_ctx, lambda r: 0,
        [None] * DEPTH, [zeros_s5] * DEPTH, [zeros_s5] * DEPTH, P)

    xs = x_sample.astype(F32) + _grid_pos_embed(seq_lat)[None]
    hg0 = [_hgrn_state_to_kernel(state_hgrn[:, l]) for l in range(DEPTH)]
    re0 = [state_s5_re[:, l].astype(F32).reshape(n_lat, 2, 1, S5_STATE) for l in range(DEPTH)]
    im0 = [state_s5_im[:, l].astype(F32).reshape(n_lat, 2, 1, S5_STATE) for l in range(DEPTH)]
    y_lat, _, _, _ = _run_stream(xs.reshape(n_lat * seq_lat, D_MODEL), n_lat, seq_lat,
                                 lambda r: 1 + r // seq_lat, hg0, re0, im0, P)

    dt = x_prompt.dtype
    new_hg = jnp.stack([_hgrn_state_from_kernel(s) for s in hg_ctx], axis=1).astype(dt)
    new_re = jnp.stack([s.reshape(n_ctx, 2, S5_GROUPS, S5_P) for s in re_ctx], axis=1).astype(dt)
    new_im = jnp.stack([s.reshape(n_ctx, 2, S5_GROUPS, S5_P) for s in im_ctx], axis=1).astype(dt)
    return (y_ctx.reshape(n_ctx, seq_ctx, D_MODEL).astype(dt),
            y_lat.reshape(n_lat, seq_lat, D_MODEL).astype(x_sample.dtype), new_hg, new_re, new_im)
```

```python
import functools
import math

import numpy as np
import jax
import jax.numpy as jnp
from jax import lax
from jax.experimental import pallas as pl
from jax.experimental.pallas import tpu as pltpu

D_MODEL = 1024
DEPTH = 4
GRID_W = 64
GROUP_W = 256
N_IN_BLOCKS = 10
D_FF = 4 * D_MODEL
EPS = 1e-6
F_MIN = 1e-30
ADA_CHUNKS = 6
CONV_K = 31
CONV_PAD = CONV_K // 2
GM_CHUNK = 128
GM_HEADS = 4
HG_HEADS = 4
HG_DK = 64
HG_CHUNK = 64
HG_TILE_LEVELS = 3
S5_IN = 16
S5_GROUPS = 16
S5_P = 64
S5_STATE = S5_GROUPS * S5_P
POS_BASE = 10000.0

SUBLANE = 8
LANE = 128
COND_ROWS = 8
VMEM_LIMIT = 56 * 1024 * 1024

F32 = jnp.float32
BF16 = jnp.bfloat16


def _sigmoid(x):
    return 1.0 / (1.0 + jnp.exp(-x))


def _silu(x):
    return x * _sigmoid(x)


def _params(n_parallel=1):
    return pltpu.CompilerParams(dimension_semantics=("arbitrary",) * n_parallel,
                                vmem_limit_bytes=VMEM_LIMIT)


def _split2(x):
    hi = x.astype(BF16)
    lo = (x - hi.astype(F32)).astype(BF16)
    return hi, lo


def _dot(a, b):
    return jnp.dot(a, b, preferred_element_type=F32)


def _dot_nt(a, b):
    return lax.dot_general(a, b, (((1,), (1,)), ((), ())), preferred_element_type=F32)


def _dot_tn(a, b):
    return lax.dot_general(a, b, (((0,), (0,)), ((), ())), preferred_element_type=F32)


ADA_BLOCK = 1536


def _ada_kernel(cond_ref, w_ref, b_ref, o_ref):
    s = _silu(cond_ref[...])
    o_ref[...] = _dot(s.astype(BF16), w_ref[...].astype(BF16)) + b_ref[...]


def _ada_call(cond, ada_w, ada_b):
    n_out = ADA_CHUNKS * D_MODEL
    return pl.pallas_call(
        _ada_kernel,
        grid=(DEPTH, n_out // ADA_BLOCK),
        in_specs=[pl.BlockSpec((COND_ROWS, D_MODEL), lambda l, j: (0, 0)),
                  pl.BlockSpec((None, D_MODEL, ADA_BLOCK), lambda l, j: (l, 0, j)),
                  pl.BlockSpec((None, 1, ADA_BLOCK), lambda l, j: (l, 0, j))],
        out_specs=pl.BlockSpec((None, COND_ROWS, ADA_BLOCK), lambda l, j: (l, 0, j)),
        out_shape=jax.ShapeDtypeStruct((DEPTH, COND_ROWS, n_out), F32),
        compiler_params=_params(2),
        name="ada",
    )(cond, ada_w, ada_b.reshape(DEPTH, 1, n_out))


IN_TILE = 512
Z_MAIN_W = (N_IN_BLOCKS - 1) * GROUP_W


def _inproj_kernel(x_ref, ada_ref, g_ref, w_ref, z_ref, zs_ref, zsb_ref, half_ref):
    x = x_ref[...]
    ada = ada_ref[...]
    sh1 = ada[:, 0:D_MODEL]
    sc1 = ada[:, D_MODEL:2 * D_MODEL]
    ms = jnp.mean(x * x, axis=-1, keepdims=True)
    h = x * lax.rsqrt(ms + EPS) * (g_ref[...] * (1.0 + sc1)) + sh1
    z = _dot(h.astype(BF16), w_ref[...])
    z_ref[...] = z[:, 0:Z_MAIN_W]
    zs_ref[...] = z[:, Z_MAIN_W:]
    for h in range(GROUP_W // LANE):
        half_ref[h] = z[:, Z_MAIN_W + h * LANE:Z_MAIN_W + (h + 1) * LANE]
    for j in range(S5_BLOCK):
        for h in range(GROUP_W // LANE):
            c0 = _s5_col(h, j)
            zsb_ref[:, c0:c0 + LANE] = (
                half_ref[h, pl.ds(j, IN_TILE // S5_BLOCK, stride=S5_BLOCK), :].astype(BF16))


def _inproj_call(x, ada4, g1, w_in, layer, cond_row):
    n_tok = x.shape[0]
    zw = N_IN_BLOCKS * GROUP_W
    tile = lambda w: pl.BlockSpec((IN_TILE, w), lambda i: (i, 0))
    return pl.pallas_call(
        _inproj_kernel,
        grid=(n_tok // IN_TILE,),
        in_specs=[pl.BlockSpec((IN_TILE, D_MODEL), lambda i: (i, 0)),
                  pl.BlockSpec((None, None, 1, ADA_CHUNKS * D_MODEL),
                               lambda i: (layer, cond_row(i * IN_TILE), 0, 0)),
                  pl.BlockSpec((None, 1, D_MODEL), lambda i: (layer, 0, 0)),
                  pl.BlockSpec((None, D_MODEL, zw), lambda i: (layer, 0, 0))],
        out_specs=[tile(Z_MAIN_W), tile(GROUP_W),
                   pl.BlockSpec((IN_TILE // S5_BLOCK, S5_ROW_W), lambda i: (i, 0))],
        out_shape=[jax.ShapeDtypeStruct((n_tok, Z_MAIN_W), F32),
                   jax.ShapeDtypeStruct((n_tok, GROUP_W), F32),
                   jax.ShapeDtypeStruct((n_tok // S5_BLOCK, S5_ROW_W), BF16)],
        scratch_shapes=[pltpu.VMEM((GROUP_W // LANE, IN_TILE, LANE), F32)],
        compiler_params=_params(1),
        name="inproj",
    )(x, ada4, g1, w_in)


CONV_TILE = 64
CONV_HALO = 16
CONV_FILL = 128
CONV_NORM_TILE = 256


def _conv_kernel(z_ref, w_ref, b_ref, lng_ref, lnb_ref, o_ref, pad_ref, *, seq):
    tail = seq + 2 * CONV_HALO - (seq + SUBLANE)
    for r in range(SUBLANE):
        pad_ref[r, 0:CONV_HALO, :] = jnp.zeros((CONV_HALO, GROUP_W), F32)
        pad_ref[r, seq + SUBLANE:seq + 2 * CONV_HALO, :] = jnp.zeros((tail, GROUP_W), F32)
    for t0 in range(0, seq, CONV_FILL):
        u = z_ref[t0:t0 + CONV_FILL, 0:GROUP_W] * _sigmoid(z_ref[t0:t0 + CONV_FILL, GROUP_W:2 * GROUP_W])
        for r in range(SUBLANE):
            pad_ref[r, CONV_HALO - r + t0:CONV_HALO - r + t0 + CONV_FILL, :] = u
    first = CONV_HALO - CONV_PAD

    def tile(i, carry):
        r0 = pl.multiple_of(i * CONV_TILE, CONV_TILE)
        acc = jnp.zeros((CONV_TILE, GROUP_W), F32)
        for r in range(SUBLANE):
            taps = [k for k in range(CONV_K) if (first + k) % SUBLANE == r]
            lo = first + taps[0] - r
            hi = first + taps[-1] - r
            win = pad_ref[r, pl.ds(r0 + lo, CONV_TILE + hi - lo), :]
            for k in taps:
                off = first + k - r - lo
                wk = jnp.concatenate([w_ref[k]] * (CONV_TILE // SUBLANE), axis=0)
                acc = acc + wk * win[off:off + CONV_TILE, :]
        o_ref[pl.ds(r0, CONV_TILE), :] = acc + b_ref[...]
        return carry

    lax.fori_loop(0, seq // CONV_TILE, tile, 0)

    def norm(i, carry):
        rows = pl.ds(pl.multiple_of(i * CONV_NORM_TILE, CONV_NORM_TILE), CONV_NORM_TILE)
        c = o_ref[rows, :]
        mu = jnp.mean(c, axis=-1, keepdims=True)
        cc = c - mu
        var = jnp.mean(cc * cc, axis=-1, keepdims=True)
        y = cc * lax.rsqrt(var + EPS) * lng_ref[...] + lnb_ref[...]
        o_ref[rows, :] = _silu(y)
        return carry

    lax.fori_loop(0, seq // CONV_NORM_TILE, norm, 0)


def _conv_call(z, conv_w, conv_b, ln_g, ln_b, layer, n_seq, seq):
    vec = pl.BlockSpec((None, 1, GROUP_W), lambda s: (layer, 0, 0))
    return pl.pallas_call(
        functools.partial(_conv_kernel, seq=seq),
        grid=(n_seq,),
        in_specs=[pl.BlockSpec((seq, 2 * GROUP_W), lambda s: (s, 0)),
                  pl.BlockSpec((None, CONV_K, SUBLANE, GROUP_W), lambda s: (layer, 0, 0, 0)),
                  vec, vec, vec],
        out_specs=pl.BlockSpec((seq, GROUP_W), lambda s: (s, 0)),
        out_shape=jax.ShapeDtypeStruct((n_seq * seq, GROUP_W), F32),
        scratch_shapes=[pltpu.VMEM((SUBLANE, seq + 2 * CONV_HALO, GROUP_W), F32)],
        compiler_params=_params(1),
        name="conv",
    )(z, conv_w, conv_b, ln_g, ln_b)


def _head_mask(rows_per_head, cols_per_head, n_rows, n_cols):
    r = lax.broadcasted_iota(jnp.int32, (n_rows, n_cols), 0) // rows_per_head
    c = lax.broadcasted_iota(jnp.int32, (n_rows, n_cols), 1) // cols_per_head
    return r == c


GM_TILE = 1024


def _gmlp_kernel(z_ref, g_ref, ws_ref, bs_ref, o_ref):
    hd = GROUP_W // GM_HEADS
    mask = _head_mask(GM_CHUNK, hd, GM_HEADS * GM_CHUNK, GROUP_W)
    for r0 in range(0, GM_TILE, GM_CHUNK):
        u = z_ref[r0:r0 + GM_CHUNK, 0:GROUP_W]
        v = z_ref[r0:r0 + GM_CHUNK, GROUP_W:2 * GROUP_W]
        vn = v * lax.rsqrt(jnp.mean(v * v, axis=-1, keepdims=True) + EPS) * g_ref[...]
        stack = jnp.where(mask, jnp.concatenate([vn] * GM_HEADS, axis=0), 0.0).astype(BF16)
        sv = _dot(ws_ref[...], stack) + bs_ref[...]
        o_ref[r0:r0 + GM_CHUNK, :] = u * sv


def _gmlp_call(z, norm_g, ws_cat, bs_full, layer):
    n_tok = z.shape[0]
    return pl.pallas_call(
        _gmlp_kernel,
        grid=(n_tok // GM_TILE,),
        in_specs=[pl.BlockSpec((GM_TILE, 2 * GROUP_W), lambda i: (i, 1)),
                  pl.BlockSpec((None, 1, GROUP_W), lambda i: (layer, 0, 0)),
                  pl.BlockSpec((None, GM_CHUNK, GM_HEADS * GM_CHUNK), lambda i: (layer, 0, 0)),
                  pl.BlockSpec((None, GM_CHUNK, GROUP_W), lambda i: (layer, 0, 0))],
        out_specs=pl.BlockSpec((GM_TILE, GROUP_W), lambda i: (i, 0)),
        out_shape=jax.ShapeDtypeStruct((n_tok, GROUP_W), F32),
        compiler_params=_params(1),
        name="gmlp",
    )(z, norm_g, ws_cat, bs_full)


def _hgrn_block_masks():
    t = np.arange(HG_CHUNK)
    blk = np.zeros((HG_TILE_LEVELS, HG_CHUNK, HG_CHUNK), np.float32)
    for lvl in range(HG_TILE_LEVELS):
        b = t >> (lvl + 1)
        blk[lvl] = (b[:, None] == b[None, :])
    return np.tile(blk, (1, 1, HG_HEADS))


_HG_BMASK = _hgrn_block_masks()
HG_NORM_TILE = 256


def _hgrn_kernel(zq_ref, zi_ref, zg_ref, zff_ref, zfb_ref, lb_ref, ng_ref, bmask_ref, *rest, seq, has_state):
    s0_ref = rest[0] if has_state else None
    o_ref, s_out_ref, ob_ref, st_ref = rest[-4:]
    n = HG_CHUNK
    n_chunks = seq // n
    hmask = _head_mask(HG_DK, HG_DK, GROUP_W, GROUP_W)
    hmask_bf = jnp.where(hmask, 1.0, 0.0).astype(BF16)
    row = lax.broadcasted_iota(jnp.int32, (n, GROUP_W), 0)
    n_tiles = n // SUBLANE
    col_tile = (lax.broadcasted_iota(jnp.int32, (SUBLANE, GROUP_W), 1) % n) // SUBLANE

    def block_diag(a):
        return jnp.where(hmask, jnp.concatenate([a] * HG_HEADS, axis=0), 0.0).astype(BF16)

    def spread_rows(a, first, period):
        return jnp.concatenate([jnp.broadcast_to(a[r:r + 1, :], (period, GROUP_W))
                                for r in range(first, n, period)], axis=0)

    def boundary(cum, lvl, direction):
        m = 1 << lvl
        at = m - 1 if direction == 0 else m
        if 2 * m >= SUBLANE:
            return spread_rows(cum, at, 2 * m)
        lo = spread_rows(cum, at, SUBLANE)
        hi = spread_rows(cum, at + 2 * m, SUBLANE)
        return jnp.where((row & (2 * m)) == 0, lo, hi)

    def chunk_step(direction, c, zf_ref, dst_ref):
        lb = lb_ref[direction:direction + 1, :]
        end_row = n - 1 if direction == 0 else 0
        pos = row if direction == 0 else n - 1 - row
        rows = pl.ds(pl.multiple_of(c * n, n), n)
        q = _silu(zq_ref[rows, :])
        v = zi_ref[rows, :]
        f = lb + (1.0 - lb) * _sigmoid(zf_ref[rows, :])
        k = 1.0 - f
        f = jnp.maximum(f, F_MIN)
        cum = jnp.log(f)
        d = 1
        while d < n:
            cum = cum + jnp.where(pos >= d, pltpu.roll(cum, d if direction == 0 else n - d, 0), 0.0)
            d *= 2
        total = cum[end_row:end_row + 1]
        w_cum = jnp.exp(cum)
        w_rem = jnp.exp(total - cum)
        w_end = jnp.exp(total)
        st = st_ref[direction]
        inter = _dot_nt((q * w_cum).astype(BF16), st.astype(BF16))

        edge = SUBLANE - 1 if direction == 0 else 0
        tile_end = spread_rows(cum, edge, SUBLANE)
        k_far = k * jnp.exp(tile_end - cum)
        pairs = [(j, i) for j in range(n_tiles)
                 for i in (range(j + 1, n_tiles) if direction == 0 else range(j))]
        tile = lambda a, i: a[i * SUBLANE:(i + 1) * SUBLANE, :]
        q_far = jnp.concatenate([tile(q, i) * jnp.exp(tile(cum, i) - tile(tile_end, j)) for j, i in pairs], axis=0)
        far = _dot_nt(q_far.astype(BF16), block_diag(k_far))
        acc = [jnp.zeros((SUBLANE, GROUP_W), F32)] * n_tiles
        for p, (j, i) in enumerate(pairs):
            acc[i] = acc[i] + jnp.where(col_tile == j, tile(far, p), 0.0)
        scores = jnp.concatenate(acc, axis=0)

        for lvl in range(HG_TILE_LEVELS):
            upper = (pos & (1 << lvl)) != 0
            if lvl == 0:
                wl = jnp.where(upper, f, 1.0)
            else:
                ref = boundary(cum, lvl, direction)
                wl = jnp.exp(jnp.where(upper, cum - ref, ref - cum))
            ql = jnp.where(upper, q * wl, 0.0).astype(BF16)
            kl = jnp.where(upper, 0.0, k * wl)
            scores = scores + bmask_ref[lvl] * _dot_nt(ql, block_diag(kl))
        diag = _dot((q * k).astype(BF16), hmask_bf)
        intra = _dot(scores.astype(BF16), block_diag(v)) + diag * v
        upd = _dot_tn(v.astype(BF16), (k * w_rem).astype(BF16))
        st_ref[direction] = st * w_end + jnp.where(hmask, upd, 0.0)
        dst_ref[rows, :] = inter + intra

    st_ref[...] = s0_ref[...] if has_state else jnp.zeros(st_ref.shape, F32)

    def both(i, carry):
        chunk_step(0, i, zff_ref, o_ref)
        chunk_step(1, n_chunks - 1 - i, zfb_ref, ob_ref)
        return carry

    lax.fori_loop(0, n_chunks, both, 0)

    pick = (lax.broadcasted_iota(jnp.int32, (GROUP_W, HG_DK), 0) % HG_DK
            == lax.broadcasted_iota(jnp.int32, (GROUP_W, HG_DK), 1))
    pick_bf = jnp.where(pick, 1.0, 0.0).astype(BF16)
    for direction in range(2):
        hi, lo = _split2(st_ref[direction])
        s_out_ref[direction] = _dot_tn(hi, pick_bf) + _dot_tn(lo, pick_bf)

    def finish(i, carry):
        rows = pl.ds(pl.multiple_of(i * HG_NORM_TILE, HG_NORM_TILE), HG_NORM_TILE)
        o = o_ref[rows, :] + ob_ref[rows, :]
        hi, mid = _split2(o * o)
        ms = (_dot(hi, hmask_bf) + _dot(mid, hmask_bf)) * (1.0 / HG_DK)
        y = o * lax.rsqrt(ms + EPS) * ng_ref[...]
        o_ref[rows, :] = y * _silu(zg_ref[rows, :])
        return carry

    lax.fori_loop(0, seq // HG_NORM_TILE, finish, 0)


def _hgrn_call(z, lb, norm_g, s0t, layer, n_seq, seq):
    def zcol(j):
        return pl.BlockSpec((seq, GROUP_W), lambda s: (s, j))

    full = lambda shape: pl.BlockSpec(shape, lambda s: (0,) * len(shape))
    has_state = s0t is not None
    state_specs = [pl.BlockSpec((None, 2, GROUP_W, GROUP_W), lambda s: (s, 0, 0, 0))] if has_state else []
    state_args = (s0t,) if has_state else ()
    return pl.pallas_call(
        functools.partial(_hgrn_kernel, seq=seq, has_state=has_state),
        grid=(n_seq,),
        in_specs=[zcol(4), zcol(5), zcol(6), zcol(7), zcol(8),
                  pl.BlockSpec((None, 2, GROUP_W), lambda s: (layer, 0, 0)),
                  pl.BlockSpec((None, 1, GROUP_W), lambda s: (layer, 0, 0)),
                  full((HG_TILE_LEVELS, HG_CHUNK, GROUP_W))] + state_specs,
        out_specs=[pl.BlockSpec((seq, GROUP_W), lambda s: (s, 0)),
                   pl.BlockSpec((None, 2, GROUP_W, HG_DK), lambda s: (s, 0, 0, 0))],
        out_shape=[jax.ShapeDtypeStruct((n_seq * seq, GROUP_W), F32),
                   jax.ShapeDtypeStruct((n_seq, 2, GROUP_W, HG_DK), F32)],
        scratch_shapes=[pltpu.VMEM((seq, GROUP_W), F32), pltpu.VMEM((2, GROUP_W, GROUP_W), F32)],
        compiler_params=_params(1),
        name="hgrn",
    )(z, z, z, z, z, lb, norm_g, jnp.asarray(_HG_BMASK), *state_args)


def _hgrn_state_to_kernel(s):
    st = jnp.swapaxes(s.astype(F32), -1, -2)
    eye = jnp.eye(HG_HEADS, dtype=F32)
    full = jnp.einsum('ndhvk,hg->ndhvgk', st, eye)
    return full.reshape(s.shape[0], 2, GROUP_W, GROUP_W)


def _hgrn_state_from_kernel(st):
    return st.reshape(st.shape[0], 2, HG_HEADS, HG_DK, HG_DK)


S5_BLOCK = 8
S5_ROW_W = S5_BLOCK * GROUP_W
S5_HALVES = GROUP_W // LANE
S5_HALF_W = S5_BLOCK * LANE
S5_GROUPS_PER_HALF = S5_GROUPS // S5_HALVES
S5_PARTS = 4
S5_XW = S5_PARTS * S5_STATE
S5_PART_HALF = S5_STATE // S5_HALVES
S5_STEPS = (1, 2, 4)
S5_TABLES = 2 * (len(S5_STEPS) + 1)


def _s5_col(half, token):
    return half * S5_HALF_W + token * LANE


def _s5_in_kernel(u_ref, w_ref, x_ref):
    x_ref[...] = _dot(u_ref[...], w_ref[...])


def _s5_in_call(ub, wx, layer):
    n_blk = ub.shape[0]
    return pl.pallas_call(
        _s5_in_kernel,
        grid=(S5_HALVES, S5_PARTS),
        in_specs=[pl.BlockSpec((n_blk, S5_HALF_W), lambda h, p: (0, h)),
                  pl.BlockSpec((None, None, S5_HALF_W, S5_PART_HALF), lambda h, p: (layer, h, 0, p))],
        out_specs=pl.BlockSpec((n_blk, S5_PART_HALF), lambda h, p: (0, p * S5_HALVES + h)),
        out_shape=jax.ShapeDtypeStruct((n_blk, S5_XW), F32),
        compiler_params=_params(2),
        name="s5_in",
    )(ub, wx)


def _s5_scan_kernel(x_ref, tab_ref, s0re_ref, s0im_ref, h_ref, hre_ref, him_ref, *, n_blk):
    n_pairs = n_blk // (2 * SUBLANE)
    row = lax.broadcasted_iota(jnp.int32, (SUBLANE, S5_STATE), 0)
    n_steps = len(S5_STEPS)

    for direction in range(2):
        c_re = 2 * direction * S5_STATE
        c_im = c_re + S5_STATE
        edge = 0 if direction == 0 else SUBLANE - 1
        last = SUBLANE - 1 - edge
        unit = 0 if direction == 0 else SUBLANE - 1

        def tile(rows, carry):
            hc_r, hc_i, px_r, px_i = carry
            xr = x_ref[rows, c_re:c_re + S5_STATE]
            xi = x_ref[rows, c_im:c_im + S5_STATE]
            shift = 1 if direction == 0 else SUBLANE - 1
            hr = jnp.where(row == edge, px_r, pltpu.roll(xr, shift, 0))
            hi = jnp.where(row == edge, px_i, pltpu.roll(xi, shift, 0))
            for s, step in enumerate(S5_STEPS):
                sh = step if direction == 0 else SUBLANE - step
                rr = pltpu.roll(hr, sh, 0)
                ri = pltpu.roll(hi, sh, 0)
                ar = tab_ref[direction, 2 * s]
                ai = tab_ref[direction, 2 * s + 1]
                hr, hi = hr + ar * rr - ai * ri, hi + ar * ri + ai * rr
            pr = tab_ref[direction, 2 * n_steps]
            pi = tab_ref[direction, 2 * n_steps + 1]
            hr, hi = hr + pr * hc_r - pi * hc_i, hi + pr * hc_i + pi * hc_r
            bc = lambda a: jnp.broadcast_to(a[last:last + 1, :], (SUBLANE, S5_STATE))
            return hr, hi, (bc(hr), bc(hi), bc(xr), bc(xi))

        def pair(j, carry):
            jj = j if direction == 0 else n_pairs - 1 - j
            base = pl.multiple_of(jj * 2 * SUBLANE, 2 * SUBLANE)
            offs = (0, SUBLANE) if direction == 0 else (SUBLANE, 0)
            out = {}
            for off in offs:
                hr, hi, carry = tile(pl.ds(base + off, SUBLANE), carry)
                out[off] = (hr, hi)
            both = pl.ds(base, 2 * SUBLANE)
            h_ref[both, c_re:c_re + S5_STATE] = jnp.concatenate([out[0][0], out[SUBLANE][0]], 0).astype(BF16)
            h_ref[both, c_im:c_im + S5_STATE] = jnp.concatenate([out[0][1], out[SUBLANE][1]], 0).astype(BF16)
            return carry

        zero = jnp.zeros((SUBLANE, S5_STATE), F32)
        init = (zero, zero,
                jnp.broadcast_to(s0re_ref[direction], (SUBLANE, S5_STATE)),
                jnp.broadcast_to(s0im_ref[direction], (SUBLANE, S5_STATE)))
        hc_r, hc_i, px_r, px_i = lax.fori_loop(0, n_pairs, pair, init)
        a_r = tab_ref[direction, 2 * n_steps][unit:unit + 1, :]
        a_i = tab_ref[direction, 2 * n_steps + 1][unit:unit + 1, :]
        hre_ref[direction] = a_r * hc_r[0:1, :] - a_i * hc_i[0:1, :] + px_r[0:1, :]
        him_ref[direction] = a_r * hc_i[0:1, :] + a_i * hc_r[0:1, :] + px_i[0:1, :]


def _s5_scan_call(x, tab, s0re, s0im, layer, n_seq, n_blk):
    state = pl.BlockSpec((None, 2, 1, S5_STATE), lambda s: (s, 0, 0, 0))
    return pl.pallas_call(
        functools.partial(_s5_scan_kernel, n_blk=n_blk),
        grid=(n_seq,),
        in_specs=[pl.BlockSpec((n_blk, S5_XW), lambda s: (s, 0)),
                  pl.BlockSpec((None, 2, S5_TABLES, SUBLANE, S5_STATE), lambda s: (layer, 0, 0, 0, 0)),
                  state, state],
        out_specs=[pl.BlockSpec((n_blk, S5_XW), lambda s: (s, 0)), state, state],
        out_shape=[jax.ShapeDtypeStruct((n_seq * n_blk, S5_XW), BF16),
                   jax.ShapeDtypeStruct((n_seq, 2, 1, S5_STATE), F32),
                   jax.ShapeDtypeStruct((n_seq, 2, 1, S5_STATE), F32)],
        compiler_params=_params(1),
        name="s5_scan",
    )(x, tab, s0re, s0im)


def _s5_out_kernel(u_ref, *refs):
    h_refs, (wt_ref, wc_ref, y_ref) = refs[:S5_PARTS], refs[S5_PARTS:]
    y = _dot(u_ref[...], wt_ref[...])
    for p in range(S5_PARTS):
        y = y + _dot(h_refs[p][...], wc_ref[p * S5_PART_HALF:(p + 1) * S5_PART_HALF, :])
    y_ref[...] = y


def _s5_out_call(ub, h, wt, wc, layer):
    n_blk = ub.shape[0]

    def state_part(p):
        return pl.BlockSpec((n_blk, S5_PART_HALF), lambda j: (0, p * S5_HALVES + j))

    return pl.pallas_call(
        _s5_out_kernel,
        grid=(S5_HALVES,),
        in_specs=[pl.BlockSpec((n_blk, S5_HALF_W), lambda j: (0, j))]
                 + [state_part(p) for p in range(S5_PARTS)]
                 + [pl.BlockSpec((None, None, S5_HALF_W, S5_HALF_W), lambda j: (layer, j, 0, 0)),
                    pl.BlockSpec((None, None, S5_PARTS * S5_PART_HALF, S5_HALF_W), lambda j: (layer, j, 0, 0))],
        out_specs=pl.BlockSpec((n_blk, S5_HALF_W), lambda j: (0, j)),
        out_shape=jax.ShapeDtypeStruct((n_blk, S5_ROW_W), F32),
        compiler_params=_params(1),
        name="s5_out",
    )(ub, *([h] * S5_PARTS), wt, wc)


def _s5_mix(ub, consts, s0re, s0im, layer, n_seq, seq):
    wx, wt, wc, tab = consts
    n_blk = seq // S5_BLOCK
    x = _s5_in_call(ub, wx, layer)
    h, h_re, h_im = _s5_scan_call(x, tab, s0re, s0im, layer, n_seq, n_blk)
    y = _s5_out_call(ub, h, wt, wc, layer)
    return y, h_re, h_im


S5_EXPAND_ROWS = 512


def _expand_kernel(c_ref, t_ref, o_ref, *, row_group, col_group):
    rows, cols = o_ref.shape
    r = lax.broadcasted_iota(jnp.int32, (rows, cols), 0) + pl.program_id(1) * rows
    col = lax.broadcasted_iota(jnp.int32, (rows, cols), 1)
    group = lambda i, period_width: (i % period_width[0]) // period_width[1]
    same_group = group(r, row_group) == group(col, col_group)
    o_ref[...] = jnp.where(same_group, _dot(c_ref[...], t_ref[...]), 0.0).astype(BF16)


def _expand_call(compact, spread, row_group, col_group):
    n_ops, n_rows, k = compact.shape
    n_cols = spread.shape[1]
    return pl.pallas_call(
        functools.partial(_expand_kernel, row_group=row_group, col_group=col_group),
        grid=(n_ops, n_rows // S5_EXPAND_ROWS),
        in_specs=[pl.BlockSpec((None, S5_EXPAND_ROWS, k), lambda l, i: (l, i, 0)),
                  pl.BlockSpec((k, n_cols), lambda l, i: (0, 0))],
        out_specs=pl.BlockSpec((None, S5_EXPAND_ROWS, n_cols), lambda l, i: (l, i, 0)),
        out_shape=jax.ShapeDtypeStruct((n_ops, n_rows, n_cols), BF16),
        compiler_params=_params(2),
        name="s5_expand",
    )(compact, spread)


def _s5_constants(a_re, a_im, log_dt, b_re, b_im, c_re, c_im):
    n = S5_BLOCK
    a = lax.complex(a_re.astype(F32), a_im.astype(F32))
    dt = jnp.exp(log_dt.astype(F32))[..., None]
    a_bar = jnp.exp(a * dt)
    b_bar = ((a_bar - 1.0) / a)[..., None] * lax.complex(b_re.astype(F32), b_im.astype(F32))
    c = lax.complex(c_re.astype(F32), c_im.astype(F32))
    pw = [jnp.ones_like(a_bar)]
    for _ in range(n):
        pw.append(pw[-1] * a_bar)
    pw = jnp.stack(pw, axis=2)
    gh = S5_GROUPS_PER_HALF
    kron = lambda *m: functools.reduce(np.kron, m)
    spread_state = jnp.asarray(kron(np.eye(S5_PARTS), np.ones((1, gh)), np.eye(S5_P)), BF16)
    spread_token = jnp.asarray(kron(np.eye(n), np.ones((1, gh)), np.eye(S5_IN)), BF16)
    token_group = (LANE, S5_IN)
    state_group = (S5_PART_HALF, S5_P)
    halves = lambda a: a.reshape(a.shape[:3] + (S5_HALVES, gh) + a.shape[4:])

    w_f = pw[:, 0, ::-1][:, 1:, :, :, None] * b_bar[:, 0, None]
    w_b = pw[:, 1, :n, :, :, None] * b_bar[:, 1, None]
    w4 = jnp.stack([jnp.real(w_f), jnp.imag(w_f), jnp.real(w_b), jnp.imag(w_b)], axis=1)
    wx_c = jnp.transpose(halves(w4), (0, 3, 2, 4, 6, 1, 5))
    wx_c = wx_c.reshape(DEPTH * S5_HALVES, S5_HALF_W, S5_PARTS * S5_P)
    wx = _expand_call(wx_c.astype(BF16), spread_state, token_group, state_group)
    wx = wx.reshape(DEPTH, S5_HALVES, S5_HALF_W, S5_PARTS * S5_PART_HALF)

    o_f = c[:, 0, None] * pw[:, 0, 1:, :, None, :]
    o_b = c[:, 1, None] * pw[:, 1, ::-1][:, :n, :, None, :]
    o4 = jnp.stack([jnp.real(o_f), -jnp.imag(o_f), jnp.real(o_b), -jnp.imag(o_b)], axis=1)
    wc_c = jnp.transpose(halves(o4), (0, 3, 1, 4, 6, 2, 5))
    wc_c = wc_c.reshape(DEPTH * S5_HALVES, S5_PARTS * S5_PART_HALF, n * S5_IN)
    wc = _expand_call(wc_c.astype(BF16), spread_token, state_group, token_group)
    wc = wc.reshape(DEPTH, S5_HALVES, S5_PARTS * S5_PART_HALF, S5_HALF_W)

    k_f = jnp.real(jnp.einsum('lgcp,ltgp,lgpd->lgdtc', c[:, 0], pw[:, 0, :n], b_bar[:, 0]))
    k_b = jnp.real(jnp.einsum('lgcp,ltgp,lgpd->lgdtc', c[:, 1], pw[:, 1, :n], b_bar[:, 1]))
    tt, jj, ii = np.meshgrid(np.arange(n), np.arange(n), np.arange(n), indexing='ij')
    place = lambda hit: jnp.asarray(np.kron(hit.reshape(n, n * n).astype(np.float32), np.eye(S5_IN)), F32)
    flat = lambda k: k.reshape(DEPTH * S5_GROUPS * S5_IN, n * S5_IN)
    t_all = (jnp.dot(flat(k_f), place(ii - jj == tt), precision=lax.Precision.HIGHEST)
             + jnp.dot(flat(k_b), place(jj - ii == tt), precision=lax.Precision.HIGHEST))
    wt_c = t_all.reshape(DEPTH, S5_HALVES, gh, S5_IN, n, n * S5_IN)
    wt_c = jnp.transpose(wt_c, (0, 1, 4, 2, 3, 5)).reshape(DEPTH * S5_HALVES, S5_HALF_W, n * S5_IN)
    wt = _expand_call(wt_c.astype(BF16), spread_token, token_group, token_group)
    wt = wt.reshape(DEPTH, S5_HALVES, S5_HALF_W, S5_HALF_W)

    a_blk = pw[:, :, n].reshape(DEPTH, 2, S5_STATE)
    pows = [a_blk]
    for _ in range(SUBLANE - 1):
        pows.append(pows[-1] * a_blk)
    row = jnp.arange(SUBLANE)
    tabs = []
    for direction in range(2):
        t = []
        for step in S5_STEPS:
            live = (row >= step) if direction == 0 else (row < SUBLANE - step)
            t.append(jnp.where(live[None, :, None], pows[step - 1][:, direction, None, :], 0.0))
        order = range(SUBLANE) if direction == 0 else range(SUBLANE - 1, -1, -1)
        t.append(jnp.stack([pows[i][:, direction] for i in order], axis=1))
        planes = []
        for x in t:
            planes += [jnp.real(x), jnp.imag(x)]
        tabs.append(jnp.stack(planes, axis=1))
    tab = jnp.stack(tabs, axis=1).astype(F32)
    return wx, wt, wc, tab


OUT_TILE = 256
FF_BLOCK = 1024


def _outproj_kernel(x_ref, oa_ref, ob_ref, oc_ref, ys_ref, zs_ref, sd_ref, gw_ref, gb_ref,
                    ada_ref, g2_ref, wo_ref, w1_ref, w2_ref, gf_ref, *rest, final):
    out_refs, tok_ref = rest[:-1], rest[-1]
    ada = ada_ref[...]
    chunk = lambda j: ada[:, j * D_MODEL:(j + 1) * D_MODEL]
    gate1, sh2, sc2, gate2 = chunk(2), chunk(3), chunk(4), chunk(5)
    for j in range(S5_BLOCK):
        for h in range(GROUP_W // LANE):
            c0 = _s5_col(h, j)
            tok_ref[h, pl.ds(j, OUT_TILE // S5_BLOCK, stride=S5_BLOCK), :] = ys_ref[:, c0:c0 + LANE]
    ssm = jnp.concatenate([tok_ref[h] for h in range(GROUP_W // LANE)], axis=-1)
    ys = jax.nn.gelu(ssm + sd_ref[...] * zs_ref[...])
    od = ys * _sigmoid(_dot(ys.astype(BF16), gw_ref[...]) + gb_ref[...])
    mixed = jnp.concatenate([oa_ref[...], ob_ref[...], oc_ref[...], od], axis=-1).astype(BF16)
    x = x_ref[...] + gate1 * _dot(mixed, wo_ref[...])
    ms = jnp.mean(x * x, axis=-1, keepdims=True)
    h = (x * lax.rsqrt(ms + EPS) * (g2_ref[...] * (1.0 + sc2)) + sh2).astype(BF16)
    acc = jnp.zeros((OUT_TILE, D_MODEL), F32)
    for j in range(D_FF // FF_BLOCK):
        cols = slice(j * FF_BLOCK, (j + 1) * FF_BLOCK)
        a = jnp.maximum(_dot(h, w1_ref[:, cols]), 0.0)
        acc = acc + _dot((a * a).astype(BF16), w2_ref[cols, :])
    x = x + gate2 * acc
    out_refs[0][...] = x
    if final:
        ms = jnp.mean(x * x, axis=-1, keepdims=True)
        out_refs[1][...] = x * lax.rsqrt(ms + EPS) * gf_ref[...]


def _outproj_call(x, mix, s5_par, ada4, g2, w_out, w1, w2, gf, layer, cond_row, final):
    n_tok = x.shape[0]
    tok = pl.BlockSpec((OUT_TILE, D_MODEL), lambda i: (i, 0))
    grp = pl.BlockSpec((OUT_TILE, GROUP_W), lambda i: (i, 0))
    lay = lambda shape: pl.BlockSpec((None,) + shape, lambda i: (layer,) + (0,) * len(shape))
    n_out = 2 if final else 1
    res = pl.pallas_call(
        functools.partial(_outproj_kernel, final=final),
        grid=(n_tok // OUT_TILE,),
        in_specs=[tok, grp, grp, grp,
                  pl.BlockSpec((OUT_TILE // S5_BLOCK, S5_ROW_W), lambda i: (i, 0)), grp,
                  lay((1, GROUP_W)), lay((GROUP_W, GROUP_W)), lay((1, GROUP_W)),
                  pl.BlockSpec((None, None, 1, ADA_CHUNKS * D_MODEL),
                               lambda i: (layer, cond_row(i * OUT_TILE), 0, 0)),
                  lay((1, D_MODEL)), lay((D_MODEL, D_MODEL)), lay((D_MODEL, D_FF)), lay((D_FF, D_MODEL)),
                  pl.BlockSpec((1, D_MODEL), lambda i: (0, 0))],
        out_specs=[tok] * n_out,
        out_shape=[jax.ShapeDtypeStruct((n_tok, D_MODEL), F32)] * n_out,
        scratch_shapes=[pltpu.VMEM((GROUP_W // LANE, OUT_TILE, LANE), F32)],
        compiler_params=_params(1),
        name="outproj",
    )(x, *mix, *s5_par, ada4, g2, w_out, w1, w2, gf)
    return res


def _grid_pos_embed(n_tokens):
    rows = n_tokens // GRID_W
    r, col = jnp.meshgrid(jnp.arange(rows, dtype=F32), jnp.arange(GRID_W, dtype=F32), indexing='ij')
    r = r.reshape(-1)
    col = col.reshape(-1)
    quarter = D_MODEL // 4
    freq = jnp.exp(-math.log(POS_BASE) * jnp.arange(quarter, dtype=F32) / quarter)
    ar = r[:, None] * freq
    ac = col[:, None] * freq
    return jnp.concatenate([jnp.sin(ar), jnp.cos(ar), jnp.sin(ac), jnp.cos(ac)], axis=-1)


def _layer(x, l, n_seq, seq, cond_row, hg_s0, s5_s0re, s5_s0im, P, final):
    z, zs, zs_blk = _inproj_call(x, P['ada4'], P['norm1_g'], P['w_in'], l, cond_row)
    o_conv = _conv_call(z, P['conv_w'], P['conv_b'], P['conv_ln_g'], P['conv_ln_b'], l, n_seq, seq)
    o_gm = _gmlp_call(z, P['gmlp_norm_g'], P['gmlp_ws'], P['gmlp_bs'], l)
    o_hg, hg_end = _hgrn_call(z, P['hgrn_lb'], P['hgrn_norm_g'], hg_s0, l, n_seq, seq)
    y_s5, h_re, h_im = _s5_mix(zs_blk, P['s5'], s5_s0re, s5_s0im, l, n_seq, seq)
    res = _outproj_call(x, (o_conv, o_gm, o_hg, y_s5, zs), (P['s5_d'], P['s5_glu_w'], P['s5_glu_b']),
                        P['ada4'], P['norm2_g'], P['w_out'],
                        P['mlp_w1'], P['mlp_w2'], P['final_norm_g'], l, cond_row, final)
    return res, hg_end, h_re, h_im


def _prepare(w, hgrn_lb):
    vec = lambda a: a.astype(F32).reshape(DEPTH, 1, -1)
    hd = GROUP_W // GM_HEADS
    return {
        'norm1_g': vec(w['norm1_g']), 'norm2_g': vec(w['norm2_g']),
        'w_in': w['w_in'].astype(BF16), 'w_out': w['w_out'].astype(BF16),
        'mlp_w1': w['mlp_w1'].astype(BF16), 'mlp_w2': w['mlp_w2'].astype(BF16),
        'conv_w': jnp.broadcast_to(w['conv_w'].astype(F32)[:, :, None, :], (DEPTH, CONV_K, SUBLANE, GROUP_W)),
        'conv_b': vec(w['conv_b']), 'conv_ln_g': vec(w['conv_ln_g']), 'conv_ln_b': vec(w['conv_ln_b']),
        'gmlp_norm_g': vec(w['gmlp_norm_g']),
        'gmlp_ws': jnp.transpose(w['gmlp_ws'], (0, 2, 1, 3)).reshape(
            DEPTH, GM_CHUNK, GM_HEADS * GM_CHUNK).astype(BF16),
        'gmlp_bs': jnp.repeat(jnp.transpose(w['gmlp_bs'].astype(F32), (0, 2, 1)), hd, axis=2),
        'hgrn_lb': jnp.transpose(hgrn_lb, (1, 0, 2)), 'hgrn_norm_g': vec(w['hgrn_norm_g']),
        's5': _s5_constants(w['s5_a_re'], w['s5_a_im'], w['s5_log_dt'], w['s5_b_re'], w['s5_b_im'],
                            w['s5_c_re'], w['s5_c_im']),
        's5_d': vec(w['s5_d']), 's5_glu_w': w['s5_glu_w'].astype(BF16), 's5_glu_b': vec(w['s5_glu_b']),
        'final_norm_g': w['final_norm_g'].astype(F32).reshape(1, D_MODEL),
    }


def _run_stream(x, n_seq, seq, cond_row, hg_s0, s5_s0re, s5_s0im, P):
    hg_states, s5_re, s5_im = [], [], []
    y = None
    for l in range(DEPTH):
        final = l == DEPTH - 1
        res, hg_end, h_re, h_im = _layer(x, l, n_seq, seq, cond_row, hg_s0[l], s5_s0re[l], s5_s0im[l], P, final)
        x = res[0]
        if final:
            y = res[1]
        hg_states.append(hg_end)
        s5_re.append(h_re)
        s5_im.append(h_im)
    return y, hg_states, s5_re, s5_im


def kernel(x_prompt, x_sample, state_hgrn, state_s5_re, state_s5_im, c, c_ctx, norm1_g, norm2_g, ada_w, ada_b, w_in, conv_w, conv_b, conv_ln_g, conv_ln_b, gmlp_norm_g, gmlp_ws, gmlp_bs, hgrn_lb_logits, hgrn_norm_g, s5_a_re, s5_a_im, s5_log_dt, s5_b_re, s5_b_im, s5_c_re, s5_c_im, s5_d, s5_glu_w, s5_glu_b, w_out, mlp_w1, mlp_w2, final_norm_g):
    n_ctx, seq_ctx, _ = x_prompt.shape
    n_lat, seq_lat, _ = x_sample.shape
    assert n_lat + 1 <= COND_ROWS
    assert seq_ctx % IN_TILE == 0 or IN_TILE % seq_ctx == 0

    lb_soft = jax.nn.softmax(hgrn_lb_logits.astype(F32), axis=1)
    hgrn_lb = jnp.cumsum(lb_soft, axis=1) - lb_soft[:, :1]
    P = _prepare(dict(
        norm1_g=norm1_g, norm2_g=norm2_g, w_in=w_in, w_out=w_out, mlp_w1=mlp_w1, mlp_w2=mlp_w2,
        conv_w=conv_w, conv_b=conv_b, conv_ln_g=conv_ln_g, conv_ln_b=conv_ln_b,
        gmlp_norm_g=gmlp_norm_g, gmlp_ws=gmlp_ws, gmlp_bs=gmlp_bs, hgrn_norm_g=hgrn_norm_g,
        s5_a_re=s5_a_re, s5_a_im=s5_a_im, s5_log_dt=s5_log_dt, s5_b_re=s5_b_re, s5_b_im=s5_b_im,
        s5_c_re=s5_c_re, s5_c_im=s5_c_im, s5_d=s5_d, s5_glu_w=s5_glu_w, s5_glu_b=s5_glu_b,
        final_norm_g=final_norm_g), hgrn_lb)
    cond =jnp.zeros((COND_ROWS, D_MODEL), F32).at[0].set(c_ctx.astype(F32)).at[1:1 + n_lat].set(c.astype(F32))
    ada = _ada_call(cond, ada_w.astype(F32), ada_b.astype(F32))
    P['ada4'] = ada.reshape(DEPTH, COND_ROWS, 1, ADA_CHUNKS * D_MODEL)

    zeros_s5 = jnp.zeros((n_ctx, 2, 1, S5_STATE), F32)
    y_ctx, hg_ctx, re_ctx, im_ctx = _run_stream(
        x_prompt.astype(F32).reshape(n_ctx * seq_ctx, D_MODEL), n_ctx, seq_ctx, lambda r: 0,
        [None] * DEPTH, [zeros_s5] * DEPTH, [zeros_s5] * DEPTH, P)

    xs = x_sample.astype(F32) + _grid_pos_embed(seq_lat)[None]
    hg0 = [_hgrn_state_to_kernel(state_hgrn[:, l]) for l in range(DEPTH)]
    re0 = [state_s5_re[:, l].astype(F32).reshape(n_lat, 2, 1, S5_STATE) for l in range(DEPTH)]
    im0 = [state_s5_im[:, l].astype(F32).reshape(n_lat, 2, 1, S5_STATE) for l in range(DEPTH)]
    y_lat, _, _, _ = _run_stream(xs.reshape(n_lat * seq_lat, D_MODEL), n_lat, seq_lat,
                                 lambda r: 1 + r // seq_lat, hg0, re0, im0, P)

    dt = x_prompt.dtype
    new_hg = jnp.stack([_hgrn_state_from_kernel(s) for s in hg_ctx], axis=1).astype(dt)
    new_re = jnp.stack([s.reshape(n_ctx, 2, S5_GROUPS, S5_P) for s in re_ctx], axis=1).astype(dt)
    new_im = jnp.stack([s.reshape(n_ctx, 2, S5_GROUPS, S5_P) for s in im_ctx], axis=1).astype(dt)
    return (y_ctx.reshape(n_ctx, seq_ctx, D_MODEL).astype(dt),
            y_lat.reshape(n_lat, seq_lat, D_MODEL).astype(x_sample.dtype), new_hg, new_re, new_im)
```

```python
import functools
import math

import numpy as np
import jax
import jax.numpy as jnp
from jax import lax
from jax.experimental import pallas as pl
from jax.experimental.pallas import tpu as pltpu

D_MODEL = 1024
DEPTH = 4
GRID_W = 64
GROUP_W = 256
N_IN_BLOCKS = 10
D_FF = 4 * D_MODEL
EPS = 1e-6
F_MIN = 1e-30
ADA_CHUNKS = 6
CONV_K = 31
CONV_PAD = CONV_K // 2
GM_CHUNK = 128
GM_HEADS = 4
HG_HEADS = 4
HG_DK = 64
HG_CHUNK = 64
HG_TILE_LEVELS = 3
S5_IN = 16
S5_GROUPS = 16
S5_P = 64
S5_STATE = S5_GROUPS * S5_P
POS_BASE = 10000.0

SUBLANE = 8
LANE = 128
COND_ROWS = 8
VMEM_LIMIT = 56 * 1024 * 1024

F32 = jnp.float32
BF16 = jnp.bfloat16


def _sigmoid(x):
    return 1.0 / (1.0 + jnp.exp(-x))


def _silu(x):
    return x * _sigmoid(x)


def _params(n_parallel=1):
    return pltpu.CompilerParams(dimension_semantics=("arbitrary",) * n_parallel,
                                vmem_limit_bytes=VMEM_LIMIT)


def _split2(x):
    hi = x.astype(BF16)
    lo = (x - hi.astype(F32)).astype(BF16)
    return hi, lo


def _dot(a, b):
    return jnp.dot(a, b, preferred_element_type=F32)


def _dot_nt(a, b):
    return lax.dot_general(a, b, (((1,), (1,)), ((), ())), preferred_element_type=F32)


def _dot_tn(a, b):
    return lax.dot_general(a, b, (((0,), (0,)), ((), ())), preferred_element_type=F32)


ADA_BLOCK = 1536


def _ada_kernel(cond_ref, w_ref, b_ref, o_ref):
    s = _silu(cond_ref[...])
    o_ref[...] = _dot(s.astype(BF16), w_ref[...].astype(BF16)) + b_ref[...]


def _ada_call(cond, ada_w, ada_b):
    n_out = ADA_CHUNKS * D_MODEL
    return pl.pallas_call(
        _ada_kernel,
        grid=(DEPTH, n_out // ADA_BLOCK),
        in_specs=[pl.BlockSpec((COND_ROWS, D_MODEL), lambda l, j: (0, 0)),
                  pl.BlockSpec((None, D_MODEL, ADA_BLOCK), lambda l, j: (l, 0, j)),
                  pl.BlockSpec((None, 1, ADA_BLOCK), lambda l, j: (l, 0, j))],
        out_specs=pl.BlockSpec((None, COND_ROWS, ADA_BLOCK), lambda l, j: (l, 0, j)),
        out_shape=jax.ShapeDtypeStruct((DEPTH, COND_ROWS, n_out), F32),
        compiler_params=_params(2),
        name="ada",
    )(cond, ada_w, ada_b.reshape(DEPTH, 1, n_out))


IN_TILE = 512
Z_MAIN_W = (N_IN_BLOCKS - 1) * GROUP_W


def _inproj_kernel(x_ref, ada_ref, g_ref, w_ref, z_ref, zs_ref, zsb_ref, half_ref):
    x = x_ref[...]
    ada = ada_ref[...]
    sh1 = ada[:, 0:D_MODEL]
    sc1 = ada[:, D_MODEL:2 * D_MODEL]
    ms = jnp.mean(x * x, axis=-1, keepdims=True)
    h = x * lax.rsqrt(ms + EPS) * (g_ref[...] * (1.0 + sc1)) + sh1
    z = _dot(h.astype(BF16), w_ref[...])
    z_ref[...] = z[:, 0:Z_MAIN_W]
    zs_ref[...] = z[:, Z_MAIN_W:]
    for h in range(GROUP_W // LANE):
        half_ref[h] = z[:, Z_MAIN_W + h * LANE:Z_MAIN_W + (h + 1) * LANE]
    for j in range(S5_BLOCK):
        for h in range(GROUP_W // LANE):
            c0 = _s5_col(h, j)
            zsb_ref[:, c0:c0 + LANE] = (
                half_ref[h, pl.ds(j, IN_TILE // S5_BLOCK, stride=S5_BLOCK), :].astype(BF16))


def _inproj_call(x, ada4, g1, w_in, layer, cond_row):
    n_tok = x.shape[0]
    zw = N_IN_BLOCKS * GROUP_W
    tile = lambda w: pl.BlockSpec((IN_TILE, w), lambda i: (i, 0))
    return pl.pallas_call(
        _inproj_kernel,
        grid=(n_tok // IN_TILE,),
        in_specs=[pl.BlockSpec((IN_TILE, D_MODEL), lambda i: (i, 0)),
                  pl.BlockSpec((None, None, 1, ADA_CHUNKS * D_MODEL),
                               lambda i: (layer, cond_row(i * IN_TILE), 0, 0)),
                  pl.BlockSpec((None, 1, D_MODEL), lambda i: (layer, 0, 0)),
                  pl.BlockSpec((None, D_MODEL, zw), lambda i: (layer, 0, 0))],
        out_specs=[tile(Z_MAIN_W), tile(GROUP_W),
                   pl.BlockSpec((IN_TILE // S5_BLOCK, S5_ROW_W), lambda i: (i, 0))],
        out_shape=[jax.ShapeDtypeStruct((n_tok, Z_MAIN_W), F32),
                   jax.ShapeDtypeStruct((n_tok, GROUP_W), F32),
                   jax.ShapeDtypeStruct((n_tok // S5_BLOCK, S5_ROW_W), BF16)],
        scratch_shapes=[pltpu.VMEM((GROUP_W // LANE, IN_TILE, LANE), F32)],
        compiler_params=_params(1),
        name="inproj",
    )(x, ada4, g1, w_in)


CONV_TILE = 64
CONV_HALO = 16
CONV_FILL = 128
CONV_NORM_TILE = 256


def _conv_kernel(z_ref, w_ref, b_ref, lng_ref, lnb_ref, o_ref, pad_ref, *, seq):
    tail = seq + 2 * CONV_HALO - (seq + SUBLANE)
    for r in range(SUBLANE):
        pad_ref[r, 0:CONV_HALO, :] = jnp.zeros((CONV_HALO, GROUP_W), F32)
        pad_ref[r, seq + SUBLANE:seq + 2 * CONV_HALO, :] = jnp.zeros((tail, GROUP_W), F32)
    for t0 in range(0, seq, CONV_FILL):
        u = z_ref[t0:t0 + CONV_FILL, 0:GROUP_W] * _sigmoid(z_ref[t0:t0 + CONV_FILL, GROUP_W:2 * GROUP_W])
        for r in range(SUBLANE):
            pad_ref[r, CONV_HALO - r + t0:CONV_HALO - r + t0 + CONV_FILL, :] = u
    first = CONV_HALO - CONV_PAD

    def tile(i, carry):
        r0 = pl.multiple_of(i * CONV_TILE, CONV_TILE)
        acc = jnp.zeros((CONV_TILE, GROUP_W), F32)
        for r in range(SUBLANE):
            taps = [k for k in range(CONV_K) if (first + k) % SUBLANE == r]
            lo = first + taps[0] - r
            hi = first + taps[-1] - r
            win = pad_ref[r, pl.ds(r0 + lo, CONV_TILE + hi - lo), :]
            for k in taps:
                off = first + k - r - lo
                wk = jnp.concatenate([w_ref[k]] * (CONV_TILE // SUBLANE), axis=0)
                acc = acc + wk * win[off:off + CONV_TILE, :]
        o_ref[pl.ds(r0, CONV_TILE), :] = acc + b_ref[...]
        return carry

    lax.fori_loop(0, seq // CONV_TILE, tile, 0)

    def norm(i, carry):
        rows = pl.ds(pl.multiple_of(i * CONV_NORM_TILE, CONV_NORM_TILE), CONV_NORM_TILE)
        c = o_ref[rows, :]
        mu = jnp.mean(c, axis=-1, keepdims=True)
        cc = c - mu
        var = jnp.mean(cc * cc, axis=-1, keepdims=True)
        y = cc * lax.rsqrt(var + EPS) * lng_ref[...] + lnb_ref[...]
        o_ref[rows, :] = _silu(y)
        return carry

    lax.fori_loop(0, seq // CONV_NORM_TILE, norm, 0)


def _conv_call(z, conv_w, conv_b, ln_g, ln_b, layer, n_seq, seq):
    vec = pl.BlockSpec((None, 1, GROUP_W), lambda s: (layer, 0, 0))
    return pl.pallas_call(
        functools.partial(_conv_kernel, seq=seq),
        grid=(n_seq,),
        in_specs=[pl.BlockSpec((seq, 2 * GROUP_W), lambda s: (s, 0)),
                  pl.BlockSpec((None, CONV_K, SUBLANE, GROUP_W), lambda s: (layer, 0, 0, 0)),
                  vec, vec, vec],
        out_specs=pl.BlockSpec((seq, GROUP_W), lambda s: (s, 0)),
        out_shape=jax.ShapeDtypeStruct((n_seq * seq, GROUP_W), F32),
        scratch_shapes=[pltpu.VMEM((SUBLANE, seq + 2 * CONV_HALO, GROUP_W), F32)],
        compiler_params=_params(1),
        name="conv",
    )(z, conv_w, conv_b, ln_g, ln_b)


def _head_mask(rows_per_head, cols_per_head, n_rows, n_cols):
    r = lax.broadcasted_iota(jnp.int32, (n_rows, n_cols), 0) // rows_per_head
    c = lax.broadcasted_iota(jnp.int32, (n_rows, n_cols), 1) // cols_per_head
    return r == c


GM_TILE = 1024


def _gmlp_kernel(z_ref, g_ref, ws_ref, bs_ref, o_ref):
    hd = GROUP_W // GM_HEADS
    mask = _head_mask(GM_CHUNK, hd, GM_HEADS * GM_CHUNK, GROUP_W)
    for r0 in range(0, GM_TILE, GM_CHUNK):
        u = z_ref[r0:r0 + GM_CHUNK, 0:GROUP_W]
        v = z_ref[r0:r0 + GM_CHUNK, GROUP_W:2 * GROUP_W]
        vn = v * lax.rsqrt(jnp.mean(v * v, axis=-1, keepdims=True) + EPS) * g_ref[...]
        stack = jnp.where(mask, jnp.concatenate([vn] * GM_HEADS, axis=0), 0.0).astype(BF16)
        sv = _dot(ws_ref[...], stack) + bs_ref[...]
        o_ref[r0:r0 + GM_CHUNK, :] = u * sv


def _gmlp_call(z, norm_g, ws_cat, bs_full, layer):
    n_tok = z.shape[0]
    return pl.pallas_call(
        _gmlp_kernel,
        grid=(n_tok // GM_TILE,),
        in_specs=[pl.BlockSpec((GM_TILE, 2 * GROUP_W), lambda i: (i, 1)),
                  pl.BlockSpec((None, 1, GROUP_W), lambda i: (layer, 0, 0)),
                  pl.BlockSpec((None, GM_CHUNK, GM_HEADS * GM_CHUNK), lambda i: (layer, 0, 0)),
                  pl.BlockSpec((None, GM_CHUNK, GROUP_W), lambda i: (layer, 0, 0))],
        out_specs=pl.BlockSpec((GM_TILE, GROUP_W), lambda i: (i, 0)),
        out_shape=jax.ShapeDtypeStruct((n_tok, GROUP_W), F32),
        compiler_params=_params(1),
        name="gmlp",
    )(z, norm_g, ws_cat, bs_full)


def _hgrn_block_masks():
    t = np.arange(HG_CHUNK)
    blk = np.zeros((HG_TILE_LEVELS, HG_CHUNK, HG_CHUNK), np.float32)
    for lvl in range(HG_TILE_LEVELS):
        b = t >> (lvl + 1)
        blk[lvl] = (b[:, None] == b[None, :])
    return np.tile(blk, (1, 1, HG_HEADS))


_HG_BMASK = _hgrn_block_masks()
HG_NORM_TILE = 256


def _hgrn_kernel(zq_ref, zi_ref, zg_ref, zff_ref, zfb_ref, lb_ref, ng_ref, bmask_ref, *rest,
                 seq, n_seqs, has_state):
    s0_ref = rest[0] if has_state else None
    o_ref, s_out_ref, ob_ref, st_ref = rest[-4:]
    n = HG_CHUNK
    n_chunks = seq // n
    hmask = _head_mask(HG_DK, HG_DK, GROUP_W, GROUP_W)
    hmask_bf = jnp.where(hmask, 1.0, 0.0).astype(BF16)
    row = lax.broadcasted_iota(jnp.int32, (n, GROUP_W), 0)
    n_tiles = n // SUBLANE
    col_tile = (lax.broadcasted_iota(jnp.int32, (SUBLANE, GROUP_W), 1) % n) // SUBLANE
    sub3 = lax.broadcasted_iota(jnp.int32, (n_tiles, SUBLANE, GROUP_W), 1)

    def block_diag(a):
        return jnp.where(hmask, jnp.concatenate([a] * HG_HEADS, axis=0), 0.0).astype(BF16)

    def spread_rows(a, first, period):
        return jnp.concatenate([jnp.broadcast_to(a[r:r + 1, :], (period, GROUP_W))
                                for r in range(first, n, period)], axis=0)

    def boundary(cum, lvl, direction):
        m = 1 << lvl
        at = m - 1 if direction == 0 else m
        if 2 * m >= SUBLANE:
            return spread_rows(cum, at, 2 * m)
        lo = spread_rows(cum, at, SUBLANE)
        hi = spread_rows(cum, at + 2 * m, SUBLANE)
        return jnp.where((row & (2 * m)) == 0, lo, hi)

    def chunk_step(s, direction, c, zf_ref, dst_ref):
        lb = lb_ref[direction:direction + 1, :]
        end_row = n - 1 if direction == 0 else 0
        pos = row if direction == 0 else n - 1 - row
        rows = pl.ds(pl.multiple_of(s * seq + c * n, n), n)
        q = _silu(zq_ref[rows, :])
        v = zi_ref[rows, :]
        f = lb + (1.0 - lb) * _sigmoid(zf_ref[rows, :])
        k = 1.0 - f
        f = jnp.maximum(f, F_MIN)
        cum3 = jnp.log2(f).reshape(n_tiles, SUBLANE, GROUP_W)
        for d in (1, 2, 4):
            shifted = pltpu.roll(cum3, d if direction == 0 else SUBLANE - d, 1)
            live = sub3 >= d if direction == 0 else sub3 < SUBLANE - d
            cum3 = cum3 + jnp.where(live, shifted, 0.0)
        edge = SUBLANE - 1 if direction == 0 else 0
        order = range(n_tiles) if direction == 0 else range(n_tiles - 1, -1, -1)
        tiles, run = [None] * n_tiles, None
        for t in order:
            tiles[t] = cum3[t] if run is None else cum3[t] + run
            run = jnp.broadcast_to(tiles[t][edge:edge + 1, :], (SUBLANE, GROUP_W))
        cum = jnp.concatenate(tiles, axis=0)
        total = cum[end_row:end_row + 1]
        w_cum = jnp.exp2(cum)
        w_rem = jnp.exp2(total - cum)
        w_end = jnp.exp2(total)
        st = st_ref[s, direction]
        inter = _dot_nt((q * w_cum).astype(BF16), st.astype(BF16))

        tile_end = spread_rows(cum, edge, SUBLANE)
        k_far = k * jnp.exp2(tile_end - cum)
        pairs = [(j, i) for j in range(n_tiles)
                 for i in (range(j + 1, n_tiles) if direction == 0 else range(j))]
        tile = lambda a, i: a[i * SUBLANE:(i + 1) * SUBLANE, :]
        q_far = jnp.concatenate([tile(q, i) * jnp.exp2(tile(cum, i) - tile(tile_end, j)) for j, i in pairs], axis=0)
        far = _dot_nt(q_far.astype(BF16), block_diag(k_far))
        acc = [jnp.zeros((SUBLANE, GROUP_W), F32)] * n_tiles
        for p, (j, i) in enumerate(pairs):
            acc[i] = acc[i] + jnp.where(col_tile == j, tile(far, p), 0.0)
        scores = jnp.concatenate(acc, axis=0)

        for lvl in range(HG_TILE_LEVELS):
            upper = (pos & (1 << lvl)) != 0
            if lvl == 0:
                wl = jnp.where(upper, f, 1.0)
            else:
                ref = boundary(cum, lvl, direction)
                wl = jnp.exp2(jnp.where(upper, cum - ref, ref - cum))
            ql = jnp.where(upper, q * wl, 0.0).astype(BF16)
            kl = jnp.where(upper, 0.0, k * wl)
            scores = scores + bmask_ref[lvl] * _dot_nt(ql, block_diag(kl))
        diag = _dot((q * k).astype(BF16), hmask_bf)
        intra = _dot(scores.astype(BF16), block_diag(v)) + diag * v
        upd = _dot_tn(v.astype(BF16), (k * w_rem).astype(BF16))
        st_ref[s, direction] = st * w_end + jnp.where(hmask, upd, 0.0)
        dst_ref[rows, :] = inter + intra

    st_ref[...] = s0_ref[...] if has_state else jnp.zeros(st_ref.shape, F32)

    def both(i, carry):
        for s in range(n_seqs):
            chunk_step(s, 0, i, zff_ref, o_ref)
            chunk_step(s, 1, n_chunks - 1 - i, zfb_ref, ob_ref)
        return carry

    lax.fori_loop(0, n_chunks, both, 0)

    pick = (lax.broadcasted_iota(jnp.int32, (GROUP_W, HG_DK), 0) % HG_DK
            == lax.broadcasted_iota(jnp.int32, (GROUP_W, HG_DK), 1))
    pick_bf = jnp.where(pick, 1.0, 0.0).astype(BF16)
    for s in range(n_seqs):
        for direction in range(2):
            hi, lo = _split2(st_ref[s, direction])
            s_out_ref[s, direction] = _dot_tn(hi, pick_bf) + _dot_tn(lo, pick_bf)

    def finish(i, carry):
        rows = pl.ds(pl.multiple_of(i * HG_NORM_TILE, HG_NORM_TILE), HG_NORM_TILE)
        o = o_ref[rows, :] + ob_ref[rows, :]
        hi, mid = _split2(o * o)
        ms = (_dot(hi, hmask_bf) + _dot(mid, hmask_bf)) * (1.0 / HG_DK)
        y = o * lax.rsqrt(ms + EPS) * ng_ref[...]
        o_ref[rows, :] = y * _silu(zg_ref[rows, :])
        return carry

    lax.fori_loop(0, n_seqs * seq // HG_NORM_TILE, finish, 0)


HG_ROWS_PER_STEP = 1024
HG_MIN_SEQS_PER_STEP = 2


def _hgrn_call(z, lb, norm_g, s0t, layer, n_seq, seq):
    per_step = min(n_seq, max(HG_MIN_SEQS_PER_STEP, HG_ROWS_PER_STEP // seq))
    assert n_seq % per_step == 0
    rows = per_step * seq
    mode = pl.Buffered(1) if rows > HG_ROWS_PER_STEP else None

    def zcol(j):
        return pl.BlockSpec((rows, GROUP_W), lambda s: (s, j), pipeline_mode=mode)

    full = lambda shape: pl.BlockSpec(shape, lambda s: (0,) * len(shape))
    has_state = s0t is not None
    state_specs = [pl.BlockSpec((per_step, 2, GROUP_W, GROUP_W), lambda s: (s, 0, 0, 0))] if has_state else []
    state_args = (s0t,) if has_state else ()
    return pl.pallas_call(
        functools.partial(_hgrn_kernel, seq=seq, n_seqs=per_step, has_state=has_state),
        grid=(n_seq // per_step,),
        in_specs=[zcol(4), zcol(5), zcol(6), zcol(7), zcol(8),
                  pl.BlockSpec((None, 2, GROUP_W), lambda s: (layer, 0, 0)),
                  pl.BlockSpec((None, 1, GROUP_W), lambda s: (layer, 0, 0)),
                  full((HG_TILE_LEVELS, HG_CHUNK, GROUP_W))] + state_specs,
        out_specs=[pl.BlockSpec((rows, GROUP_W), lambda s: (s, 0)),
                   pl.BlockSpec((per_step, 2, GROUP_W, HG_DK), lambda s: (s, 0, 0, 0))],
        out_shape=[jax.ShapeDtypeStruct((n_seq * seq, GROUP_W), F32),
                   jax.ShapeDtypeStruct((n_seq, 2, GROUP_W, HG_DK), F32)],
        scratch_shapes=[pltpu.VMEM((rows, GROUP_W), F32), pltpu.VMEM((per_step, 2, GROUP_W, GROUP_W), F32)],
        compiler_params=_params(1),
        name="hgrn",
    )(z, z, z, z, z, lb, norm_g, jnp.asarray(_HG_BMASK), *state_args)


def _hgrn_state_to_kernel(s):
    st = jnp.swapaxes(s.astype(F32), -1, -2)
    eye = jnp.eye(HG_HEADS, dtype=F32)
    full = jnp.einsum('ndhvk,hg->ndhvgk', st, eye)
    return full.reshape(s.shape[0], 2, GROUP_W, GROUP_W)


def _hgrn_state_from_kernel(st):
    return st.reshape(st.shape[0], 2, HG_HEADS, HG_DK, HG_DK)


S5_BLOCK = 8
S5_ROW_W = S5_BLOCK * GROUP_W
S5_HALVES = GROUP_W // LANE
S5_HALF_W = S5_BLOCK * LANE
S5_GROUPS_PER_HALF = S5_GROUPS // S5_HALVES
S5_PARTS = 4
S5_XW = S5_PARTS * S5_STATE
S5_PART_HALF = S5_STATE // S5_HALVES
S5_STEPS = (1, 2, 4)
S5_TABLES = 2 * (len(S5_STEPS) + 1)


def _s5_col(half, token):
    return half * S5_HALF_W + token * LANE


def _s5_in_kernel(u_ref, w_ref, x_ref):
    x_ref[...] = _dot(u_ref[...], w_ref[...])


def _s5_in_call(ub, wx, layer):
    n_blk = ub.shape[0]
    return pl.pallas_call(
        _s5_in_kernel,
        grid=(S5_HALVES, S5_PARTS),
        in_specs=[pl.BlockSpec((n_blk, S5_HALF_W), lambda h, p: (0, h)),
                  pl.BlockSpec((None, None, S5_HALF_W, S5_PART_HALF), lambda h, p: (layer, h, 0, p))],
        out_specs=pl.BlockSpec((n_blk, S5_PART_HALF), lambda h, p: (0, p * S5_HALVES + h)),
        out_shape=jax.ShapeDtypeStruct((n_blk, S5_XW), F32),
        compiler_params=_params(2),
        name="s5_in",
    )(ub, wx)


def _s5_scan_kernel(x_ref, tab_ref, s0re_ref, s0im_ref, h_ref, hre_ref, him_ref, *, n_blk):
    n_pairs = n_blk // (2 * SUBLANE)
    row = lax.broadcasted_iota(jnp.int32, (SUBLANE, S5_STATE), 0)
    n_steps = len(S5_STEPS)

    for direction in range(2):
        c_re = 2 * direction * S5_STATE
        c_im = c_re + S5_STATE
        edge = 0 if direction == 0 else SUBLANE - 1
        last = SUBLANE - 1 - edge
        unit = 0 if direction == 0 else SUBLANE - 1

        def tile(rows, carry):
            hc_r, hc_i, px_r, px_i = carry
            xr = x_ref[rows, c_re:c_re + S5_STATE]
            xi = x_ref[rows, c_im:c_im + S5_STATE]
            shift = 1 if direction == 0 else SUBLANE - 1
            hr = jnp.where(row == edge, px_r, pltpu.roll(xr, shift, 0))
            hi = jnp.where(row == edge, px_i, pltpu.roll(xi, shift, 0))
            for s, step in enumerate(S5_STEPS):
                sh = step if direction == 0 else SUBLANE - step
                rr = pltpu.roll(hr, sh, 0)
                ri = pltpu.roll(hi, sh, 0)
                ar = tab_ref[direction, 2 * s]
                ai = tab_ref[direction, 2 * s + 1]
                hr, hi = hr + ar * rr - ai * ri, hi + ar * ri + ai * rr
            pr = tab_ref[direction, 2 * n_steps]
            pi = tab_ref[direction, 2 * n_steps + 1]
            hr, hi = hr + pr * hc_r - pi * hc_i, hi + pr * hc_i + pi * hc_r
            bc = lambda a: jnp.broadcast_to(a[last:last + 1, :], (SUBLANE, S5_STATE))
            return hr, hi, (bc(hr), bc(hi), bc(xr), bc(xi))

        def pair(j, carry):
            jj = j if direction == 0 else n_pairs - 1 - j
            base = pl.multiple_of(jj * 2 * SUBLANE, 2 * SUBLANE)
            offs = (0, SUBLANE) if direction == 0 else (SUBLANE, 0)
            out = {}
            for off in offs:
                hr, hi, carry = tile(pl.ds(base + off, SUBLANE), carry)
                out[off] = (hr, hi)
            both = pl.ds(base, 2 * SUBLANE)
            h_ref[both, c_re:c_re + S5_STATE] = jnp.concatenate([out[0][0], out[SUBLANE][0]], 0).astype(BF16)
            h_ref[both, c_im:c_im + S5_STATE] = jnp.concatenate([out[0][1], out[SUBLANE][1]], 0).astype(BF16)
            return carry

        zero = jnp.zeros((SUBLANE, S5_STATE), F32)
        init = (zero, zero,
                jnp.broadcast_to(s0re_ref[direction], (SUBLANE, S5_STATE)),
                jnp.broadcast_to(s0im_ref[direction], (SUBLANE, S5_STATE)))
        hc_r, hc_i, px_r, px_i = lax.fori_loop(0, n_pairs, pair, init)
        a_r = tab_ref[direction, 2 * n_steps][unit:unit + 1, :]
        a_i = tab_ref[direction, 2 * n_steps + 1][unit:unit + 1, :]
        hre_ref[direction] = a_r * hc_r[0:1, :] - a_i * hc_i[0:1, :] + px_r[0:1, :]
        him_ref[direction] = a_r * hc_i[0:1, :] + a_i * hc_r[0:1, :] + px_i[0:1, :]


def _s5_scan_call(x, tab, s0re, s0im, layer, n_seq, n_blk):
    state = pl.BlockSpec((None, 2, 1, S5_STATE), lambda s: (s, 0, 0, 0))
    return pl.pallas_call(
        functools.partial(_s5_scan_kernel, n_blk=n_blk),
        grid=(n_seq,),
        in_specs=[pl.BlockSpec((n_blk, S5_XW), lambda s: (s, 0)),
                  pl.BlockSpec((None, 2, S5_TABLES, SUBLANE, S5_STATE), lambda s: (layer, 0, 0, 0, 0)),
                  state, state],
        out_specs=[pl.BlockSpec((n_blk, S5_XW), lambda s: (s, 0)), state, state],
        out_shape=[jax.ShapeDtypeStruct((n_seq * n_blk, S5_XW), BF16),
                   jax.ShapeDtypeStruct((n_seq, 2, 1, S5_STATE), F32),
                   jax.ShapeDtypeStruct((n_seq, 2, 1, S5_STATE), F32)],
        compiler_params=_params(1),
        name="s5_scan",
    )(x, tab, s0re, s0im)


def _s5_out_kernel(u_ref, *refs):
    h_refs, (wt_ref, wc_ref, y_ref) = refs[:S5_PARTS], refs[S5_PARTS:]
    y = _dot(u_ref[...], wt_ref[...])
    for p in range(S5_PARTS):
        y = y + _dot(h_refs[p][...], wc_ref[p * S5_PART_HALF:(p + 1) * S5_PART_HALF, :])
    y_ref[...] = y


def _s5_out_call(ub, h, wt, wc, layer):
    n_blk = ub.shape[0]

    def state_part(p):
        return pl.BlockSpec((n_blk, S5_PART_HALF), lambda j: (0, p * S5_HALVES + j))

    return pl.pallas_call(
        _s5_out_kernel,
        grid=(S5_HALVES,),
        in_specs=[pl.BlockSpec((n_blk, S5_HALF_W), lambda j: (0, j))]
                 + [state_part(p) for p in range(S5_PARTS)]
                 + [pl.BlockSpec((None, None, S5_HALF_W, S5_HALF_W), lambda j: (layer, j, 0, 0)),
                    pl.BlockSpec((None, None, S5_PARTS * S5_PART_HALF, S5_HALF_W), lambda j: (layer, j, 0, 0))],
        out_specs=pl.BlockSpec((n_blk, S5_HALF_W), lambda j: (0, j)),
        out_shape=jax.ShapeDtypeStruct((n_blk, S5_ROW_W), F32),
        compiler_params=_params(1),
        name="s5_out",
    )(ub, *([h] * S5_PARTS), wt, wc)


def _s5_mix(ub, consts, s0re, s0im, layer, n_seq, seq):
    wx, wt, wc, tab = consts
    n_blk = seq // S5_BLOCK
    x = _s5_in_call(ub, wx, layer)
    h, h_re, h_im = _s5_scan_call(x, tab, s0re, s0im, layer, n_seq, n_blk)
    y = _s5_out_call(ub, h, wt, wc, layer)
    return y, h_re, h_im


S5_EXPAND_ROWS = 512


def _expand_kernel(c_ref, t_ref, o_ref, *, row_group, col_group):
    rows, cols = o_ref.shape
    r = lax.broadcasted_iota(jnp.int32, (rows, cols), 0) + pl.program_id(1) * rows
    col = lax.broadcasted_iota(jnp.int32, (rows, cols), 1)
    group = lambda i, period_width: (i % period_width[0]) // period_width[1]
    same_group = group(r, row_group) == group(col, col_group)
    o_ref[...] = jnp.where(same_group, _dot(c_ref[...], t_ref[...]), 0.0).astype(BF16)


def _expand_call(compact, spread, row_group, col_group):
    n_ops, n_rows, k = compact.shape
    n_cols = spread.shape[1]
    return pl.pallas_call(
        functools.partial(_expand_kernel, row_group=row_group, col_group=col_group),
        grid=(n_ops, n_rows // S5_EXPAND_ROWS),
        in_specs=[pl.BlockSpec((None, S5_EXPAND_ROWS, k), lambda l, i: (l, i, 0)),
                  pl.BlockSpec((k, n_cols), lambda l, i: (0, 0))],
        out_specs=pl.BlockSpec((None, S5_EXPAND_ROWS, n_cols), lambda l, i: (l, i, 0)),
        out_shape=jax.ShapeDtypeStruct((n_ops, n_rows, n_cols), BF16),
        compiler_params=_params(2),
        name="s5_expand",
    )(compact, spread)


def _s5_constants(a_re, a_im, log_dt, b_re, b_im, c_re, c_im):
    n = S5_BLOCK
    a = lax.complex(a_re.astype(F32), a_im.astype(F32))
    dt = jnp.exp(log_dt.astype(F32))[..., None]
    a_bar = jnp.exp(a * dt)
    b_bar = ((a_bar - 1.0) / a)[..., None] * lax.complex(b_re.astype(F32), b_im.astype(F32))
    c = lax.complex(c_re.astype(F32), c_im.astype(F32))
    pw = [jnp.ones_like(a_bar)]
    for _ in range(n):
        pw.append(pw[-1] * a_bar)
    pw = jnp.stack(pw, axis=2)
    gh = S5_GROUPS_PER_HALF
    kron = lambda *m: functools.reduce(np.kron, m)
    spread_state = jnp.asarray(kron(np.eye(S5_PARTS), np.ones((1, gh)), np.eye(S5_P)), BF16)
    spread_token = jnp.asarray(kron(np.eye(n), np.ones((1, gh)), np.eye(S5_IN)), BF16)
    token_group = (LANE, S5_IN)
    state_group = (S5_PART_HALF, S5_P)
    halves = lambda a: a.reshape(a.shape[:3] + (S5_HALVES, gh) + a.shape[4:])

    w_f = pw[:, 0, ::-1][:, 1:, :, :, None] * b_bar[:, 0, None]
    w_b = pw[:, 1, :n, :, :, None] * b_bar[:, 1, None]
    w4 = jnp.stack([jnp.real(w_f), jnp.imag(w_f), jnp.real(w_b), jnp.imag(w_b)], axis=1)
    wx_c = jnp.transpose(halves(w4), (0, 3, 2, 4, 6, 1, 5))
    wx_c = wx_c.reshape(DEPTH * S5_HALVES, S5_HALF_W, S5_PARTS * S5_P)
    wx = _expand_call(wx_c.astype(BF16), spread_state, token_group, state_group)
    wx = wx.reshape(DEPTH, S5_HALVES, S5_HALF_W, S5_PARTS * S5_PART_HALF)

    o_f = c[:, 0, None] * pw[:, 0, 1:, :, None, :]
    o_b = c[:, 1, None] * pw[:, 1, ::-1][:, :n, :, None, :]
    o4 = jnp.stack([jnp.real(o_f), -jnp.imag(o_f), jnp.real(o_b), -jnp.imag(o_b)], axis=1)
    wc_c = jnp.transpose(halves(o4), (0, 3, 1, 4, 6, 2, 5))
    wc_c = wc_c.reshape(DEPTH * S5_HALVES, S5_PARTS * S5_PART_HALF, n * S5_IN)
    wc = _expand_call(wc_c.astype(BF16), spread_token, state_group, token_group)
    wc = wc.reshape(DEPTH, S5_HALVES, S5_PARTS * S5_PART_HALF, S5_HALF_W)

    k_f = jnp.real(jnp.einsum('lgcp,ltgp,lgpd->lgdtc', c[:, 0], pw[:, 0, :n], b_bar[:, 0]))
    k_b = jnp.real(jnp.einsum('lgcp,ltgp,lgpd->lgdtc', c[:, 1], pw[:, 1, :n], b_bar[:, 1]))
    tt, jj, ii = np.meshgrid(np.arange(n), np.arange(n), np.arange(n), indexing='ij')
    place = lambda hit: jnp.asarray(np.kron(hit.reshape(n, n * n).astype(np.float32), np.eye(S5_IN)), F32)
    flat = lambda k: k.reshape(DEPTH * S5_GROUPS * S5_IN, n * S5_IN)
    t_all = (jnp.dot(flat(k_f), place(ii - jj == tt), precision=lax.Precision.HIGHEST)
             + jnp.dot(flat(k_b), place(jj - ii == tt), precision=lax.Precision.HIGHEST))
    wt_c = t_all.reshape(DEPTH, S5_HALVES, gh, S5_IN, n, n * S5_IN)
    wt_c = jnp.transpose(wt_c, (0, 1, 4, 2, 3, 5)).reshape(DEPTH * S5_HALVES, S5_HALF_W, n * S5_IN)
    wt = _expand_call(wt_c.astype(BF16), spread_token, token_group, token_group)
    wt = wt.reshape(DEPTH, S5_HALVES, S5_HALF_W, S5_HALF_W)

    a_blk = pw[:, :, n].reshape(DEPTH, 2, S5_STATE)
    pows = [a_blk]
    for _ in range(SUBLANE - 1):
        pows.append(pows[-1] * a_blk)
    row = jnp.arange(SUBLANE)
    tabs = []
    for direction in range(2):
        t = []
        for step in S5_STEPS:
            live = (row >= step) if direction == 0 else (row < SUBLANE - step)
            t.append(jnp.where(live[None, :, None], pows[step - 1][:, direction, None, :], 0.0))
        order = range(SUBLANE) if direction == 0 else range(SUBLANE - 1, -1, -1)
        t.append(jnp.stack([pows[i][:, direction] for i in order], axis=1))
        planes = []
        for x in t:
            planes += [jnp.real(x), jnp.imag(x)]
        tabs.append(jnp.stack(planes, axis=1))
    tab = jnp.stack(tabs, axis=1).astype(F32)
    return wx, wt, wc, tab


OUT_TILE = 256
FF_BLOCK = 1024


def _outproj_kernel(x_ref, oa_ref, ob_ref, oc_ref, ys_ref, zs_ref, sd_ref, gw_ref, gb_ref,
                    ada_ref, g2_ref, wo_ref, w1_ref, w2_ref, gf_ref, *rest, final):
    out_refs, tok_ref = rest[:-1], rest[-1]
    ada = ada_ref[...]
    chunk = lambda j: ada[:, j * D_MODEL:(j + 1) * D_MODEL]
    gate1, sh2, sc2, gate2 = chunk(2), chunk(3), chunk(4), chunk(5)
    for j in range(S5_BLOCK):
        for h in range(GROUP_W // LANE):
            c0 = _s5_col(h, j)
            tok_ref[h, pl.ds(j, OUT_TILE // S5_BLOCK, stride=S5_BLOCK), :] = ys_ref[:, c0:c0 + LANE]
    ssm = jnp.concatenate([tok_ref[h] for h in range(GROUP_W // LANE)], axis=-1)
    ys = jax.nn.gelu(ssm + sd_ref[...] * zs_ref[...])
    od = ys * _sigmoid(_dot(ys.astype(BF16), gw_ref[...]) + gb_ref[...])
    mixed = jnp.concatenate([oa_ref[...], ob_ref[...], oc_ref[...], od], axis=-1).astype(BF16)
    x = x_ref[...] + gate1 * _dot(mixed, wo_ref[...])
    ms = jnp.mean(x * x, axis=-1, keepdims=True)
    h = (x * lax.rsqrt(ms + EPS) * (g2_ref[...] * (1.0 + sc2)) + sh2).astype(BF16)
    acc = jnp.zeros((OUT_TILE, D_MODEL), F32)
    for j in range(D_FF // FF_BLOCK):
        cols = slice(j * FF_BLOCK, (j + 1) * FF_BLOCK)
        a = jnp.maximum(_dot(h, w1_ref[:, cols]), 0.0)
        acc = acc + _dot((a * a).astype(BF16), w2_ref[cols, :])
    x = x + gate2 * acc
    out_refs[0][...] = x
    if final:
        ms = jnp.mean(x * x, axis=-1, keepdims=True)
        out_refs[1][...] = x * lax.rsqrt(ms + EPS) * gf_ref[...]


def _outproj_call(x, mix, s5_par, ada4, g2, w_out, w1, w2, gf, layer, cond_row, final):
    n_tok = x.shape[0]
    tok = pl.BlockSpec((OUT_TILE, D_MODEL), lambda i: (i, 0))
    grp = pl.BlockSpec((OUT_TILE, GROUP_W), lambda i: (i, 0))
    lay = lambda shape: pl.BlockSpec((None,) + shape, lambda i: (layer,) + (0,) * len(shape))
    n_out = 2 if final else 1
    res = pl.pallas_call(
        functools.partial(_outproj_kernel, final=final),
        grid=(n_tok // OUT_TILE,),
        in_specs=[tok, grp, grp, grp,
                  pl.BlockSpec((OUT_TILE // S5_BLOCK, S5_ROW_W), lambda i: (i, 0)), grp,
                  lay((1, GROUP_W)), lay((GROUP_W, GROUP_W)), lay((1, GROUP_W)),
                  pl.BlockSpec((None, None, 1, ADA_CHUNKS * D_MODEL),
                               lambda i: (layer, cond_row(i * OUT_TILE), 0, 0)),
                  lay((1, D_MODEL)), lay((D_MODEL, D_MODEL)), lay((D_MODEL, D_FF)), lay((D_FF, D_MODEL)),
                  pl.BlockSpec((1, D_MODEL), lambda i: (0, 0))],
        out_specs=[tok] * n_out,
        out_shape=[jax.ShapeDtypeStruct((n_tok, D_MODEL), F32)] * n_out,
        scratch_shapes=[pltpu.VMEM((GROUP_W // LANE, OUT_TILE, LANE), F32)],
        compiler_params=_params(1),
        name="outproj",
    )(x, *mix, *s5_par, ada4, g2, w_out, w1, w2, gf)
    return res


def _grid_pos_embed(n_tokens):
    rows = n_tokens // GRID_W
    r, col = jnp.meshgrid(jnp.arange(rows, dtype=F32), jnp.arange(GRID_W, dtype=F32), indexing='ij')
    r = r.reshape(-1)
    col = col.reshape(-1)
    quarter = D_MODEL // 4
    freq = jnp.exp(-math.log(POS_BASE) * jnp.arange(quarter, dtype=F32) / quarter)
    ar = r[:, None] * freq
    ac = col[:, None] * freq
    return jnp.concatenate([jnp.sin(ar), jnp.cos(ar), jnp.sin(ac), jnp.cos(ac)], axis=-1)


def _layer(x, l, n_seq, seq, cond_row, hg_s0, s5_s0re, s5_s0im, P, final):
    z, zs, zs_blk = _inproj_call(x, P['ada4'], P['norm1_g'], P['w_in'], l, cond_row)
    o_conv = _conv_call(z, P['conv_w'], P['conv_b'], P['conv_ln_g'], P['conv_ln_b'], l, n_seq, seq)
    o_gm = _gmlp_call(z, P['gmlp_norm_g'], P['gmlp_ws'], P['gmlp_bs'], l)
    o_hg, hg_end = _hgrn_call(z, P['hgrn_lb'], P['hgrn_norm_g'], hg_s0, l, n_seq, seq)
    y_s5, h_re, h_im = _s5_mix(zs_blk, P['s5'], s5_s0re, s5_s0im, l, n_seq, seq)
    res = _outproj_call(x, (o_conv, o_gm, o_hg, y_s5, zs), (P['s5_d'], P['s5_glu_w'], P['s5_glu_b']),
                        P['ada4'], P['norm2_g'], P['w_out'],
                        P['mlp_w1'], P['mlp_w2'], P['final_norm_g'], l, cond_row, final)
    return res, hg_end, h_re, h_im


def _prepare(w, hgrn_lb):
    vec = lambda a: a.astype(F32).reshape(DEPTH, 1, -1)
    hd = GROUP_W // GM_HEADS
    return {
        'norm1_g': vec(w['norm1_g']), 'norm2_g': vec(w['norm2_g']),
        'w_in': w['w_in'].astype(BF16), 'w_out': w['w_out'].astype(BF16),
        'mlp_w1': w['mlp_w1'].astype(BF16), 'mlp_w2': w['mlp_w2'].astype(BF16),
        'conv_w': jnp.broadcast_to(w['conv_w'].astype(F32)[:, :, None, :], (DEPTH, CONV_K, SUBLANE, GROUP_W)),
        'conv_b': vec(w['conv_b']), 'conv_ln_g': vec(w['conv_ln_g']), 'conv_ln_b': vec(w['conv_ln_b']),
        'gmlp_norm_g': vec(w['gmlp_norm_g']),
        'gmlp_ws': jnp.transpose(w['gmlp_ws'], (0, 2, 1, 3)).reshape(
            DEPTH, GM_CHUNK, GM_HEADS * GM_CHUNK).astype(BF16),
        'gmlp_bs': jnp.repeat(jnp.transpose(w['gmlp_bs'].astype(F32), (0, 2, 1)), hd, axis=2),
        'hgrn_lb': jnp.transpose(hgrn_lb, (1, 0, 2)), 'hgrn_norm_g': vec(w['hgrn_norm_g']),
        's5': _s5_constants(w['s5_a_re'], w['s5_a_im'], w['s5_log_dt'], w['s5_b_re'], w['s5_b_im'],
                            w['s5_c_re'], w['s5_c_im']),
        's5_d': vec(w['s5_d']), 's5_glu_w': w['s5_glu_w'].astype(BF16), 's5_glu_b': vec(w['s5_glu_b']),
        'final_norm_g': w['final_norm_g'].astype(F32).reshape(1, D_MODEL),
    }


def _run_stream(x, n_seq, seq, cond_row, hg_s0, s5_s0re, s5_s0im, P):
    hg_states, s5_re, s5_im = [], [], []
    y = None
    for l in range(DEPTH):
        final = l == DEPTH - 1
        res, hg_end, h_re, h_im = _layer(x, l, n_seq, seq, cond_row, hg_s0[l], s5_s0re[l], s5_s0im[l], P, final)
        x = res[0]
        if final:
            y = res[1]
        hg_states.append(hg_end)
        s5_re.append(h_re)
        s5_im.append(h_im)
    return y, hg_states, s5_re, s5_im


def kernel(x_prompt, x_sample, state_hgrn, state_s5_re, state_s5_im, c, c_ctx, norm1_g, norm2_g, ada_w, ada_b, w_in, conv_w, conv_b, conv_ln_g, conv_ln_b, gmlp_norm_g, gmlp_ws, gmlp_bs, hgrn_lb_logits, hgrn_norm_g, s5_a_re, s5_a_im, s5_log_dt, s5_b_re, s5_b_im, s5_c_re, s5_c_im, s5_d, s5_glu_w, s5_glu_b, w_out, mlp_w1, mlp_w2, final_norm_g):
    n_ctx, seq_ctx, _ = x_prompt.shape
    n_lat, seq_lat, _ = x_sample.shape
    assert n_lat + 1 <= COND_ROWS
    assert seq_ctx % IN_TILE == 0 or IN_TILE % seq_ctx == 0

    lb_soft = jax.nn.softmax(hgrn_lb_logits.astype(F32), axis=1)
    hgrn_lb = jnp.cumsum(lb_soft, axis=1) - lb_soft[:, :1]
    P = _prepare(dict(
        norm1_g=norm1_g, norm2_g=norm2_g, w_in=w_in, w_out=w_out, mlp_w1=mlp_w1, mlp_w2=mlp_w2,
        conv_w=conv_w, conv_b=conv_b, conv_ln_g=conv_ln_g, conv_ln_b=conv_ln_b,
        gmlp_norm_g=gmlp_norm_g, gmlp_ws=gmlp_ws, gmlp_bs=gmlp_bs, hgrn_norm_g=hgrn_norm_g,
        s5_a_re=s5_a_re, s5_a_im=s5_a_im, s5_log_dt=s5_log_dt, s5_b_re=s5_b_re, s5_b_im=s5_b_im,
        s5_c_re=s5_c_re, s5_c_im=s5_c_im, s5_d=s5_d, s5_glu_w=s5_glu_w, s5_glu_b=s5_glu_b,
        final_norm_g=final_norm_g), hgrn_lb)
    cond =jnp.zeros((COND_ROWS, D_MODEL), F32).at[0].set(c_ctx.astype(F32)).at[1:1 + n_lat].set(c.astype(F32))
    ada = _ada_call(cond, ada_w.astype(F32), ada_b.astype(F32))
    P['ada4'] = ada.reshape(DEPTH, COND_ROWS, 1, ADA_CHUNKS * D_MODEL)

    zeros_s5 = jnp.zeros((n_ctx, 2, 1, S5_STATE), F32)
    y_ctx, hg_ctx, re_ctx, im_ctx = _run_stream(
        x_prompt.astype(F32).reshape(n_ctx * seq_ctx, D_MODEL), n_ctx, seq_ctx, lambda r: 0,
        [None] * DEPTH, [zeros_s5] * DEPTH, [zeros_s5] * DEPTH, P)

    xs = x_sample.astype(F32) + _grid_pos_embed(seq_lat)[None]
    hg0 = [_hgrn_state_to_kernel(state_hgrn[:, l]) for l in range(DEPTH)]
    re0 = [state_s5_re[:, l].astype(F32).reshape(n_lat, 2, 1, S5_STATE) for l in range(DEPTH)]
    im0 = [state_s5_im[:, l].astype(F32).reshape(n_lat, 2, 1, S5_STATE) for l in range(DEPTH)]
    y_lat, _, _, _ = _run_stream(xs.reshape(n_lat * seq_lat, D_MODEL), n_lat, seq_lat,
                                 lambda r: 1 + r // seq_lat, hg0, re0, im0, P)

    dt = x_prompt.dtype
    new_hg = jnp.stack([_hgrn_state_from_kernel(s) for s in hg_ctx], axis=1).astype(dt)
    new_re = jnp.stack([s.reshape(n_ctx, 2, S5_GROUPS, S5_P) for s in re_ctx], axis=1).astype(dt)
    new_im = jnp.stack([s.reshape(n_ctx, 2, S5_GROUPS, S5_P) for s in im_ctx], axis=1).astype(dt)
    return (y_ctx.reshape(n_ctx, seq_ctx, D_MODEL).astype(dt),
            y_lat.reshape(n_lat, seq_lat, D_MODEL).astype(x_sample.dtype), new_hg, new_re, new_im)
```

```python
import functools
import math

import numpy as np
import jax
import jax.numpy as jnp
from jax import lax
from jax.experimental import pallas as pl
from jax.experimental.pallas import tpu as pltpu

D_MODEL = 1024
DEPTH = 4
GRID_W = 64
GROUP_W = 256
N_IN_BLOCKS = 10
D_FF = 4 * D_MODEL
EPS = 1e-6
F_MIN = 1e-30
ADA_CHUNKS = 6
CONV_K = 31
CONV_PAD = CONV_K // 2
GM_CHUNK = 128
GM_HEADS = 4
HG_HEADS = 4
HG_DK = 64
HG_CHUNK = 64
HG_TILE_LEVELS = 3
S5_IN = 16
S5_GROUPS = 16
S5_P = 64
S5_STATE = S5_GROUPS * S5_P
POS_BASE = 10000.0

SUBLANE = 8
LANE = 128
COND_ROWS = 8
VMEM_LIMIT = 56 * 1024 * 1024

F32 = jnp.float32
BF16 = jnp.bfloat16


def _sigmoid(x):
    return 1.0 / (1.0 + jnp.exp(-x))


def _silu(x):
    return x * _sigmoid(x)


def _params(n_parallel=1):
    return pltpu.CompilerParams(dimension_semantics=("arbitrary",) * n_parallel,
                                vmem_limit_bytes=VMEM_LIMIT)


def _split2(x):
    hi = x.astype(BF16)
    lo = (x - hi.astype(F32)).astype(BF16)
    return hi, lo


def _dot(a, b):
    return jnp.dot(a, b, preferred_element_type=F32)


def _dot_nt(a, b):
    return lax.dot_general(a, b, (((1,), (1,)), ((), ())), preferred_element_type=F32)


def _dot_tn(a, b):
    return lax.dot_general(a, b, (((0,), (0,)), ((), ())), preferred_element_type=F32)


ADA_BLOCK = 1536


def _ada_kernel(cond_ref, w_ref, b_ref, o_ref):
    s = _silu(cond_ref[...])
    o_ref[...] = _dot(s.astype(BF16), w_ref[...].astype(BF16)) + b_ref[...]


def _ada_call(cond, ada_w, ada_b):
    n_out = ADA_CHUNKS * D_MODEL
    return pl.pallas_call(
        _ada_kernel,
        grid=(DEPTH, n_out // ADA_BLOCK),
        in_specs=[pl.BlockSpec((COND_ROWS, D_MODEL), lambda l, j: (0, 0)),
                  pl.BlockSpec((None, D_MODEL, ADA_BLOCK), lambda l, j: (l, 0, j)),
                  pl.BlockSpec((None, 1, ADA_BLOCK), lambda l, j: (l, 0, j))],
        out_specs=pl.BlockSpec((None, COND_ROWS, ADA_BLOCK), lambda l, j: (l, 0, j)),
        out_shape=jax.ShapeDtypeStruct((DEPTH, COND_ROWS, n_out), F32),
        compiler_params=_params(2),
        name="ada",
    )(cond, ada_w, ada_b.reshape(DEPTH, 1, n_out))


IN_TILE = 512
Z_MAIN_W = (N_IN_BLOCKS - 1) * GROUP_W


def _inproj_kernel(x_ref, ada_ref, g_ref, w_ref, z_ref, zs_ref, zsb_ref, half_ref):
    x = x_ref[...]
    ada = ada_ref[...]
    sh1 = ada[:, 0:D_MODEL]
    sc1 = ada[:, D_MODEL:2 * D_MODEL]
    ms = jnp.mean(x * x, axis=-1, keepdims=True)
    h = x * lax.rsqrt(ms + EPS) * (g_ref[...] * (1.0 + sc1)) + sh1
    z = _dot(h.astype(BF16), w_ref[...])
    z_ref[...] = z[:, 0:Z_MAIN_W]
    zs_ref[...] = z[:, Z_MAIN_W:]
    for h in range(GROUP_W // LANE):
        half_ref[h] = z[:, Z_MAIN_W + h * LANE:Z_MAIN_W + (h + 1) * LANE]
    for j in range(S5_BLOCK):
        for h in range(GROUP_W // LANE):
            c0 = _s5_col(h, j)
            zsb_ref[:, c0:c0 + LANE] = (
                half_ref[h, pl.ds(j, IN_TILE // S5_BLOCK, stride=S5_BLOCK), :].astype(BF16))


def _inproj_call(x, ada4, g1, w_in, layer, cond_row):
    n_tok = x.shape[0]
    zw = N_IN_BLOCKS * GROUP_W
    tile = lambda w: pl.BlockSpec((IN_TILE, w), lambda i: (i, 0))
    return pl.pallas_call(
        _inproj_kernel,
        grid=(n_tok // IN_TILE,),
        in_specs=[pl.BlockSpec((IN_TILE, D_MODEL), lambda i: (i, 0)),
                  pl.BlockSpec((None, None, 1, ADA_CHUNKS * D_MODEL),
                               lambda i: (layer, cond_row(i * IN_TILE), 0, 0)),
                  pl.BlockSpec((None, 1, D_MODEL), lambda i: (layer, 0, 0)),
                  pl.BlockSpec((None, D_MODEL, zw), lambda i: (layer, 0, 0))],
        out_specs=[tile(Z_MAIN_W), tile(GROUP_W),
                   pl.BlockSpec((IN_TILE // S5_BLOCK, S5_ROW_W), lambda i: (i, 0))],
        out_shape=[jax.ShapeDtypeStruct((n_tok, Z_MAIN_W), F32),
                   jax.ShapeDtypeStruct((n_tok, GROUP_W), F32),
                   jax.ShapeDtypeStruct((n_tok // S5_BLOCK, S5_ROW_W), BF16)],
        scratch_shapes=[pltpu.VMEM((GROUP_W // LANE, IN_TILE, LANE), F32)],
        compiler_params=_params(1),
        name="inproj",
    )(x, ada4, g1, w_in)


CONV_TILE = 64
CONV_HALO = 16
CONV_FILL = 128
CONV_NORM_TILE = 256


def _conv_kernel(z_ref, w_ref, b_ref, lng_ref, lnb_ref, o_ref, pad_ref, *, seq):
    tail = seq + 2 * CONV_HALO - (seq + SUBLANE)
    for r in range(SUBLANE):
        pad_ref[r, 0:CONV_HALO, :] = jnp.zeros((CONV_HALO, GROUP_W), F32)
        pad_ref[r, seq + SUBLANE:seq + 2 * CONV_HALO, :] = jnp.zeros((tail, GROUP_W), F32)
    for t0 in range(0, seq, CONV_FILL):
        u = z_ref[t0:t0 + CONV_FILL, 0:GROUP_W] * _sigmoid(z_ref[t0:t0 + CONV_FILL, GROUP_W:2 * GROUP_W])
        for r in range(SUBLANE):
            pad_ref[r, CONV_HALO - r + t0:CONV_HALO - r + t0 + CONV_FILL, :] = u
    first = CONV_HALO - CONV_PAD

    def tile(i, carry):
        r0 = pl.multiple_of(i * CONV_TILE, CONV_TILE)
        acc = jnp.zeros((CONV_TILE, GROUP_W), F32)
        for r in range(SUBLANE):
            taps = [k for k in range(CONV_K) if (first + k) % SUBLANE == r]
            lo = first + taps[0] - r
            hi = first + taps[-1] - r
            win = pad_ref[r, pl.ds(r0 + lo, CONV_TILE + hi - lo), :]
            for k in taps:
                off = first + k - r - lo
                wk = jnp.concatenate([w_ref[k]] * (CONV_TILE // SUBLANE), axis=0)
                acc = acc + wk * win[off:off + CONV_TILE, :]
        o_ref[pl.ds(r0, CONV_TILE), :] = acc + b_ref[...]
        return carry

    lax.fori_loop(0, seq // CONV_TILE, tile, 0)

    def norm(i, carry):
        rows = pl.ds(pl.multiple_of(i * CONV_NORM_TILE, CONV_NORM_TILE), CONV_NORM_TILE)
        c = o_ref[rows, :]
        mu = jnp.mean(c, axis=-1, keepdims=True)
        cc = c - mu
        var = jnp.mean(cc * cc, axis=-1, keepdims=True)
        y = cc * lax.rsqrt(var + EPS) * lng_ref[...] + lnb_ref[...]
        o_ref[rows, :] = _silu(y)
        return carry

    lax.fori_loop(0, seq // CONV_NORM_TILE, norm, 0)


def _conv_call(z, conv_w, conv_b, ln_g, ln_b, layer, n_seq, seq):
    vec = pl.BlockSpec((None, 1, GROUP_W), lambda s: (layer, 0, 0))
    return pl.pallas_call(
        functools.partial(_conv_kernel, seq=seq),
        grid=(n_seq,),
        in_specs=[pl.BlockSpec((seq, 2 * GROUP_W), lambda s: (s, 0)),
                  pl.BlockSpec((None, CONV_K, SUBLANE, GROUP_W), lambda s: (layer, 0, 0, 0)),
                  vec, vec, vec],
        out_specs=pl.BlockSpec((seq, GROUP_W), lambda s: (s, 0)),
        out_shape=jax.ShapeDtypeStruct((n_seq * seq, GROUP_W), F32),
        scratch_shapes=[pltpu.VMEM((SUBLANE, seq + 2 * CONV_HALO, GROUP_W), F32)],
        compiler_params=_params(1),
        name="conv",
    )(z, conv_w, conv_b, ln_g, ln_b)


def _head_mask(rows_per_head, cols_per_head, n_rows, n_cols):
    r = lax.broadcasted_iota(jnp.int32, (n_rows, n_cols), 0) // rows_per_head
    c = lax.broadcasted_iota(jnp.int32, (n_rows, n_cols), 1) // cols_per_head
    return r == c


GM_TILE = 1024


def _gmlp_kernel(z_ref, g_ref, ws_ref, bs_ref, o_ref):
    hd = GROUP_W // GM_HEADS
    mask = _head_mask(GM_CHUNK, hd, GM_HEADS * GM_CHUNK, GROUP_W)
    for r0 in range(0, GM_TILE, GM_CHUNK):
        u = z_ref[r0:r0 + GM_CHUNK, 0:GROUP_W]
        v = z_ref[r0:r0 + GM_CHUNK, GROUP_W:2 * GROUP_W]
        vn = v * lax.rsqrt(jnp.mean(v * v, axis=-1, keepdims=True) + EPS) * g_ref[...]
        stack = jnp.where(mask, jnp.concatenate([vn] * GM_HEADS, axis=0), 0.0).astype(BF16)
        sv = _dot(ws_ref[...], stack) + bs_ref[...]
        o_ref[r0:r0 + GM_CHUNK, :] = u * sv


def _gmlp_call(z, norm_g, ws_cat, bs_full, layer):
    n_tok = z.shape[0]
    return pl.pallas_call(
        _gmlp_kernel,
        grid=(n_tok // GM_TILE,),
        in_specs=[pl.BlockSpec((GM_TILE, 2 * GROUP_W), lambda i: (i, 1)),
                  pl.BlockSpec((None, 1, GROUP_W), lambda i: (layer, 0, 0)),
                  pl.BlockSpec((None, GM_CHUNK, GM_HEADS * GM_CHUNK), lambda i: (layer, 0, 0)),
                  pl.BlockSpec((None, GM_CHUNK, GROUP_W), lambda i: (layer, 0, 0))],
        out_specs=pl.BlockSpec((GM_TILE, GROUP_W), lambda i: (i, 0)),
        out_shape=jax.ShapeDtypeStruct((n_tok, GROUP_W), F32),
        compiler_params=_params(1),
        name="gmlp",
    )(z, norm_g, ws_cat, bs_full)


def _hgrn_block_masks():
    t = np.arange(HG_CHUNK)
    blk = np.zeros((HG_TILE_LEVELS, HG_CHUNK, HG_CHUNK), np.float32)
    for lvl in range(HG_TILE_LEVELS):
        b = t >> (lvl + 1)
        blk[lvl] = (b[:, None] == b[None, :])
    return np.tile(blk, (1, 1, HG_HEADS))


_HG_BMASK = _hgrn_block_masks()
HG_NORM_TILE = 256


def _hgrn_kernel(zq_ref, zi_ref, zg_ref, zff_ref, zfb_ref, lb_ref, ng_ref, bmask_ref, *rest,
                 seq, n_seqs, has_state):
    s0_ref = rest[0] if has_state else None
    o_ref, s_out_ref, ob_ref, st_ref = rest[-4:]
    n = HG_CHUNK
    n_chunks = seq // n
    hmask = _head_mask(HG_DK, HG_DK, GROUP_W, GROUP_W)
    hmask_bf = jnp.where(hmask, 1.0, 0.0).astype(BF16)
    row = lax.broadcasted_iota(jnp.int32, (n, GROUP_W), 0)
    n_tiles = n // SUBLANE
    col_tile = (lax.broadcasted_iota(jnp.int32, (SUBLANE, GROUP_W), 1) % n) // SUBLANE
    sub3 = lax.broadcasted_iota(jnp.int32, (n_tiles, SUBLANE, GROUP_W), 1)

    def block_diag(a):
        return jnp.where(hmask, jnp.concatenate([a] * HG_HEADS, axis=0), 0.0).astype(BF16)

    def spread_rows(a, first, period):
        return jnp.concatenate([jnp.broadcast_to(a[r:r + 1, :], (period, GROUP_W))
                                for r in range(first, n, period)], axis=0)

    def boundary(cum, lvl, direction):
        m = 1 << lvl
        at = m - 1 if direction == 0 else m
        if 2 * m >= SUBLANE:
            return spread_rows(cum, at, 2 * m)
        lo = spread_rows(cum, at, SUBLANE)
        hi = spread_rows(cum, at + 2 * m, SUBLANE)
        return jnp.where((row & (2 * m)) == 0, lo, hi)

    def chunk_step(s, direction, c, zf_ref, dst_ref):
        lb = lb_ref[direction:direction + 1, :]
        end_row = n - 1 if direction == 0 else 0
        pos = row if direction == 0 else n - 1 - row
        rows = pl.ds(pl.multiple_of(s * seq + c * n, n), n)
        q = _silu(zq_ref[rows, :])
        v = zi_ref[rows, :]
        f = lb + (1.0 - lb) * _sigmoid(zf_ref[rows, :])
        k = 1.0 - f
        f = jnp.maximum(f, F_MIN)
        cum3 = jnp.log2(f).reshape(n_tiles, SUBLANE, GROUP_W)
        for d in (1, 2, 4):
            shifted = pltpu.roll(cum3, d if direction == 0 else SUBLANE - d, 1)
            live = sub3 >= d if direction == 0 else sub3 < SUBLANE - d
            cum3 = cum3 + jnp.where(live, shifted, 0.0)
        edge = SUBLANE - 1 if direction == 0 else 0
        order = range(n_tiles) if direction == 0 else range(n_tiles - 1, -1, -1)
        tiles, run = [None] * n_tiles, None
        for t in order:
            tiles[t] = cum3[t] if run is None else cum3[t] + run
            run = jnp.broadcast_to(tiles[t][edge:edge + 1, :], (SUBLANE, GROUP_W))
        cum = jnp.concatenate(tiles, axis=0)
        total = cum[end_row:end_row + 1]
        w_cum = jnp.exp2(cum)
        w_rem = jnp.exp2(total - cum)
        w_end = jnp.exp2(total)
        st = st_ref[s, direction]
        inter = _dot_nt((q * w_cum).astype(BF16), st.astype(BF16))

        tile_end = spread_rows(cum, edge, SUBLANE)
        k_far = k * jnp.exp2(tile_end - cum)
        pairs = [(j, i) for j in range(n_tiles)
                 for i in (range(j + 1, n_tiles) if direction == 0 else range(j))]
        tile = lambda a, i: a[i * SUBLANE:(i + 1) * SUBLANE, :]
        q_far = jnp.concatenate([tile(q, i) * jnp.exp2(tile(cum, i) - tile(tile_end, j)) for j, i in pairs], axis=0)
        far = _dot_nt(q_far.astype(BF16), block_diag(k_far))
        acc = [jnp.zeros((SUBLANE, GROUP_W), F32)] * n_tiles
        for p, (j, i) in enumerate(pairs):
            acc[i] = acc[i] + jnp.where(col_tile == j, tile(far, p), 0.0)
        scores = jnp.concatenate(acc, axis=0)

        for lvl in range(HG_TILE_LEVELS):
            upper = (pos & (1 << lvl)) != 0
            if lvl == 0:
                wl = jnp.where(upper, f, 1.0)
            else:
                ref = boundary(cum, lvl, direction)
                wl = jnp.exp2(jnp.where(upper, cum - ref, ref - cum))
            ql = jnp.where(upper, q * wl, 0.0).astype(BF16)
            kl = jnp.where(upper, 0.0, k * wl)
            scores = scores + bmask_ref[lvl] * _dot_nt(ql, block_diag(kl))
        diag = _dot((q * k).astype(BF16), hmask_bf)
        intra = _dot(scores.astype(BF16), block_diag(v)) + diag * v
        upd = _dot_tn(v.astype(BF16), (k * w_rem).astype(BF16))
        st_ref[s, direction] = st * w_end + jnp.where(hmask, upd, 0.0)
        dst_ref[rows, :] = inter + intra

    st_ref[...] = s0_ref[...] if has_state else jnp.zeros(st_ref.shape, F32)

    def both(i, carry):
        for s in range(n_seqs):
            chunk_step(s, 0, i, zff_ref, o_ref)
            chunk_step(s, 1, n_chunks - 1 - i, zfb_ref, ob_ref)
        return carry

    lax.fori_loop(0, n_chunks, both, 0)

    pick = (lax.broadcasted_iota(jnp.int32, (GROUP_W, HG_DK), 0) % HG_DK
            == lax.broadcasted_iota(jnp.int32, (GROUP_W, HG_DK), 1))
    pick_bf = jnp.where(pick, 1.0, 0.0).astype(BF16)
    for s in range(n_seqs):
        for direction in range(2):
            hi, lo = _split2(st_ref[s, direction])
            s_out_ref[s, direction] = _dot_tn(hi, pick_bf) + _dot_tn(lo, pick_bf)

    def finish(i, carry):
        rows = pl.ds(pl.multiple_of(i * HG_NORM_TILE, HG_NORM_TILE), HG_NORM_TILE)
        o = o_ref[rows, :] + ob_ref[rows, :]
        hi, mid = _split2(o * o)
        ms = (_dot(hi, hmask_bf) + _dot(mid, hmask_bf)) * (1.0 / HG_DK)
        y = o * lax.rsqrt(ms + EPS) * ng_ref[...]
        o_ref[rows, :] = y * _silu(zg_ref[rows, :])
        return carry

    lax.fori_loop(0, n_seqs * seq // HG_NORM_TILE, finish, 0)


HG_ROWS_PER_STEP = 1024
HG_MIN_SEQS_PER_STEP = 2


def _hgrn_call(z, lb, norm_g, s0t, states, layer, n_seq, seq):
    per_step = min(n_seq, max(HG_MIN_SEQS_PER_STEP, HG_ROWS_PER_STEP // seq))
    assert n_seq % per_step == 0
    rows = per_step * seq
    mode = pl.Buffered(1) if rows > HG_ROWS_PER_STEP else None

    def zcol(j):
        return pl.BlockSpec((rows, GROUP_W), lambda s: (s, j), pipeline_mode=mode)

    full = lambda shape: pl.BlockSpec(shape, lambda s: (0,) * len(shape))
    has_state = s0t is not None
    state_specs = [pl.BlockSpec((per_step, 2, GROUP_W, GROUP_W), lambda s: (s, 0, 0, 0))] if has_state else []
    state_args = (s0t,) if has_state else ()
    in_specs = [zcol(4), zcol(5), zcol(6), zcol(7), zcol(8),
                pl.BlockSpec((None, 2, GROUP_W), lambda s: (layer, 0, 0)),
                pl.BlockSpec((None, 1, GROUP_W), lambda s: (layer, 0, 0)),
                full((HG_TILE_LEVELS, HG_CHUNK, GROUP_W))] + state_specs
    aliases = {}
    if states is not None:
        aliases = {len(in_specs): 1}
        in_specs = in_specs + [pl.BlockSpec(memory_space=pl.ANY)]
        state_args = state_args + (states,)
    return pl.pallas_call(
        functools.partial(_hgrn_kernel, seq=seq, n_seqs=per_step, has_state=has_state),
        grid=(n_seq // per_step,),
        in_specs=in_specs,
        out_specs=[pl.BlockSpec((rows, GROUP_W), lambda s: (s, 0)),
                   pl.BlockSpec((per_step, None, 2, GROUP_W, HG_DK), lambda s: (s, layer, 0, 0, 0))],
        out_shape=[jax.ShapeDtypeStruct((n_seq * seq, GROUP_W), F32),
                   jax.ShapeDtypeStruct((n_seq, DEPTH, 2, GROUP_W, HG_DK), F32)],
        scratch_shapes=[pltpu.VMEM((rows, GROUP_W), F32), pltpu.VMEM((per_step, 2, GROUP_W, GROUP_W), F32)],
        input_output_aliases=aliases,
        compiler_params=_params(1),
        name="hgrn",
    )(z, z, z, z, z, lb, norm_g, jnp.asarray(_HG_BMASK), *state_args)


def _hgrn_state_to_kernel(s):
    st = jnp.swapaxes(s.astype(F32), -1, -2)
    eye = jnp.eye(HG_HEADS, dtype=F32)
    full = jnp.einsum('ndhvk,hg->ndhvgk', st, eye)
    return full.reshape(s.shape[0], 2, GROUP_W, GROUP_W)


def _hgrn_state_from_kernel(st):
    return st.reshape(st.shape[:3] + (HG_HEADS, HG_DK, HG_DK))


S5_BLOCK = 8
S5_ROW_W = S5_BLOCK * GROUP_W
S5_HALVES = GROUP_W // LANE
S5_HALF_W = S5_BLOCK * LANE
S5_GROUPS_PER_HALF = S5_GROUPS // S5_HALVES
S5_PARTS = 4
S5_XW = S5_PARTS * S5_STATE
S5_PART_HALF = S5_STATE // S5_HALVES
S5_STEPS = (1, 2, 4)
S5_TABLES = 2 * (len(S5_STEPS) + 1)


def _s5_col(half, token):
    return half * S5_HALF_W + token * LANE


def _s5_in_kernel(u_ref, w_ref, x_ref):
    x_ref[...] = _dot(u_ref[...], w_ref[...])


def _s5_in_call(ub, wx, layer):
    n_blk = ub.shape[0]
    return pl.pallas_call(
        _s5_in_kernel,
        grid=(S5_HALVES, S5_PARTS),
        in_specs=[pl.BlockSpec((n_blk, S5_HALF_W), lambda h, p: (0, h)),
                  pl.BlockSpec((None, None, S5_HALF_W, S5_PART_HALF), lambda h, p: (layer, h, 0, p))],
        out_specs=pl.BlockSpec((n_blk, S5_PART_HALF), lambda h, p: (0, p * S5_HALVES + h)),
        out_shape=jax.ShapeDtypeStruct((n_blk, S5_XW), F32),
        compiler_params=_params(2),
        name="s5_in",
    )(ub, wx)


def _s5_scan_kernel(x_ref, tab_ref, s0re_ref, s0im_ref, h_ref, hre_ref, him_ref, *, n_blk):
    n_pairs = n_blk // (2 * SUBLANE)
    row = lax.broadcasted_iota(jnp.int32, (SUBLANE, S5_STATE), 0)
    n_steps = len(S5_STEPS)

    for direction in range(2):
        c_re = 2 * direction * S5_STATE
        c_im = c_re + S5_STATE
        edge = 0 if direction == 0 else SUBLANE - 1
        last = SUBLANE - 1 - edge
        unit = 0 if direction == 0 else SUBLANE - 1

        def tile(rows, carry):
            hc_r, hc_i, px_r, px_i = carry
            xr = x_ref[rows, c_re:c_re + S5_STATE]
            xi = x_ref[rows, c_im:c_im + S5_STATE]
            shift = 1 if direction == 0 else SUBLANE - 1
            hr = jnp.where(row == edge, px_r, pltpu.roll(xr, shift, 0))
            hi = jnp.where(row == edge, px_i, pltpu.roll(xi, shift, 0))
            for s, step in enumerate(S5_STEPS):
                sh = step if direction == 0 else SUBLANE - step
                rr = pltpu.roll(hr, sh, 0)
                ri = pltpu.roll(hi, sh, 0)
                ar = tab_ref[direction, 2 * s]
                ai = tab_ref[direction, 2 * s + 1]
                hr, hi = hr + ar * rr - ai * ri, hi + ar * ri + ai * rr
            pr = tab_ref[direction, 2 * n_steps]
            pi = tab_ref[direction, 2 * n_steps + 1]
            hr, hi = hr + pr * hc_r - pi * hc_i, hi + pr * hc_i + pi * hc_r
            bc = lambda a: jnp.broadcast_to(a[last:last + 1, :], (SUBLANE, S5_STATE))
            return hr, hi, (bc(hr), bc(hi), bc(xr), bc(xi))

        def pair(j, carry):
            jj = j if direction == 0 else n_pairs - 1 - j
            base = pl.multiple_of(jj * 2 * SUBLANE, 2 * SUBLANE)
            offs = (0, SUBLANE) if direction == 0 else (SUBLANE, 0)
            out = {}
            for off in offs:
                hr, hi, carry = tile(pl.ds(base + off, SUBLANE), carry)
                out[off] = (hr, hi)
            both = pl.ds(base, 2 * SUBLANE)
            h_ref[both, c_re:c_re + S5_STATE] = jnp.concatenate([out[0][0], out[SUBLANE][0]], 0).astype(BF16)
            h_ref[both, c_im:c_im + S5_STATE] = jnp.concatenate([out[0][1], out[SUBLANE][1]], 0).astype(BF16)
            return carry

        zero = jnp.zeros((SUBLANE, S5_STATE), F32)
        init = (zero, zero,
                jnp.broadcast_to(s0re_ref[direction], (SUBLANE, S5_STATE)),
                jnp.broadcast_to(s0im_ref[direction], (SUBLANE, S5_STATE)))
        hc_r, hc_i, px_r, px_i = lax.fori_loop(0, n_pairs, pair, init)
        a_r = tab_ref[direction, 2 * n_steps][unit:unit + 1, :]
        a_i = tab_ref[direction, 2 * n_steps + 1][unit:unit + 1, :]
        hre_ref[direction] = a_r * hc_r[0:1, :] - a_i * hc_i[0:1, :] + px_r[0:1, :]
        him_ref[direction] = a_r * hc_i[0:1, :] + a_i * hc_r[0:1, :] + px_i[0:1, :]


def _s5_scan_call(x, tab, s0re, s0im, layer, n_seq, n_blk):
    state = pl.BlockSpec((None, 2, 1, S5_STATE), lambda s: (s, 0, 0, 0))
    return pl.pallas_call(
        functools.partial(_s5_scan_kernel, n_blk=n_blk),
        grid=(n_seq,),
        in_specs=[pl.BlockSpec((n_blk, S5_XW), lambda s: (s, 0)),
                  pl.BlockSpec((None, 2, S5_TABLES, SUBLANE, S5_STATE), lambda s: (layer, 0, 0, 0, 0)),
                  state, state],
        out_specs=[pl.BlockSpec((n_blk, S5_XW), lambda s: (s, 0)), state, state],
        out_shape=[jax.ShapeDtypeStruct((n_seq * n_blk, S5_XW), BF16),
                   jax.ShapeDtypeStruct((n_seq, 2, 1, S5_STATE), F32),
                   jax.ShapeDtypeStruct((n_seq, 2, 1, S5_STATE), F32)],
        compiler_params=_params(1),
        name="s5_scan",
    )(x, tab, s0re, s0im)


def _s5_out_kernel(u_ref, *refs):
    h_refs, (wt_ref, wc_ref, y_ref) = refs[:S5_PARTS], refs[S5_PARTS:]
    y = _dot(u_ref[...], wt_ref[...])
    for p in range(S5_PARTS):
        y = y + _dot(h_refs[p][...], wc_ref[p * S5_PART_HALF:(p + 1) * S5_PART_HALF, :])
    y_ref[...] = y


def _s5_out_call(ub, h, wt, wc, layer):
    n_blk = ub.shape[0]

    def state_part(p):
        return pl.BlockSpec((n_blk, S5_PART_HALF), lambda j: (0, p * S5_HALVES + j))

    return pl.pallas_call(
        _s5_out_kernel,
        grid=(S5_HALVES,),
        in_specs=[pl.BlockSpec((n_blk, S5_HALF_W), lambda j: (0, j))]
                 + [state_part(p) for p in range(S5_PARTS)]
                 + [pl.BlockSpec((None, None, S5_HALF_W, S5_HALF_W), lambda j: (layer, j, 0, 0)),
                    pl.BlockSpec((None, None, S5_PARTS * S5_PART_HALF, S5_HALF_W), lambda j: (layer, j, 0, 0))],
        out_specs=pl.BlockSpec((n_blk, S5_HALF_W), lambda j: (0, j)),
        out_shape=jax.ShapeDtypeStruct((n_blk, S5_ROW_W), F32),
        compiler_params=_params(1),
        name="s5_out",
    )(ub, *([h] * S5_PARTS), wt, wc)


def _s5_mix(ub, consts, s0re, s0im, layer, n_seq, seq):
    wx, wt, wc, tab = consts
    n_blk = seq // S5_BLOCK
    x = _s5_in_call(ub, wx, layer)
    h, h_re, h_im = _s5_scan_call(x, tab, s0re, s0im, layer, n_seq, n_blk)
    y = _s5_out_call(ub, h, wt, wc, layer)
    return y, h_re, h_im


S5_EXPAND_ROWS = 512


def _expand_kernel(c_ref, t_ref, o_ref, keep_ref, *, row_group, col_group):
    rows, cols = o_ref.shape

    @pl.when((pl.program_id(0) == 0) & (pl.program_id(1) == 0))
    def _():
        r = lax.broadcasted_iota(jnp.int32, (rows, cols), 0)
        col = lax.broadcasted_iota(jnp.int32, (rows, cols), 1)
        group = lambda i, period_width: (i % period_width[0]) // period_width[1]
        keep_ref[...] = jnp.where(group(r, row_group) == group(col, col_group), 1.0, 0.0)

    o_ref[...] = (_dot(c_ref[...], t_ref[...]) * keep_ref[...]).astype(BF16)


def _expand_call(compact, spread, row_group, col_group):
    n_ops, n_rows, k = compact.shape
    n_cols = spread.shape[1]
    assert S5_EXPAND_ROWS % row_group[0] == 0
    return pl.pallas_call(
        functools.partial(_expand_kernel, row_group=row_group, col_group=col_group),
        grid=(n_ops, n_rows // S5_EXPAND_ROWS),
        in_specs=[pl.BlockSpec((None, S5_EXPAND_ROWS, k), lambda l, i: (l, i, 0)),
                  pl.BlockSpec((k, n_cols), lambda l, i: (0, 0))],
        out_specs=pl.BlockSpec((None, S5_EXPAND_ROWS, n_cols), lambda l, i: (l, i, 0)),
        out_shape=jax.ShapeDtypeStruct((n_ops, n_rows, n_cols), BF16),
        scratch_shapes=[pltpu.VMEM((S5_EXPAND_ROWS, n_cols), F32)],
        compiler_params=_params(2),
        name="s5_expand",
    )(compact, spread)


def _s5_constants(a_re, a_im, log_dt, b_re, b_im, c_re, c_im):
    n = S5_BLOCK
    a = lax.complex(a_re.astype(F32), a_im.astype(F32))
    dt = jnp.exp(log_dt.astype(F32))[..., None]
    a_bar = jnp.exp(a * dt)
    b_bar = ((a_bar - 1.0) / a)[..., None] * lax.complex(b_re.astype(F32), b_im.astype(F32))
    c = lax.complex(c_re.astype(F32), c_im.astype(F32))
    pw = [jnp.ones_like(a_bar)]
    for _ in range(n):
        pw.append(pw[-1] * a_bar)
    pw = jnp.stack(pw, axis=2)
    gh = S5_GROUPS_PER_HALF
    kron = lambda *m: functools.reduce(np.kron, m)
    spread_state = jnp.asarray(kron(np.eye(S5_PARTS), np.ones((1, gh)), np.eye(S5_P)), BF16)
    spread_token = jnp.asarray(kron(np.eye(n), np.ones((1, gh)), np.eye(S5_IN)), BF16)
    token_group = (LANE, S5_IN)
    state_group = (S5_PART_HALF, S5_P)
    halves = lambda a: a.reshape(a.shape[:3] + (S5_HALVES, gh) + a.shape[4:])

    w_f = pw[:, 0, ::-1][:, 1:, :, :, None] * b_bar[:, 0, None]
    w_b = pw[:, 1, :n, :, :, None] * b_bar[:, 1, None]
    w4 = jnp.stack([jnp.real(w_f), jnp.imag(w_f), jnp.real(w_b), jnp.imag(w_b)], axis=1)
    wx_c = jnp.transpose(halves(w4), (0, 3, 2, 4, 6, 1, 5))
    wx_c = wx_c.reshape(DEPTH * S5_HALVES, S5_HALF_W, S5_PARTS * S5_P)
    wx = _expand_call(wx_c.astype(BF16), spread_state, token_group, state_group)
    wx = wx.reshape(DEPTH, S5_HALVES, S5_HALF_W, S5_PARTS * S5_PART_HALF)

    o_f = c[:, 0, None] * pw[:, 0, 1:, :, None, :]
    o_b = c[:, 1, None] * pw[:, 1, ::-1][:, :n, :, None, :]
    o4 = jnp.stack([jnp.real(o_f), -jnp.imag(o_f), jnp.real(o_b), -jnp.imag(o_b)], axis=1)
    wc_c = jnp.transpose(halves(o4), (0, 3, 1, 4, 6, 2, 5))
    wc_c = wc_c.reshape(DEPTH * S5_HALVES, S5_PARTS * S5_PART_HALF, n * S5_IN)
    wc = _expand_call(wc_c.astype(BF16), spread_token, state_group, token_group)
    wc = wc.reshape(DEPTH, S5_HALVES, S5_PARTS * S5_PART_HALF, S5_HALF_W)

    k_f = jnp.real(jnp.einsum('lgcp,ltgp,lgpd->lgdtc', c[:, 0], pw[:, 0, :n], b_bar[:, 0]))
    k_b = jnp.real(jnp.einsum('lgcp,ltgp,lgpd->lgdtc', c[:, 1], pw[:, 1, :n], b_bar[:, 1]))
    tt, jj, ii = np.meshgrid(np.arange(n), np.arange(n), np.arange(n), indexing='ij')
    place = lambda hit: jnp.asarray(np.kron(hit.reshape(n, n * n).astype(np.float32), np.eye(S5_IN)), F32)
    flat = lambda k: k.reshape(DEPTH * S5_GROUPS * S5_IN, n * S5_IN)
    t_all = (jnp.dot(flat(k_f), place(ii - jj == tt), precision=lax.Precision.HIGHEST)
             + jnp.dot(flat(k_b), place(jj - ii == tt), precision=lax.Precision.HIGHEST))
    wt_c = t_all.reshape(DEPTH, S5_HALVES, gh, S5_IN, n, n * S5_IN)
    wt_c = jnp.transpose(wt_c, (0, 1, 4, 2, 3, 5)).reshape(DEPTH * S5_HALVES, S5_HALF_W, n * S5_IN)
    wt = _expand_call(wt_c.astype(BF16), spread_token, token_group, token_group)
    wt = wt.reshape(DEPTH, S5_HALVES, S5_HALF_W, S5_HALF_W)

    a_blk = pw[:, :, n].reshape(DEPTH, 2, S5_STATE)
    pows = [a_blk]
    for _ in range(SUBLANE - 1):
        pows.append(pows[-1] * a_blk)
    row = jnp.arange(SUBLANE)
    tabs = []
    for direction in range(2):
        t = []
        for step in S5_STEPS:
            live = (row >= step) if direction == 0 else (row < SUBLANE - step)
            t.append(jnp.where(live[None, :, None], pows[step - 1][:, direction, None, :], 0.0))
        order = range(SUBLANE) if direction == 0 else range(SUBLANE - 1, -1, -1)
        t.append(jnp.stack([pows[i][:, direction] for i in order], axis=1))
        planes = []
        for x in t:
            planes += [jnp.real(x), jnp.imag(x)]
        tabs.append(jnp.stack(planes, axis=1))
    tab = jnp.stack(tabs, axis=1).astype(F32)
    return wx, wt, wc, tab


OUT_TILE = 256
FF_BLOCK = 1024


def _outproj_kernel(x_ref, oa_ref, ob_ref, oc_ref, ys_ref, zs_ref, sd_ref, gw_ref, gb_ref,
                    ada_ref, g2_ref, wo_ref, w1_ref, w2_ref, gf_ref, *rest, final):
    out_refs, tok_ref = rest[:-1], rest[-1]
    ada = ada_ref[...]
    chunk = lambda j: ada[:, j * D_MODEL:(j + 1) * D_MODEL]
    gate1, sh2, sc2, gate2 = chunk(2), chunk(3), chunk(4), chunk(5)
    for j in range(S5_BLOCK):
        for h in range(GROUP_W // LANE):
            c0 = _s5_col(h, j)
            tok_ref[h, pl.ds(j, OUT_TILE // S5_BLOCK, stride=S5_BLOCK), :] = ys_ref[:, c0:c0 + LANE]
    ssm = jnp.concatenate([tok_ref[h] for h in range(GROUP_W // LANE)], axis=-1)
    ys = jax.nn.gelu(ssm + sd_ref[...] * zs_ref[...])
    od = ys * _sigmoid(_dot(ys.astype(BF16), gw_ref[...]) + gb_ref[...])
    mixed = jnp.concatenate([oa_ref[...], ob_ref[...], oc_ref[...], od], axis=-1).astype(BF16)
    x = x_ref[...] + gate1 * _dot(mixed, wo_ref[...])
    ms = jnp.mean(x * x, axis=-1, keepdims=True)
    h = (x * lax.rsqrt(ms + EPS) * (g2_ref[...] * (1.0 + sc2)) + sh2).astype(BF16)
    acc = jnp.zeros((OUT_TILE, D_MODEL), F32)
    for j in range(D_FF // FF_BLOCK):
        cols = slice(j * FF_BLOCK, (j + 1) * FF_BLOCK)
        a = jnp.maximum(_dot(h, w1_ref[:, cols]), 0.0)
        acc = acc + _dot((a * a).astype(BF16), w2_ref[cols, :])
    x = x + gate2 * acc
    out_refs[0][...] = x
    if final:
        ms = jnp.mean(x * x, axis=-1, keepdims=True)
        out_refs[1][...] = x * lax.rsqrt(ms + EPS) * gf_ref[...]


def _outproj_call(x, mix, s5_par, ada4, g2, w_out, w1, w2, gf, layer, cond_row, final):
    n_tok = x.shape[0]
    tok = pl.BlockSpec((OUT_TILE, D_MODEL), lambda i: (i, 0))
    grp = pl.BlockSpec((OUT_TILE, GROUP_W), lambda i: (i, 0))
    lay = lambda shape: pl.BlockSpec((None,) + shape, lambda i: (layer,) + (0,) * len(shape))
    n_out = 2 if final else 1
    res = pl.pallas_call(
        functools.partial(_outproj_kernel, final=final),
        grid=(n_tok // OUT_TILE,),
        in_specs=[tok, grp, grp, grp,
                  pl.BlockSpec((OUT_TILE // S5_BLOCK, S5_ROW_W), lambda i: (i, 0)), grp,
                  lay((1, GROUP_W)), lay((GROUP_W, GROUP_W)), lay((1, GROUP_W)),
                  pl.BlockSpec((None, None, 1, ADA_CHUNKS * D_MODEL),
                               lambda i: (layer, cond_row(i * OUT_TILE), 0, 0)),
                  lay((1, D_MODEL)), lay((D_MODEL, D_MODEL)), lay((D_MODEL, D_FF)), lay((D_FF, D_MODEL)),
                  pl.BlockSpec((1, D_MODEL), lambda i: (0, 0))],
        out_specs=[tok] * n_out,
        out_shape=[jax.ShapeDtypeStruct((n_tok, D_MODEL), F32)] * n_out,
        scratch_shapes=[pltpu.VMEM((GROUP_W // LANE, OUT_TILE, LANE), F32)],
        compiler_params=_params(1),
        name="outproj",
    )(x, *mix, *s5_par, ada4, g2, w_out, w1, w2, gf)
    return res


def _grid_pos_embed(n_tokens):
    rows = n_tokens // GRID_W
    r, col = jnp.meshgrid(jnp.arange(rows, dtype=F32), jnp.arange(GRID_W, dtype=F32), indexing='ij')
    r = r.reshape(-1)
    col = col.reshape(-1)
    quarter = D_MODEL // 4
    freq = jnp.exp(-math.log(POS_BASE) * jnp.arange(quarter, dtype=F32) / quarter)
    ar = r[:, None] * freq
    ac = col[:, None] * freq
    return jnp.concatenate([jnp.sin(ar), jnp.cos(ar), jnp.sin(ac), jnp.cos(ac)], axis=-1)


def _layer(x, l, n_seq, seq, cond_row, hg_s0, hg_states, s5_s0re, s5_s0im, P, final):
    z, zs, zs_blk = _inproj_call(x, P['ada4'], P['norm1_g'], P['w_in'], l, cond_row)
    o_conv = _conv_call(z, P['conv_w'], P['conv_b'], P['conv_ln_g'], P['conv_ln_b'], l, n_seq, seq)
    o_gm = _gmlp_call(z, P['gmlp_norm_g'], P['gmlp_ws'], P['gmlp_bs'], l)
    o_hg, hg_end = _hgrn_call(z, P['hgrn_lb'], P['hgrn_norm_g'], hg_s0, hg_states, l, n_seq, seq)
    y_s5, h_re, h_im = _s5_mix(zs_blk, P['s5'], s5_s0re, s5_s0im, l, n_seq, seq)
    res = _outproj_call(x, (o_conv, o_gm, o_hg, y_s5, zs), (P['s5_d'], P['s5_glu_w'], P['s5_glu_b']),
                        P['ada4'], P['norm2_g'], P['w_out'],
                        P['mlp_w1'], P['mlp_w2'], P['final_norm_g'], l, cond_row, final)
    return res, hg_end, h_re, h_im


def _prepare(w, hgrn_lb):
    vec = lambda a: a.astype(F32).reshape(DEPTH, 1, -1)
    hd = GROUP_W // GM_HEADS
    return {
        'norm1_g': vec(w['norm1_g']), 'norm2_g': vec(w['norm2_g']),
        'w_in': w['w_in'].astype(BF16), 'w_out': w['w_out'].astype(BF16),
        'mlp_w1': w['mlp_w1'].astype(BF16), 'mlp_w2': w['mlp_w2'].astype(BF16),
        'conv_w': jnp.broadcast_to(w['conv_w'].astype(F32)[:, :, None, :], (DEPTH, CONV_K, SUBLANE, GROUP_W)),
        'conv_b': vec(w['conv_b']), 'conv_ln_g': vec(w['conv_ln_g']), 'conv_ln_b': vec(w['conv_ln_b']),
        'gmlp_norm_g': vec(w['gmlp_norm_g']),
        'gmlp_ws': jnp.transpose(w['gmlp_ws'], (0, 2, 1, 3)).reshape(
            DEPTH, GM_CHUNK, GM_HEADS * GM_CHUNK).astype(BF16),
        'gmlp_bs': jnp.repeat(jnp.transpose(w['gmlp_bs'].astype(F32), (0, 2, 1)), hd, axis=2),
        'hgrn_lb': jnp.transpose(hgrn_lb, (1, 0, 2)), 'hgrn_norm_g': vec(w['hgrn_norm_g']),
        's5': _s5_constants(w['s5_a_re'], w['s5_a_im'], w['s5_log_dt'], w['s5_b_re'], w['s5_b_im'],
                            w['s5_c_re'], w['s5_c_im']),
        's5_d': vec(w['s5_d']), 's5_glu_w': w['s5_glu_w'].astype(BF16), 's5_glu_b': vec(w['s5_glu_b']),
        'final_norm_g': w['final_norm_g'].astype(F32).reshape(1, D_MODEL),
    }


def _run_stream(x, n_seq, seq, cond_row, hg_s0, s5_s0re, s5_s0im, P):
    hg_states, s5_re, s5_im = None, [], []
    y = None
    for l in range(DEPTH):
        final = l == DEPTH - 1
        res, hg_states, h_re, h_im = _layer(x, l, n_seq, seq, cond_row, hg_s0[l], hg_states,
                                            s5_s0re[l], s5_s0im[l], P, final)
        x = res[0]
        if final:
            y = res[1]
        s5_re.append(h_re)
        s5_im.append(h_im)
    return y, hg_states, s5_re, s5_im


def kernel(x_prompt, x_sample, state_hgrn, state_s5_re, state_s5_im, c, c_ctx, norm1_g, norm2_g, ada_w, ada_b, w_in, conv_w, conv_b, conv_ln_g, conv_ln_b, gmlp_norm_g, gmlp_ws, gmlp_bs, hgrn_lb_logits, hgrn_norm_g, s5_a_re, s5_a_im, s5_log_dt, s5_b_re, s5_b_im, s5_c_re, s5_c_im, s5_d, s5_glu_w, s5_glu_b, w_out, mlp_w1, mlp_w2, final_norm_g):
    n_ctx, seq_ctx, _ = x_prompt.shape
    n_lat, seq_lat, _ = x_sample.shape
    assert n_lat + 1 <= COND_ROWS
    assert seq_ctx % IN_TILE == 0 or IN_TILE % seq_ctx == 0

    lb_soft = jax.nn.softmax(hgrn_lb_logits.astype(F32), axis=1)
    hgrn_lb = jnp.cumsum(lb_soft, axis=1) - lb_soft[:, :1]
    P = _prepare(dict(
        norm1_g=norm1_g, norm2_g=norm2_g, w_in=w_in, w_out=w_out, mlp_w1=mlp_w1, mlp_w2=mlp_w2,
        conv_w=conv_w, conv_b=conv_b, conv_ln_g=conv_ln_g, conv_ln_b=conv_ln_b,
        gmlp_norm_g=gmlp_norm_g, gmlp_ws=gmlp_ws, gmlp_bs=gmlp_bs, hgrn_norm_g=hgrn_norm_g,
        s5_a_re=s5_a_re, s5_a_im=s5_a_im, s5_log_dt=s5_log_dt, s5_b_re=s5_b_re, s5_b_im=s5_b_im,
        s5_c_re=s5_c_re, s5_c_im=s5_c_im, s5_d=s5_d, s5_glu_w=s5_glu_w, s5_glu_b=s5_glu_b,
        final_norm_g=final_norm_g), hgrn_lb)
    cond =jnp.zeros((COND_ROWS, D_MODEL), F32).at[0].set(c_ctx.astype(F32)).at[1:1 + n_lat].set(c.astype(F32))
    ada = _ada_call(cond, ada_w.astype(F32), ada_b.astype(F32))
    P['ada4'] = ada.reshape(DEPTH, COND_ROWS, 1, ADA_CHUNKS * D_MODEL)

    zeros_s5 = jnp.zeros((n_ctx, 2, 1, S5_STATE), F32)
    y_ctx, hg_ctx, re_ctx, im_ctx = _run_stream(
        x_prompt.astype(F32).reshape(n_ctx * seq_ctx, D_MODEL), n_ctx, seq_ctx, lambda r: 0,
        [None] * DEPTH, [zeros_s5] * DEPTH, [zeros_s5] * DEPTH, P)

    xs = x_sample.astype(F32) + _grid_pos_embed(seq_lat)[None]
    hg0 = [_hgrn_state_to_kernel(state_hgrn[:, l]) for l in range(DEPTH)]
    re0 = [state_s5_re[:, l].astype(F32).reshape(n_lat, 2, 1, S5_STATE) for l in range(DEPTH)]
    im0 = [state_s5_im[:, l].astype(F32).reshape(n_lat, 2, 1, S5_STATE) for l in range(DEPTH)]
    y_lat, _, _, _ = _run_stream(xs.reshape(n_lat * seq_lat, D_MODEL), n_lat, seq_lat,
                                 lambda r: 1 + r // seq_lat, hg0, re0, im0, P)

    dt = x_prompt.dtype
    new_hg = _hgrn_state_from_kernel(hg_ctx).astype(dt)
    new_re = jnp.stack([s.reshape(n_ctx, 2, S5_GROUPS, S5_P) for s in re_ctx], axis=1).astype(dt)
    new_im = jnp.stack([s.reshape(n_ctx, 2, S5_GROUPS, S5_P) for s in im_ctx], axis=1).astype(dt)
    return (y_ctx.reshape(n_ctx, seq_ctx, D_MODEL).astype(dt),
            y_lat.reshape(n_lat, seq_lat, D_MODEL).astype(x_sample.dtype), new_hg, new_re, new_im)
```

```python
import functools
import math

import numpy as np
import jax
import jax.numpy as jnp
from jax import lax
from jax.experimental import pallas as pl
from jax.experimental.pallas import tpu as pltpu

D_MODEL = 1024
DEPTH = 4
GRID_W = 64
GROUP_W = 256
N_IN_BLOCKS = 10
D_FF = 4 * D_MODEL
EPS = 1e-6
F_MIN = 1e-30
ADA_CHUNKS = 6
CONV_K = 31
CONV_PAD = CONV_K // 2
GM_CHUNK = 128
GM_HEADS = 4
HG_HEADS = 4
HG_DK = 64
HG_CHUNK = 64
HG_TILE_LEVELS = 3
S5_IN = 16
S5_GROUPS = 16
S5_P = 64
S5_STATE = S5_GROUPS * S5_P
POS_BASE = 10000.0

SUBLANE = 8
LANE = 128
COND_ROWS = 8
VMEM_LIMIT = 56 * 1024 * 1024

F32 = jnp.float32
BF16 = jnp.bfloat16


def _sigmoid(x):
    return 1.0 / (1.0 + jnp.exp(-x))


def _silu(x):
    return x * _sigmoid(x)


def _params(n_parallel=1):
    return pltpu.CompilerParams(dimension_semantics=("arbitrary",) * n_parallel,
                                vmem_limit_bytes=VMEM_LIMIT)


def _split2(x):
    hi = x.astype(BF16)
    lo = (x - hi.astype(F32)).astype(BF16)
    return hi, lo


def _dot(a, b):
    return jnp.dot(a, b, preferred_element_type=F32)


def _dot_nt(a, b):
    return lax.dot_general(a, b, (((1,), (1,)), ((), ())), preferred_element_type=F32)


def _dot_tn(a, b):
    return lax.dot_general(a, b, (((0,), (0,)), ((), ())), preferred_element_type=F32)


ADA_BLOCK = 1536


def _ada_kernel(cond_ref, w_ref, b_ref, o_ref):
    s = _silu(cond_ref[...])
    o_ref[...] = _dot(s.astype(BF16), w_ref[...].astype(BF16)) + b_ref[...]


def _ada_call(cond, ada_w, ada_b):
    n_out = ADA_CHUNKS * D_MODEL
    return pl.pallas_call(
        _ada_kernel,
        grid=(DEPTH, n_out // ADA_BLOCK),
        in_specs=[pl.BlockSpec((COND_ROWS, D_MODEL), lambda l, j: (0, 0)),
                  pl.BlockSpec((None, D_MODEL, ADA_BLOCK), lambda l, j: (l, 0, j)),
                  pl.BlockSpec((None, 1, ADA_BLOCK), lambda l, j: (l, 0, j))],
        out_specs=pl.BlockSpec((None, COND_ROWS, ADA_BLOCK), lambda l, j: (l, 0, j)),
        out_shape=jax.ShapeDtypeStruct((DEPTH, COND_ROWS, n_out), F32),
        compiler_params=_params(2),
        name="ada",
    )(cond, ada_w, ada_b.reshape(DEPTH, 1, n_out))


IN_TILE = 512
Z_MAIN_W = (N_IN_BLOCKS - 1) * GROUP_W


def _inproj_kernel(x_ref, ada_ref, g_ref, w_ref, z_ref, zs_ref, zsb_ref, half_ref):
    x = x_ref[...]
    ada = ada_ref[...]
    sh1 = ada[:, 0:D_MODEL]
    sc1 = ada[:, D_MODEL:2 * D_MODEL]
    ms = jnp.mean(x * x, axis=-1, keepdims=True)
    h = x * lax.rsqrt(ms + EPS) * (g_ref[...] * (1.0 + sc1)) + sh1
    z = _dot(h.astype(BF16), w_ref[...])
    z_ref[...] = z[:, 0:Z_MAIN_W]
    zs_ref[...] = z[:, Z_MAIN_W:]
    for h in range(GROUP_W // LANE):
        half_ref[h] = z[:, Z_MAIN_W + h * LANE:Z_MAIN_W + (h + 1) * LANE]
    for j in range(S5_BLOCK):
        for h in range(GROUP_W // LANE):
            c0 = _s5_col(h, j)
            zsb_ref[:, c0:c0 + LANE] = (
                half_ref[h, pl.ds(j, IN_TILE // S5_BLOCK, stride=S5_BLOCK), :].astype(BF16))


def _inproj_call(x, ada4, g1, w_in, layer, cond_row):
    n_tok = x.shape[0]
    zw = N_IN_BLOCKS * GROUP_W
    tile = lambda w: pl.BlockSpec((IN_TILE, w), lambda i: (i, 0))
    return pl.pallas_call(
        _inproj_kernel,
        grid=(n_tok // IN_TILE,),
        in_specs=[pl.BlockSpec((IN_TILE, D_MODEL), lambda i: (i, 0)),
                  pl.BlockSpec((None, None, 1, ADA_CHUNKS * D_MODEL),
                               lambda i: (layer, cond_row(i * IN_TILE), 0, 0)),
                  pl.BlockSpec((None, 1, D_MODEL), lambda i: (layer, 0, 0)),
                  pl.BlockSpec((None, D_MODEL, zw), lambda i: (layer, 0, 0))],
        out_specs=[tile(Z_MAIN_W), tile(GROUP_W),
                   pl.BlockSpec((IN_TILE // S5_BLOCK, S5_ROW_W), lambda i: (i, 0))],
        out_shape=[jax.ShapeDtypeStruct((n_tok, Z_MAIN_W), F32),
                   jax.ShapeDtypeStruct((n_tok, GROUP_W), F32),
                   jax.ShapeDtypeStruct((n_tok // S5_BLOCK, S5_ROW_W), BF16)],
        scratch_shapes=[pltpu.VMEM((GROUP_W // LANE, IN_TILE, LANE), F32)],
        compiler_params=_params(1),
        name="inproj",
    )(x, ada4, g1, w_in)


CONV_TILE = 64
CONV_HALO = 16
CONV_FILL = 128
CONV_NORM_TILE = 256


def _conv_kernel(z_ref, w_ref, b_ref, lng_ref, lnb_ref, o_ref, pad_ref, *, seq):
    tail = seq + 2 * CONV_HALO - (seq + SUBLANE)
    for r in range(SUBLANE):
        pad_ref[r, 0:CONV_HALO, :] = jnp.zeros((CONV_HALO, GROUP_W), F32)
        pad_ref[r, seq + SUBLANE:seq + 2 * CONV_HALO, :] = jnp.zeros((tail, GROUP_W), F32)
    for t0 in range(0, seq, CONV_FILL):
        u = z_ref[t0:t0 + CONV_FILL, 0:GROUP_W] * _sigmoid(z_ref[t0:t0 + CONV_FILL, GROUP_W:2 * GROUP_W])
        for r in range(SUBLANE):
            pad_ref[r, CONV_HALO - r + t0:CONV_HALO - r + t0 + CONV_FILL, :] = u
    first = CONV_HALO - CONV_PAD

    def tile(i, carry):
        r0 = pl.multiple_of(i * CONV_TILE, CONV_TILE)
        acc = jnp.zeros((CONV_TILE, GROUP_W), F32)
        for r in range(SUBLANE):
            taps = [k for k in range(CONV_K) if (first + k) % SUBLANE == r]
            lo = first + taps[0] - r
            hi = first + taps[-1] - r
            win = pad_ref[r, pl.ds(r0 + lo, CONV_TILE + hi - lo), :]
            for k in taps:
                off = first + k - r - lo
                wk = jnp.concatenate([w_ref[k]] * (CONV_TILE // SUBLANE), axis=0)
                acc = acc + wk * win[off:off + CONV_TILE, :]
        o_ref[pl.ds(r0, CONV_TILE), :] = acc + b_ref[...]
        return carry

    lax.fori_loop(0, seq // CONV_TILE, tile, 0)

    def norm(i, carry):
        rows = pl.ds(pl.multiple_of(i * CONV_NORM_TILE, CONV_NORM_TILE), CONV_NORM_TILE)
        c = o_ref[rows, :]
        mu = jnp.mean(c, axis=-1, keepdims=True)
        cc = c - mu
        var = jnp.mean(cc * cc, axis=-1, keepdims=True)
        y = cc * lax.rsqrt(var + EPS) * lng_ref[...] + lnb_ref[...]
        o_ref[rows, :] = _silu(y)
        return carry

    lax.fori_loop(0, seq // CONV_NORM_TILE, norm, 0)


def _conv_call(z, conv_w, conv_b, ln_g, ln_b, layer, n_seq, seq):
    vec = pl.BlockSpec((None, 1, GROUP_W), lambda s: (layer, 0, 0))
    return pl.pallas_call(
        functools.partial(_conv_kernel, seq=seq),
        grid=(n_seq,),
        in_specs=[pl.BlockSpec((seq, 2 * GROUP_W), lambda s: (s, 0)),
                  pl.BlockSpec((None, CONV_K, SUBLANE, GROUP_W), lambda s: (layer, 0, 0, 0)),
                  vec, vec, vec],
        out_specs=pl.BlockSpec((seq, GROUP_W), lambda s: (s, 0)),
        out_shape=jax.ShapeDtypeStruct((n_seq * seq, GROUP_W), F32),
        scratch_shapes=[pltpu.VMEM((SUBLANE, seq + 2 * CONV_HALO, GROUP_W), F32)],
        compiler_params=_params(1),
        name="conv",
    )(z, conv_w, conv_b, ln_g, ln_b)


def _head_mask(rows_per_head, cols_per_head, n_rows, n_cols):
    r = lax.broadcasted_iota(jnp.int32, (n_rows, n_cols), 0) // rows_per_head
    c = lax.broadcasted_iota(jnp.int32, (n_rows, n_cols), 1) // cols_per_head
    return r == c


GM_TILE = 1024


def _gmlp_kernel(z_ref, g_ref, ws_ref, bs_ref, o_ref):
    hd = GROUP_W // GM_HEADS
    mask = _head_mask(GM_CHUNK, hd, GM_HEADS * GM_CHUNK, GROUP_W)
    for r0 in range(0, GM_TILE, GM_CHUNK):
        u = z_ref[r0:r0 + GM_CHUNK, 0:GROUP_W]
        v = z_ref[r0:r0 + GM_CHUNK, GROUP_W:2 * GROUP_W]
        vn = v * lax.rsqrt(jnp.mean(v * v, axis=-1, keepdims=True) + EPS) * g_ref[...]
        stack = jnp.where(mask, jnp.concatenate([vn] * GM_HEADS, axis=0), 0.0).astype(BF16)
        sv = _dot(ws_ref[...], stack) + bs_ref[...]
        o_ref[r0:r0 + GM_CHUNK, :] = u * sv


def _gmlp_call(z, norm_g, ws_cat, bs_full, layer):
    n_tok = z.shape[0]
    return pl.pallas_call(
        _gmlp_kernel,
        grid=(n_tok // GM_TILE,),
        in_specs=[pl.BlockSpec((GM_TILE, 2 * GROUP_W), lambda i: (i, 1)),
                  pl.BlockSpec((None, 1, GROUP_W), lambda i: (layer, 0, 0)),
                  pl.BlockSpec((None, GM_CHUNK, GM_HEADS * GM_CHUNK), lambda i: (layer, 0, 0)),
                  pl.BlockSpec((None, GM_CHUNK, GROUP_W), lambda i: (layer, 0, 0))],
        out_specs=pl.BlockSpec((GM_TILE, GROUP_W), lambda i: (i, 0)),
        out_shape=jax.ShapeDtypeStruct((n_tok, GROUP_W), F32),
        compiler_params=_params(1),
        name="gmlp",
    )(z, norm_g, ws_cat, bs_full)


def _hgrn_block_masks():
    t = np.arange(HG_CHUNK)
    blk = np.zeros((HG_TILE_LEVELS, HG_CHUNK, HG_CHUNK), np.float32)
    for lvl in range(HG_TILE_LEVELS):
        b = t >> (lvl + 1)
        blk[lvl] = (b[:, None] == b[None, :])
    return np.tile(blk, (1, 1, HG_HEADS))


_HG_BMASK = _hgrn_block_masks()
HG_NORM_TILE = 256


def _hgrn_kernel(zq_ref, zi_ref, zg_ref, zff_ref, zfb_ref, lb_ref, ng_ref, bmask_ref, *rest,
                 seq, n_seqs, has_state):
    s0_ref = rest[0] if has_state else None
    o_ref, s_out_ref, ob_ref, st_ref = rest[-4:]
    n = HG_CHUNK
    n_chunks = seq // n
    hmask = _head_mask(HG_DK, HG_DK, GROUP_W, GROUP_W)
    hmask_bf = jnp.where(hmask, 1.0, 0.0).astype(BF16)
    row = lax.broadcasted_iota(jnp.int32, (n, GROUP_W), 0)
    n_tiles = n // SUBLANE
    col_tile = (lax.broadcasted_iota(jnp.int32, (SUBLANE, GROUP_W), 1) % n) // SUBLANE
    sub3 = lax.broadcasted_iota(jnp.int32, (n_tiles, SUBLANE, GROUP_W), 1)

    def block_diag(a):
        return jnp.where(hmask, jnp.concatenate([a] * HG_HEADS, axis=0), 0.0).astype(BF16)

    def spread_rows(a, first, period):
        return jnp.concatenate([jnp.broadcast_to(a[r:r + 1, :], (period, GROUP_W))
                                for r in range(first, n, period)], axis=0)

    def boundary(cum, lvl, direction):
        m = 1 << lvl
        at = m - 1 if direction == 0 else m
        if 2 * m >= SUBLANE:
            return spread_rows(cum, at, 2 * m)
        lo = spread_rows(cum, at, SUBLANE)
        hi = spread_rows(cum, at + 2 * m, SUBLANE)
        return jnp.where((row & (2 * m)) == 0, lo, hi)

    def chunk_step(s, direction, c, zf_ref, dst_ref):
        lb = lb_ref[direction:direction + 1, :]
        end_row = n - 1 if direction == 0 else 0
        pos = row if direction == 0 else n - 1 - row
        rows = pl.ds(pl.multiple_of(s * seq + c * n, n), n)
        q = _silu(zq_ref[rows, :])
        v = zi_ref[rows, :]
        f = lb + (1.0 - lb) * _sigmoid(zf_ref[rows, :])
        k = 1.0 - f
        f = jnp.maximum(f, F_MIN)
        cum3 = jnp.log2(f).reshape(n_tiles, SUBLANE, GROUP_W)
        for d in (1, 2, 4):
            shifted = pltpu.roll(cum3, d if direction == 0 else SUBLANE - d, 1)
            live = sub3 >= d if direction == 0 else sub3 < SUBLANE - d
            cum3 = cum3 + jnp.where(live, shifted, 0.0)
        edge = SUBLANE - 1 if direction == 0 else 0
        order = range(n_tiles) if direction == 0 else range(n_tiles - 1, -1, -1)
        tiles, run = [None] * n_tiles, None
        for t in order:
            tiles[t] = cum3[t] if run is None else cum3[t] + run
            run = jnp.broadcast_to(tiles[t][edge:edge + 1, :], (SUBLANE, GROUP_W))
        cum = jnp.concatenate(tiles, axis=0)
        total = cum[end_row:end_row + 1]
        w_cum = jnp.exp2(cum)
        w_rem = jnp.exp2(total - cum)
        w_end = jnp.exp2(total)
        st = st_ref[s, direction]
        inter = _dot_nt((q * w_cum).astype(BF16), st.astype(BF16))

        tile_end = spread_rows(cum, edge, SUBLANE)
        k_far = k * jnp.exp2(tile_end - cum)
        pairs = [(j, i) for j in range(n_tiles)
                 for i in (range(j + 1, n_tiles) if direction == 0 else range(j))]
        tile = lambda a, i: a[i * SUBLANE:(i + 1) * SUBLANE, :]
        q_far = jnp.concatenate([tile(q, i) * jnp.exp2(tile(cum, i) - tile(tile_end, j)) for j, i in pairs], axis=0)
        far = _dot_nt(q_far.astype(BF16), block_diag(k_far))
        acc = [jnp.zeros((SUBLANE, GROUP_W), F32)] * n_tiles
        for p, (j, i) in enumerate(pairs):
            acc[i] = acc[i] + jnp.where(col_tile == j, tile(far, p), 0.0)
        scores = jnp.concatenate(acc, axis=0)

        for lvl in range(HG_TILE_LEVELS):
            upper = (pos & (1 << lvl)) != 0
            if lvl == 0:
                wl = jnp.where(upper, f, 1.0)
            else:
                ref = boundary(cum, lvl, direction)
                wl = jnp.exp2(jnp.where(upper, cum - ref, ref - cum))
            ql = jnp.where(upper, q * wl, 0.0).astype(BF16)
            kl = jnp.where(upper, 0.0, k * wl)
            scores = scores + bmask_ref[lvl] * _dot_nt(ql, block_diag(kl))
        diag = _dot((q * k).astype(BF16), hmask_bf)
        intra = _dot(scores.astype(BF16), block_diag(v)) + diag * v
        upd = _dot_tn(v.astype(BF16), (k * w_rem).astype(BF16))
        st_ref[s, direction] = st * w_end + jnp.where(hmask, upd, 0.0)
        dst_ref[rows, :] = inter + intra

    st_ref[...] = s0_ref[...] if has_state else jnp.zeros(st_ref.shape, F32)

    def both(i, carry):
        for s in range(n_seqs):
            chunk_step(s, 0, i, zff_ref, o_ref)
            chunk_step(s, 1, n_chunks - 1 - i, zfb_ref, ob_ref)
        return carry

    lax.fori_loop(0, n_chunks, both, 0)

    pick = (lax.broadcasted_iota(jnp.int32, (GROUP_W, HG_DK), 0) % HG_DK
            == lax.broadcasted_iota(jnp.int32, (GROUP_W, HG_DK), 1))
    pick_bf = jnp.where(pick, 1.0, 0.0).astype(BF16)
    for s in range(n_seqs):
        for direction in range(2):
            hi, lo = _split2(st_ref[s, direction])
            s_out_ref[s, direction] = _dot_tn(hi, pick_bf) + _dot_tn(lo, pick_bf)

    def finish(i, carry):
        rows = pl.ds(pl.multiple_of(i * HG_NORM_TILE, HG_NORM_TILE), HG_NORM_TILE)
        o = o_ref[rows, :] + ob_ref[rows, :]
        hi, mid = _split2(o * o)
        ms = (_dot(hi, hmask_bf) + _dot(mid, hmask_bf)) * (1.0 / HG_DK)
        y = o * lax.rsqrt(ms + EPS) * ng_ref[...]
        o_ref[rows, :] = y * _silu(zg_ref[rows, :])
        return carry

    lax.fori_loop(0, n_seqs * seq // HG_NORM_TILE, finish, 0)


HG_ROWS_PER_STEP = 2048
HG_MIN_SEQS_PER_STEP = 2


def _hgrn_call(z, lb, norm_g, s0t, states, layer, n_seq, seq):
    per_step = min(n_seq, max(HG_MIN_SEQS_PER_STEP, HG_ROWS_PER_STEP // seq))
    assert n_seq % per_step == 0
    rows = per_step * seq
    mode = pl.Buffered(1) if rows > HG_ROWS_PER_STEP else None

    def zcol(j):
        return pl.BlockSpec((rows, GROUP_W), lambda s: (s, j), pipeline_mode=mode)

    full = lambda shape: pl.BlockSpec(shape, lambda s: (0,) * len(shape))
    has_state = s0t is not None
    state_specs = [pl.BlockSpec((per_step, 2, GROUP_W, GROUP_W), lambda s: (s, 0, 0, 0))] if has_state else []
    state_args = (s0t,) if has_state else ()
    in_specs = [zcol(4), zcol(5), zcol(6), zcol(7), zcol(8),
                pl.BlockSpec((None, 2, GROUP_W), lambda s: (layer, 0, 0)),
                pl.BlockSpec((None, 1, GROUP_W), lambda s: (layer, 0, 0)),
                full((HG_TILE_LEVELS, HG_CHUNK, GROUP_W))] + state_specs
    aliases = {}
    if states is not None:
        aliases = {len(in_specs): 1}
        in_specs = in_specs + [pl.BlockSpec(memory_space=pl.ANY)]
        state_args = state_args + (states,)
    return pl.pallas_call(
        functools.partial(_hgrn_kernel, seq=seq, n_seqs=per_step, has_state=has_state),
        grid=(n_seq // per_step,),
        in_specs=in_specs,
        out_specs=[pl.BlockSpec((rows, GROUP_W), lambda s: (s, 0)),
                   pl.BlockSpec((per_step, None, 2, GROUP_W, HG_DK), lambda s: (s, layer, 0, 0, 0))],
        out_shape=[jax.ShapeDtypeStruct((n_seq * seq, GROUP_W), F32),
                   jax.ShapeDtypeStruct((n_seq, DEPTH, 2, GROUP_W, HG_DK), F32)],
        scratch_shapes=[pltpu.VMEM((rows, GROUP_W), F32), pltpu.VMEM((per_step, 2, GROUP_W, GROUP_W), F32)],
        input_output_aliases=aliases,
        compiler_params=_params(1),
        name="hgrn",
    )(z, z, z, z, z, lb, norm_g, jnp.asarray(_HG_BMASK), *state_args)


def _hgrn_state_to_kernel(s):
    st = jnp.swapaxes(s.astype(F32), -1, -2)
    eye = jnp.eye(HG_HEADS, dtype=F32)
    full = jnp.einsum('ndhvk,hg->ndhvgk', st, eye)
    return full.reshape(s.shape[0], 2, GROUP_W, GROUP_W)


def _hgrn_state_from_kernel(st):
    return st.reshape(st.shape[:3] + (HG_HEADS, HG_DK, HG_DK))


S5_BLOCK = 8
S5_ROW_W = S5_BLOCK * GROUP_W
S5_HALVES = GROUP_W // LANE
S5_HALF_W = S5_BLOCK * LANE
S5_GROUPS_PER_HALF = S5_GROUPS // S5_HALVES
S5_PARTS = 4
S5_XW = S5_PARTS * S5_STATE
S5_PART_HALF = S5_STATE // S5_HALVES
S5_STEPS = (1, 2, 4)
S5_TABLES = 2 * (len(S5_STEPS) + 1)


def _s5_col(half, token):
    return half * S5_HALF_W + token * LANE


def _s5_in_kernel(u_ref, w_ref, x_ref):
    x_ref[...] = _dot(u_ref[...], w_ref[...])


def _s5_in_call(ub, wx, layer):
    n_blk = ub.shape[0]
    return pl.pallas_call(
        _s5_in_kernel,
        grid=(S5_HALVES, S5_PARTS),
        in_specs=[pl.BlockSpec((n_blk, S5_HALF_W), lambda h, p: (0, h)),
                  pl.BlockSpec((None, None, S5_HALF_W, S5_PART_HALF), lambda h, p: (layer, h, 0, p))],
        out_specs=pl.BlockSpec((n_blk, S5_PART_HALF), lambda h, p: (0, p * S5_HALVES + h)),
        out_shape=jax.ShapeDtypeStruct((n_blk, S5_XW), F32),
        compiler_params=_params(2),
        name="s5_in",
    )(ub, wx)


def _s5_scan_kernel(x_ref, tab_ref, s0re_ref, s0im_ref, h_ref, hre_ref, him_ref, *, n_blk):
    n_pairs = n_blk // (2 * SUBLANE)
    row = lax.broadcasted_iota(jnp.int32, (SUBLANE, S5_STATE), 0)
    n_steps = len(S5_STEPS)

    for direction in range(2):
        c_re = 2 * direction * S5_STATE
        c_im = c_re + S5_STATE
        edge = 0 if direction == 0 else SUBLANE - 1
        last = SUBLANE - 1 - edge
        unit = 0 if direction == 0 else SUBLANE - 1

        def tile(rows, carry):
            hc_r, hc_i, px_r, px_i = carry
            xr = x_ref[rows, c_re:c_re + S5_STATE]
            xi = x_ref[rows, c_im:c_im + S5_STATE]
            shift = 1 if direction == 0 else SUBLANE - 1
            hr = jnp.where(row == edge, px_r, pltpu.roll(xr, shift, 0))
            hi = jnp.where(row == edge, px_i, pltpu.roll(xi, shift, 0))
            for s, step in enumerate(S5_STEPS):
                sh = step if direction == 0 else SUBLANE - step
                rr = pltpu.roll(hr, sh, 0)
                ri = pltpu.roll(hi, sh, 0)
                ar = tab_ref[direction, 2 * s]
                ai = tab_ref[direction, 2 * s + 1]
                hr, hi = hr + ar * rr - ai * ri, hi + ar * ri + ai * rr
            pr = tab_ref[direction, 2 * n_steps]
            pi = tab_ref[direction, 2 * n_steps + 1]
            hr, hi = hr + pr * hc_r - pi * hc_i, hi + pr * hc_i + pi * hc_r
            bc = lambda a: jnp.broadcast_to(a[last:last + 1, :], (SUBLANE, S5_STATE))
            return hr, hi, (bc(hr), bc(hi), bc(xr), bc(xi))

        def pair(j, carry):
            jj = j if direction == 0 else n_pairs - 1 - j
            base = pl.multiple_of(jj * 2 * SUBLANE, 2 * SUBLANE)
            offs = (0, SUBLANE) if direction == 0 else (SUBLANE, 0)
            out = {}
            for off in offs:
                hr, hi, carry = tile(pl.ds(base + off, SUBLANE), carry)
                out[off] = (hr, hi)
            both = pl.ds(base, 2 * SUBLANE)
            h_ref[both, c_re:c_re + S5_STATE] = jnp.concatenate([out[0][0], out[SUBLANE][0]], 0).astype(BF16)
            h_ref[both, c_im:c_im + S5_STATE] = jnp.concatenate([out[0][1], out[SUBLANE][1]], 0).astype(BF16)
            return carry

        zero = jnp.zeros((SUBLANE, S5_STATE), F32)
        init = (zero, zero,
                jnp.broadcast_to(s0re_ref[direction], (SUBLANE, S5_STATE)),
                jnp.broadcast_to(s0im_ref[direction], (SUBLANE, S5_STATE)))
        hc_r, hc_i, px_r, px_i = lax.fori_loop(0, n_pairs, pair, init)
        a_r = tab_ref[direction, 2 * n_steps][unit:unit + 1, :]
        a_i = tab_ref[direction, 2 * n_steps + 1][unit:unit + 1, :]
        hre_ref[direction] = a_r * hc_r[0:1, :] - a_i * hc_i[0:1, :] + px_r[0:1, :]
        him_ref[direction] = a_r * hc_i[0:1, :] + a_i * hc_r[0:1, :] + px_i[0:1, :]


def _s5_scan_call(x, tab, s0re, s0im, layer, n_seq, n_blk):
    state = pl.BlockSpec((None, 2, 1, S5_STATE), lambda s: (s, 0, 0, 0))
    return pl.pallas_call(
        functools.partial(_s5_scan_kernel, n_blk=n_blk),
        grid=(n_seq,),
        in_specs=[pl.BlockSpec((n_blk, S5_XW), lambda s: (s, 0)),
                  pl.BlockSpec((None, 2, S5_TABLES, SUBLANE, S5_STATE), lambda s: (layer, 0, 0, 0, 0)),
                  state, state],
        out_specs=[pl.BlockSpec((n_blk, S5_XW), lambda s: (s, 0)), state, state],
        out_shape=[jax.ShapeDtypeStruct((n_seq * n_blk, S5_XW), BF16),
                   jax.ShapeDtypeStruct((n_seq, 2, 1, S5_STATE), F32),
                   jax.ShapeDtypeStruct((n_seq, 2, 1, S5_STATE), F32)],
        compiler_params=_params(1),
        name="s5_scan",
    )(x, tab, s0re, s0im)


def _s5_out_kernel(u_ref, *refs):
    h_refs, (wt_ref, wc_ref, y_ref) = refs[:S5_PARTS], refs[S5_PARTS:]
    y = _dot(u_ref[...], wt_ref[...])
    for p in range(S5_PARTS):
        y = y + _dot(h_refs[p][...], wc_ref[p * S5_PART_HALF:(p + 1) * S5_PART_HALF, :])
    y_ref[...] = y


def _s5_out_call(ub, h, wt, wc, layer):
    n_blk = ub.shape[0]

    def state_part(p):
        return pl.BlockSpec((n_blk, S5_PART_HALF), lambda j: (0, p * S5_HALVES + j))

    return pl.pallas_call(
        _s5_out_kernel,
        grid=(S5_HALVES,),
        in_specs=[pl.BlockSpec((n_blk, S5_HALF_W), lambda j: (0, j))]
                 + [state_part(p) for p in range(S5_PARTS)]
                 + [pl.BlockSpec((None, None, S5_HALF_W, S5_HALF_W), lambda j: (layer, j, 0, 0)),
                    pl.BlockSpec((None, None, S5_PARTS * S5_PART_HALF, S5_HALF_W), lambda j: (layer, j, 0, 0))],
        out_specs=pl.BlockSpec((n_blk, S5_HALF_W), lambda j: (0, j)),
        out_shape=jax.ShapeDtypeStruct((n_blk, S5_ROW_W), F32),
        compiler_params=_params(1),
        name="s5_out",
    )(ub, *([h] * S5_PARTS), wt, wc)


def _s5_mix(ub, consts, s0re, s0im, layer, n_seq, seq):
    wx, wt, wc, tab = consts
    n_blk = seq // S5_BLOCK
    x = _s5_in_call(ub, wx, layer)
    h, h_re, h_im = _s5_scan_call(x, tab, s0re, s0im, layer, n_seq, n_blk)
    y = _s5_out_call(ub, h, wt, wc, layer)
    return y, h_re, h_im


S5_EXPAND_ROWS = 512


def _expand_kernel(c_ref, t_ref, o_ref, keep_ref, *, row_group, col_group):
    rows, cols = o_ref.shape

    @pl.when((pl.program_id(0) == 0) & (pl.program_id(1) == 0))
    def _():
        r = lax.broadcasted_iota(jnp.int32, (rows, cols), 0)
        col = lax.broadcasted_iota(jnp.int32, (rows, cols), 1)
        group = lambda i, period_width: (i % period_width[0]) // period_width[1]
        keep_ref[...] = jnp.where(group(r, row_group) == group(col, col_group), 1.0, 0.0)

    o_ref[...] = (_dot(c_ref[...], t_ref[...]) * keep_ref[...]).astype(BF16)


def _expand_call(compact, spread, row_group, col_group):
    n_ops, n_rows, k = compact.shape
    n_cols = spread.shape[1]
    assert S5_EXPAND_ROWS % row_group[0] == 0
    return pl.pallas_call(
        functools.partial(_expand_kernel, row_group=row_group, col_group=col_group),
        grid=(n_ops, n_rows // S5_EXPAND_ROWS),
        in_specs=[pl.BlockSpec((None, S5_EXPAND_ROWS, k), lambda l, i: (l, i, 0)),
                  pl.BlockSpec((k, n_cols), lambda l, i: (0, 0))],
        out_specs=pl.BlockSpec((None, S5_EXPAND_ROWS, n_cols), lambda l, i: (l, i, 0)),
        out_shape=jax.ShapeDtypeStruct((n_ops, n_rows, n_cols), BF16),
        scratch_shapes=[pltpu.VMEM((S5_EXPAND_ROWS, n_cols), F32)],
        compiler_params=_params(2),
        name="s5_expand",
    )(compact, spread)


def _s5_constants(a_re, a_im, log_dt, b_re, b_im, c_re, c_im):
    n = S5_BLOCK
    a = lax.complex(a_re.astype(F32), a_im.astype(F32))
    dt = jnp.exp(log_dt.astype(F32))[..., None]
    a_bar = jnp.exp(a * dt)
    b_bar = ((a_bar - 1.0) / a)[..., None] * lax.complex(b_re.astype(F32), b_im.astype(F32))
    c = lax.complex(c_re.astype(F32), c_im.astype(F32))
    steps = jnp.arange(n + 1, dtype=F32)[None, None, :, None, None]
    pw = jnp.exp((a * dt)[:, :, None] * steps)
    gh = S5_GROUPS_PER_HALF
    kron = lambda *m: functools.reduce(np.kron, m)
    spread_state = jnp.asarray(kron(np.eye(S5_PARTS), np.ones((1, gh)), np.eye(S5_P)), BF16)
    spread_token = jnp.asarray(kron(np.eye(n), np.ones((1, gh)), np.eye(S5_IN)), BF16)
    token_group = (LANE, S5_IN)
    state_group = (S5_PART_HALF, S5_P)
    halves = lambda a: a.reshape(a.shape[:3] + (S5_HALVES, gh) + a.shape[4:])

    w_f = pw[:, 0, ::-1][:, 1:, :, :, None] * b_bar[:, 0, None]
    w_b = pw[:, 1, :n, :, :, None] * b_bar[:, 1, None]
    w4 = jnp.stack([jnp.real(w_f), jnp.imag(w_f), jnp.real(w_b), jnp.imag(w_b)], axis=1)
    wx_c = jnp.transpose(halves(w4), (0, 3, 2, 4, 6, 1, 5))
    wx_c = wx_c.reshape(DEPTH * S5_HALVES, S5_HALF_W, S5_PARTS * S5_P)
    wx = _expand_call(wx_c.astype(BF16), spread_state, token_group, state_group)
    wx = wx.reshape(DEPTH, S5_HALVES, S5_HALF_W, S5_PARTS * S5_PART_HALF)

    o_f = c[:, 0, None] * pw[:, 0, 1:, :, None, :]
    o_b = c[:, 1, None] * pw[:, 1, ::-1][:, :n, :, None, :]
    o4 = jnp.stack([jnp.real(o_f), -jnp.imag(o_f), jnp.real(o_b), -jnp.imag(o_b)], axis=1)
    wc_c = jnp.transpose(halves(o4), (0, 3, 1, 4, 6, 2, 5))
    wc_c = wc_c.reshape(DEPTH * S5_HALVES, S5_PARTS * S5_PART_HALF, n * S5_IN)
    wc = _expand_call(wc_c.astype(BF16), spread_token, state_group, token_group)
    wc = wc.reshape(DEPTH, S5_HALVES, S5_PARTS * S5_PART_HALF, S5_HALF_W)

    k_f = jnp.real(jnp.einsum('lgcp,ltgp,lgpd->lgdtc', c[:, 0], pw[:, 0, :n], b_bar[:, 0]))
    k_b = jnp.real(jnp.einsum('lgcp,ltgp,lgpd->lgdtc', c[:, 1], pw[:, 1, :n], b_bar[:, 1]))
    tt, jj, ii = np.meshgrid(np.arange(n), np.arange(n), np.arange(n), indexing='ij')
    place = lambda hit: jnp.asarray(np.kron(hit.reshape(n, n * n).astype(np.float32), np.eye(S5_IN)), F32)
    flat = lambda k: k.reshape(DEPTH * S5_GROUPS * S5_IN, n * S5_IN)
    t_all = (jnp.dot(flat(k_f), place(ii - jj == tt), precision=lax.Precision.HIGHEST)
             + jnp.dot(flat(k_b), place(jj - ii == tt), precision=lax.Precision.HIGHEST))
    wt_c = t_all.reshape(DEPTH, S5_HALVES, gh, S5_IN, n, n * S5_IN)
    wt_c = jnp.transpose(wt_c, (0, 1, 4, 2, 3, 5)).reshape(DEPTH * S5_HALVES, S5_HALF_W, n * S5_IN)
    wt = _expand_call(wt_c.astype(BF16), spread_token, token_group, token_group)
    wt = wt.reshape(DEPTH, S5_HALVES, S5_HALF_W, S5_HALF_W)

    block_steps = n * jnp.arange(1, SUBLANE + 1, dtype=F32)[None, None, :, None]
    pows_all = jnp.exp((a * dt).reshape(DEPTH, 2, 1, S5_STATE) * block_steps)
    pows = [pows_all[:, :, i] for i in range(SUBLANE)]
    row = jnp.arange(SUBLANE)
    tabs = []
    for direction in range(2):
        t = []
        for step in S5_STEPS:
            live = (row >= step) if direction == 0 else (row < SUBLANE - step)
            t.append(jnp.where(live[None, :, None], pows[step - 1][:, direction, None, :], 0.0))
        order = range(SUBLANE) if direction == 0 else range(SUBLANE - 1, -1, -1)
        t.append(jnp.stack([pows[i][:, direction] for i in order], axis=1))
        planes = []
        for x in t:
            planes += [jnp.real(x), jnp.imag(x)]
        tabs.append(jnp.stack(planes, axis=1))
    tab = jnp.stack(tabs, axis=1).astype(F32)
    return wx, wt, wc, tab


OUT_TILE = 256
FF_BLOCK = 1024


def _outproj_kernel(x_ref, oa_ref, ob_ref, oc_ref, ys_ref, zs_ref, sd_ref, gw_ref, gb_ref,
                    ada_ref, g2_ref, wo_ref, w1_ref, w2_ref, gf_ref, *rest, final):
    out_refs, tok_ref = rest[:-1], rest[-1]
    ada = ada_ref[...]
    chunk = lambda j: ada[:, j * D_MODEL:(j + 1) * D_MODEL]
    gate1, sh2, sc2, gate2 = chunk(2), chunk(3), chunk(4), chunk(5)
    for j in range(S5_BLOCK):
        for h in range(GROUP_W // LANE):
            c0 = _s5_col(h, j)
            tok_ref[h, pl.ds(j, OUT_TILE // S5_BLOCK, stride=S5_BLOCK), :] = ys_ref[:, c0:c0 + LANE]
    ssm = jnp.concatenate([tok_ref[h] for h in range(GROUP_W // LANE)], axis=-1)
    ys = jax.nn.gelu(ssm + sd_ref[...] * zs_ref[...])
    od = ys * _sigmoid(_dot(ys.astype(BF16), gw_ref[...]) + gb_ref[...])
    mixed = jnp.concatenate([oa_ref[...], ob_ref[...], oc_ref[...], od], axis=-1).astype(BF16)
    x = x_ref[...] + gate1 * _dot(mixed, wo_ref[...])
    ms = jnp.mean(x * x, axis=-1, keepdims=True)
    h = (x * lax.rsqrt(ms + EPS) * (g2_ref[...] * (1.0 + sc2)) + sh2).astype(BF16)
    acc = jnp.zeros((OUT_TILE, D_MODEL), F32)
    for j in range(D_FF // FF_BLOCK):
        cols = slice(j * FF_BLOCK, (j + 1) * FF_BLOCK)
        a = jnp.maximum(_dot(h, w1_ref[:, cols]), 0.0)
        acc = acc + _dot((a * a).astype(BF16), w2_ref[cols, :])
    x = x + gate2 * acc
    out_refs[0][...] = x
    if final:
        ms = jnp.mean(x * x, axis=-1, keepdims=True)
        out_refs[1][...] = x * lax.rsqrt(ms + EPS) * gf_ref[...]


def _outproj_call(x, mix, s5_par, ada4, g2, w_out, w1, w2, gf, layer, cond_row, final):
    n_tok = x.shape[0]
    tok = pl.BlockSpec((OUT_TILE, D_MODEL), lambda i: (i, 0))
    grp = pl.BlockSpec((OUT_TILE, GROUP_W), lambda i: (i, 0))
    lay = lambda shape: pl.BlockSpec((None,) + shape, lambda i: (layer,) + (0,) * len(shape))
    n_out = 2 if final else 1
    res = pl.pallas_call(
        functools.partial(_outproj_kernel, final=final),
        grid=(n_tok // OUT_TILE,),
        in_specs=[tok, grp, grp, grp,
                  pl.BlockSpec((OUT_TILE // S5_BLOCK, S5_ROW_W), lambda i: (i, 0)), grp,
                  lay((1, GROUP_W)), lay((GROUP_W, GROUP_W)), lay((1, GROUP_W)),
                  pl.BlockSpec((None, None, 1, ADA_CHUNKS * D_MODEL),
                               lambda i: (layer, cond_row(i * OUT_TILE), 0, 0)),
                  lay((1, D_MODEL)), lay((D_MODEL, D_MODEL)), lay((D_MODEL, D_FF)), lay((D_FF, D_MODEL)),
                  pl.BlockSpec((1, D_MODEL), lambda i: (0, 0))],
        out_specs=[tok] * n_out,
        out_shape=[jax.ShapeDtypeStruct((n_tok, D_MODEL), F32)] * n_out,
        scratch_shapes=[pltpu.VMEM((GROUP_W // LANE, OUT_TILE, LANE), F32)],
        compiler_params=_params(1),
        name="outproj",
    )(x, *mix, *s5_par, ada4, g2, w_out, w1, w2, gf)
    return res


def _grid_pos_embed(n_tokens):
    rows = n_tokens // GRID_W
    r, col = jnp.meshgrid(jnp.arange(rows, dtype=F32), jnp.arange(GRID_W, dtype=F32), indexing='ij')
    r = r.reshape(-1)
    col = col.reshape(-1)
    quarter = D_MODEL // 4
    freq = jnp.exp(-math.log(POS_BASE) * jnp.arange(quarter, dtype=F32) / quarter)
    ar = r[:, None] * freq
    ac = col[:, None] * freq
    return jnp.concatenate([jnp.sin(ar), jnp.cos(ar), jnp.sin(ac), jnp.cos(ac)], axis=-1)


def _layer(x, l, n_seq, seq, cond_row, hg_s0, hg_states, s5_s0re, s5_s0im, P, final):
    z, zs, zs_blk = _inproj_call(x, P['ada4'], P['norm1_g'], P['w_in'], l, cond_row)
    o_conv = _conv_call(z, P['conv_w'], P['conv_b'], P['conv_ln_g'], P['conv_ln_b'], l, n_seq, seq)
    o_gm = _gmlp_call(z, P['gmlp_norm_g'], P['gmlp_ws'], P['gmlp_bs'], l)
    o_hg, hg_end = _hgrn_call(z, P['hgrn_lb'], P['hgrn_norm_g'], hg_s0, hg_states, l, n_seq, seq)
    y_s5, h_re, h_im = _s5_mix(zs_blk, P['s5'], s5_s0re, s5_s0im, l, n_seq, seq)
    res = _outproj_call(x, (o_conv, o_gm, o_hg, y_s5, zs), (P['s5_d'], P['s5_glu_w'], P['s5_glu_b']),
                        P['ada4'], P['norm2_g'], P['w_out'],
                        P['mlp_w1'], P['mlp_w2'], P['final_norm_g'], l, cond_row, final)
    return res, hg_end, h_re, h_im


def _prepare(w, hgrn_lb):
    vec = lambda a: a.astype(F32).reshape(DEPTH, 1, -1)
    hd = GROUP_W // GM_HEADS
    return {
        'norm1_g': vec(w['norm1_g']), 'norm2_g': vec(w['norm2_g']),
        'w_in': w['w_in'].astype(BF16), 'w_out': w['w_out'].astype(BF16),
        'mlp_w1': w['mlp_w1'].astype(BF16), 'mlp_w2': w['mlp_w2'].astype(BF16),
        'conv_w': jnp.broadcast_to(w['conv_w'].astype(F32)[:, :, None, :], (DEPTH, CONV_K, SUBLANE, GROUP_W)),
        'conv_b': vec(w['conv_b']), 'conv_ln_g': vec(w['conv_ln_g']), 'conv_ln_b': vec(w['conv_ln_b']),
        'gmlp_norm_g': vec(w['gmlp_norm_g']),
        'gmlp_ws': jnp.transpose(w['gmlp_ws'], (0, 2, 1, 3)).reshape(
            DEPTH, GM_CHUNK, GM_HEADS * GM_CHUNK).astype(BF16),
        'gmlp_bs': jnp.repeat(jnp.transpose(w['gmlp_bs'].astype(F32), (0, 2, 1)), hd, axis=2),
        'hgrn_lb': jnp.transpose(hgrn_lb, (1, 0, 2)), 'hgrn_norm_g': vec(w['hgrn_norm_g']),
        's5': _s5_constants(w['s5_a_re'], w['s5_a_im'], w['s5_log_dt'], w['s5_b_re'], w['s5_b_im'],
                            w['s5_c_re'], w['s5_c_im']),
        's5_d': vec(w['s5_d']), 's5_glu_w': w['s5_glu_w'].astype(BF16), 's5_glu_b': vec(w['s5_glu_b']),
        'final_norm_g': w['final_norm_g'].astype(F32).reshape(1, D_MODEL),
    }


def _run_stream(x, n_seq, seq, cond_row, hg_s0, s5_s0re, s5_s0im, P):
    hg_states, s5_re, s5_im = None, [], []
    y = None
    for l in range(DEPTH):
        final = l == DEPTH - 1
        res, hg_states, h_re, h_im = _layer(x, l, n_seq, seq, cond_row, hg_s0[l], hg_states,
                                            s5_s0re[l], s5_s0im[l], P, final)
        x = res[0]
        if final:
            y = res[1]
        s5_re.append(h_re)
        s5_im.append(h_im)
    return y, hg_states, s5_re, s5_im


def kernel(x_prompt, x_sample, state_hgrn, state_s5_re, state_s5_im, c, c_ctx, norm1_g, norm2_g, ada_w, ada_b, w_in, conv_w, conv_b, conv_ln_g, conv_ln_b, gmlp_norm_g, gmlp_ws, gmlp_bs, hgrn_lb_logits, hgrn_norm_g, s5_a_re, s5_a_im, s5_log_dt, s5_b_re, s5_b_im, s5_c_re, s5_c_im, s5_d, s5_glu_w, s5_glu_b, w_out, mlp_w1, mlp_w2, final_norm_g):
    n_ctx, seq_ctx, _ = x_prompt.shape
    n_lat, seq_lat, _ = x_sample.shape
    assert n_lat + 1 <= COND_ROWS
    assert seq_ctx % IN_TILE == 0 or IN_TILE % seq_ctx == 0

    lb_soft = jax.nn.softmax(hgrn_lb_logits.astype(F32), axis=1)
    hgrn_lb = jnp.cumsum(lb_soft, axis=1) - lb_soft[:, :1]
    P = _prepare(dict(
        norm1_g=norm1_g, norm2_g=norm2_g, w_in=w_in, w_out=w_out, mlp_w1=mlp_w1, mlp_w2=mlp_w2,
        conv_w=conv_w, conv_b=conv_b, conv_ln_g=conv_ln_g, conv_ln_b=conv_ln_b,
        gmlp_norm_g=gmlp_norm_g, gmlp_ws=gmlp_ws, gmlp_bs=gmlp_bs, hgrn_norm_g=hgrn_norm_g,
        s5_a_re=s5_a_re, s5_a_im=s5_a_im, s5_log_dt=s5_log_dt, s5_b_re=s5_b_re, s5_b_im=s5_b_im,
        s5_c_re=s5_c_re, s5_c_im=s5_c_im, s5_d=s5_d, s5_glu_w=s5_glu_w, s5_glu_b=s5_glu_b,
        final_norm_g=final_norm_g), hgrn_lb)
    cond =jnp.zeros((COND_ROWS, D_MODEL), F32).at[0].set(c_ctx.astype(F32)).at[1:1 + n_lat].set(c.astype(F32))
    ada = _ada_call(cond, ada_w.astype(F32), ada_b.astype(F32))
    P['ada4'] = ada.reshape(DEPTH, COND_ROWS, 1, ADA_CHUNKS * D_MODEL)

    zeros_s5 = jnp.zeros((n_ctx, 2, 1, S5_STATE), F32)
    y_ctx, hg_ctx, re_ctx, im_ctx = _run_stream(
        x_prompt.astype(F32).reshape(n_ctx * seq_ctx, D_MODEL), n_ctx, seq_ctx, lambda r: 0,
        [None] * DEPTH, [zeros_s5] * DEPTH, [zeros_s5] * DEPTH, P)

    xs = x_sample.astype(F32) + _grid_pos_embed(seq_lat)[None]
    hg0 = [_hgrn_state_to_kernel(state_hgrn[:, l]) for l in range(DEPTH)]
    re0 = [state_s5_re[:, l].astype(F32).reshape(n_lat, 2, 1, S5_STATE) for l in range(DEPTH)]
    im0 = [state_s5_im[:, l].astype(F32).reshape(n_lat, 2, 1, S5_STATE) for l in range(DEPTH)]
    y_lat, _, _, _ = _run_stream(xs.reshape(n_lat * seq_lat, D_MODEL), n_lat, seq_lat,
                                 lambda r: 1 + r // seq_lat, hg0, re0, im0, P)

    dt = x_prompt.dtype
    new_hg = _hgrn_state_from_kernel(hg_ctx).astype(dt)
    new_re = jnp.stack([s.reshape(n_ctx, 2, S5_GROUPS, S5_P) for s in re_ctx], axis=1).astype(dt)
    new_im = jnp.stack([s.reshape(n_ctx, 2, S5_GROUPS, S5_P) for s in im_ctx], axis=1).astype(dt)
    return (y_ctx.reshape(n_ctx, seq_ctx, D_MODEL).astype(dt),
            y_lat.reshape(n_lat, seq_lat, D_MODEL).astype(x_sample.dtype), new_hg, new_re, new_im)
```

```python
import functools
import math

import numpy as np
import jax
import jax.numpy as jnp
from jax import lax
from jax.experimental import pallas as pl
from jax.experimental.pallas import tpu as pltpu

D_MODEL = 1024
DEPTH = 4
GRID_W = 64
GROUP_W = 256
N_IN_BLOCKS = 10
D_FF = 4 * D_MODEL
EPS = 1e-6
F_MIN = 1e-30
ADA_CHUNKS = 6
CONV_K = 31
CONV_PAD = CONV_K // 2
GM_CHUNK = 128
GM_HEADS = 4
HG_HEADS = 4
HG_DK = 64
HG_CHUNK = 64
HG_TILE_LEVELS = 3
S5_IN = 16
S5_GROUPS = 16
S5_P = 64
S5_STATE = S5_GROUPS * S5_P
POS_BASE = 10000.0

SUBLANE = 8
LANE = 128
COND_ROWS = 8
VMEM_LIMIT = 56 * 1024 * 1024

F32 = jnp.float32
BF16 = jnp.bfloat16


def _sigmoid(x):
    return 1.0 / (1.0 + jnp.exp(-x))


def _silu(x):
    return x * _sigmoid(x)


def _params(n_parallel=1):
    return pltpu.CompilerParams(dimension_semantics=("arbitrary",) * n_parallel,
                                vmem_limit_bytes=VMEM_LIMIT)


def _split2(x):
    hi = x.astype(BF16)
    lo = (x - hi.astype(F32)).astype(BF16)
    return hi, lo


def _dot(a, b):
    return jnp.dot(a, b, preferred_element_type=F32)


def _dot_nt(a, b):
    return lax.dot_general(a, b, (((1,), (1,)), ((), ())), preferred_element_type=F32)


def _dot_tn(a, b):
    return lax.dot_general(a, b, (((0,), (0,)), ((), ())), preferred_element_type=F32)


ADA_BLOCK = 1536


def _ada_kernel(cond_ref, w_ref, b_ref, o_ref):
    s = _silu(cond_ref[...])
    o_ref[...] = _dot(s.astype(BF16), w_ref[...].astype(BF16)) + b_ref[...]


def _ada_call(cond, ada_w, ada_b):
    n_out = ADA_CHUNKS * D_MODEL
    return pl.pallas_call(
        _ada_kernel,
        grid=(DEPTH, n_out // ADA_BLOCK),
        in_specs=[pl.BlockSpec((COND_ROWS, D_MODEL), lambda l, j: (0, 0)),
                  pl.BlockSpec((None, D_MODEL, ADA_BLOCK), lambda l, j: (l, 0, j)),
                  pl.BlockSpec((None, 1, ADA_BLOCK), lambda l, j: (l, 0, j))],
        out_specs=pl.BlockSpec((None, COND_ROWS, ADA_BLOCK), lambda l, j: (l, 0, j)),
        out_shape=jax.ShapeDtypeStruct((DEPTH, COND_ROWS, n_out), F32),
        compiler_params=_params(2),
        name="ada",
    )(cond, ada_w, ada_b.reshape(DEPTH, 1, n_out))


IN_TILE = 512
Z_MAIN_W = (N_IN_BLOCKS - 1) * GROUP_W


def _inproj_kernel(x_ref, ada_ref, g_ref, w_ref, z_ref, zs_ref, zsb_ref, half_ref):
    x = x_ref[...]
    ada = ada_ref[...]
    sh1 = ada[:, 0:D_MODEL]
    sc1 = ada[:, D_MODEL:2 * D_MODEL]
    ms = jnp.mean(x * x, axis=-1, keepdims=True)
    h = x * lax.rsqrt(ms + EPS) * (g_ref[...] * (1.0 + sc1)) + sh1
    z = _dot(h.astype(BF16), w_ref[...])
    z_ref[...] = z[:, 0:Z_MAIN_W]
    zs_ref[...] = z[:, Z_MAIN_W:]
    for h in range(GROUP_W // LANE):
        half_ref[h] = z[:, Z_MAIN_W + h * LANE:Z_MAIN_W + (h + 1) * LANE]
    for j in range(S5_BLOCK):
        for h in range(GROUP_W // LANE):
            c0 = _s5_col(h, j)
            zsb_ref[:, c0:c0 + LANE] = (
                half_ref[h, pl.ds(j, IN_TILE // S5_BLOCK, stride=S5_BLOCK), :].astype(BF16))


def _inproj_call(x, ada4, g1, w_in, layer, cond_row):
    n_tok = x.shape[0]
    zw = N_IN_BLOCKS * GROUP_W
    tile = lambda w: pl.BlockSpec((IN_TILE, w), lambda i: (i, 0))
    return pl.pallas_call(
        _inproj_kernel,
        grid=(n_tok // IN_TILE,),
        in_specs=[pl.BlockSpec((IN_TILE, D_MODEL), lambda i: (i, 0)),
                  pl.BlockSpec((None, None, 1, ADA_CHUNKS * D_MODEL),
                               lambda i: (layer, cond_row(i * IN_TILE), 0, 0)),
                  pl.BlockSpec((None, 1, D_MODEL), lambda i: (layer, 0, 0)),
                  pl.BlockSpec((None, D_MODEL, zw), lambda i: (layer, 0, 0))],
        out_specs=[tile(Z_MAIN_W), tile(GROUP_W),
                   pl.BlockSpec((IN_TILE // S5_BLOCK, S5_ROW_W), lambda i: (i, 0))],
        out_shape=[jax.ShapeDtypeStruct((n_tok, Z_MAIN_W), F32),
                   jax.ShapeDtypeStruct((n_tok, GROUP_W), F32),
                   jax.ShapeDtypeStruct((n_tok // S5_BLOCK, S5_ROW_W), BF16)],
        scratch_shapes=[pltpu.VMEM((GROUP_W // LANE, IN_TILE, LANE), F32)],
        compiler_params=_params(1),
        name="inproj",
    )(x, ada4, g1, w_in)


CONV_TILE = 64
CONV_HALO = 16
CONV_FILL = 128
CONV_NORM_TILE = 256


def _conv_kernel(z_ref, w_ref, b_ref, lng_ref, lnb_ref, o_ref, pad_ref, *, seq):
    tail = seq + 2 * CONV_HALO - (seq + SUBLANE)
    for r in range(SUBLANE):
        pad_ref[r, 0:CONV_HALO, :] = jnp.zeros((CONV_HALO, GROUP_W), F32)
        pad_ref[r, seq + SUBLANE:seq + 2 * CONV_HALO, :] = jnp.zeros((tail, GROUP_W), F32)
    for t0 in range(0, seq, CONV_FILL):
        u = z_ref[t0:t0 + CONV_FILL, 0:GROUP_W] * _sigmoid(z_ref[t0:t0 + CONV_FILL, GROUP_W:2 * GROUP_W])
        for r in range(SUBLANE):
            pad_ref[r, CONV_HALO - r + t0:CONV_HALO - r + t0 + CONV_FILL, :] = u
    first = CONV_HALO - CONV_PAD

    def tile(i, carry):
        r0 = pl.multiple_of(i * CONV_TILE, CONV_TILE)
        acc = jnp.zeros((CONV_TILE, GROUP_W), F32)
        for r in range(SUBLANE):
            taps = [k for k in range(CONV_K) if (first + k) % SUBLANE == r]
            lo = first + taps[0] - r
            hi = first + taps[-1] - r
            win = pad_ref[r, pl.ds(r0 + lo, CONV_TILE + hi - lo), :]
            for k in taps:
                off = first + k - r - lo
                wk = jnp.concatenate([w_ref[k]] * (CONV_TILE // SUBLANE), axis=0)
                acc = acc + wk * win[off:off + CONV_TILE, :]
        o_ref[pl.ds(r0, CONV_TILE), :] = acc + b_ref[...]
        return carry

    lax.fori_loop(0, seq // CONV_TILE, tile, 0)

    def norm(i, carry):
        rows = pl.ds(pl.multiple_of(i * CONV_NORM_TILE, CONV_NORM_TILE), CONV_NORM_TILE)
        c = o_ref[rows, :]
        mu = jnp.mean(c, axis=-1, keepdims=True)
        cc = c - mu
        var = jnp.mean(cc * cc, axis=-1, keepdims=True)
        y = cc * lax.rsqrt(var + EPS) * lng_ref[...] + lnb_ref[...]
        o_ref[rows, :] = _silu(y)
        return carry

    lax.fori_loop(0, seq // CONV_NORM_TILE, norm, 0)


def _conv_call(z, conv_w, conv_b, ln_g, ln_b, layer, n_seq, seq):
    vec = pl.BlockSpec((None, 1, GROUP_W), lambda s: (layer, 0, 0))
    return pl.pallas_call(
        functools.partial(_conv_kernel, seq=seq),
        grid=(n_seq,),
        in_specs=[pl.BlockSpec((seq, 2 * GROUP_W), lambda s: (s, 0)),
                  pl.BlockSpec((None, CONV_K, SUBLANE, GROUP_W), lambda s: (layer, 0, 0, 0)),
                  vec, vec, vec],
        out_specs=pl.BlockSpec((seq, GROUP_W), lambda s: (s, 0)),
        out_shape=jax.ShapeDtypeStruct((n_seq * seq, GROUP_W), F32),
        scratch_shapes=[pltpu.VMEM((SUBLANE, seq + 2 * CONV_HALO, GROUP_W), F32)],
        compiler_params=_params(1),
        name="conv",
    )(z, conv_w, conv_b, ln_g, ln_b)


def _head_mask(rows_per_head, cols_per_head, n_rows, n_cols):
    r = lax.broadcasted_iota(jnp.int32, (n_rows, n_cols), 0) // rows_per_head
    c = lax.broadcasted_iota(jnp.int32, (n_rows, n_cols), 1) // cols_per_head
    return r == c


GM_TILE = 1024


def _gmlp_kernel(z_ref, g_ref, ws_ref, bs_ref, o_ref):
    hd = GROUP_W // GM_HEADS
    mask = _head_mask(GM_CHUNK, hd, GM_HEADS * GM_CHUNK, GROUP_W)
    for r0 in range(0, GM_TILE, GM_CHUNK):
        u = z_ref[r0:r0 + GM_CHUNK, 0:GROUP_W]
        v = z_ref[r0:r0 + GM_CHUNK, GROUP_W:2 * GROUP_W]
        vn = v * lax.rsqrt(jnp.mean(v * v, axis=-1, keepdims=True) + EPS) * g_ref[...]
        stack = jnp.where(mask, jnp.concatenate([vn] * GM_HEADS, axis=0), 0.0).astype(BF16)
        sv = _dot(ws_ref[...], stack) + bs_ref[...]
        o_ref[r0:r0 + GM_CHUNK, :] = u * sv


def _gmlp_call(z, norm_g, ws_cat, bs_full, layer):
    n_tok = z.shape[0]
    return pl.pallas_call(
        _gmlp_kernel,
        grid=(n_tok // GM_TILE,),
        in_specs=[pl.BlockSpec((GM_TILE, 2 * GROUP_W), lambda i: (i, 1)),
                  pl.BlockSpec((None, 1, GROUP_W), lambda i: (layer, 0, 0)),
                  pl.BlockSpec((None, GM_CHUNK, GM_HEADS * GM_CHUNK), lambda i: (layer, 0, 0)),
                  pl.BlockSpec((None, GM_CHUNK, GROUP_W), lambda i: (layer, 0, 0))],
        out_specs=pl.BlockSpec((GM_TILE, GROUP_W), lambda i: (i, 0)),
        out_shape=jax.ShapeDtypeStruct((n_tok, GROUP_W), F32),
        compiler_params=_params(1),
        name="gmlp",
    )(z, norm_g, ws_cat, bs_full)


def _hgrn_block_masks():
    t = np.arange(HG_CHUNK)
    blk = np.zeros((HG_TILE_LEVELS, HG_CHUNK, HG_CHUNK), np.float32)
    for lvl in range(HG_TILE_LEVELS):
        b = t >> (lvl + 1)
        blk[lvl] = (b[:, None] == b[None, :])
    return np.tile(blk, (1, 1, HG_HEADS))


_HG_BMASK = _hgrn_block_masks()
HG_NORM_TILE = 256


def _hgrn_kernel(zq_ref, zi_ref, zg_ref, zff_ref, zfb_ref, lb_ref, ng_ref, bmask_ref, *rest,
                 seq, n_seqs, has_state, state_layer):
    s0_ref = rest[0] if has_state else None
    o_ref, s_out_ref, ob_ref, st_ref = rest[-4:]
    n = HG_CHUNK
    n_chunks = seq // n
    hmask = _head_mask(HG_DK, HG_DK, GROUP_W, GROUP_W)
    hmask_bf = jnp.where(hmask, 1.0, 0.0).astype(BF16)
    row = lax.broadcasted_iota(jnp.int32, (n, GROUP_W), 0)
    n_tiles = n // SUBLANE
    col_tile = (lax.broadcasted_iota(jnp.int32, (SUBLANE, GROUP_W), 1) % n) // SUBLANE
    sub3 = lax.broadcasted_iota(jnp.int32, (n_tiles, SUBLANE, GROUP_W), 1)

    def block_diag(a):
        return jnp.where(hmask, jnp.concatenate([a] * HG_HEADS, axis=0), 0.0).astype(BF16)

    def spread_rows(a, first, period):
        return jnp.concatenate([jnp.broadcast_to(a[r:r + 1, :], (period, GROUP_W))
                                for r in range(first, n, period)], axis=0)

    def boundary(cum, lvl, direction):
        m = 1 << lvl
        at = m - 1 if direction == 0 else m
        if 2 * m >= SUBLANE:
            return spread_rows(cum, at, 2 * m)
        lo = spread_rows(cum, at, SUBLANE)
        hi = spread_rows(cum, at + 2 * m, SUBLANE)
        return jnp.where((row & (2 * m)) == 0, lo, hi)

    def chunk_step(s, direction, c, zf_ref, dst_ref):
        lb = lb_ref[direction:direction + 1, :]
        end_row = n - 1 if direction == 0 else 0
        pos = row if direction == 0 else n - 1 - row
        rows = pl.ds(pl.multiple_of(s * seq + c * n, n), n)
        q = _silu(zq_ref[rows, :])
        v = zi_ref[rows, :]
        f = lb + (1.0 - lb) * _sigmoid(zf_ref[rows, :])
        k = 1.0 - f
        f = jnp.maximum(f, F_MIN)
        cum3 = jnp.log2(f).reshape(n_tiles, SUBLANE, GROUP_W)
        for d in (1, 2, 4):
            shifted = pltpu.roll(cum3, d if direction == 0 else SUBLANE - d, 1)
            live = sub3 >= d if direction == 0 else sub3 < SUBLANE - d
            cum3 = cum3 + jnp.where(live, shifted, 0.0)
        edge = SUBLANE - 1 if direction == 0 else 0
        order = range(n_tiles) if direction == 0 else range(n_tiles - 1, -1, -1)
        tiles, run = [None] * n_tiles, None
        for t in order:
            tiles[t] = cum3[t] if run is None else cum3[t] + run
            run = jnp.broadcast_to(tiles[t][edge:edge + 1, :], (SUBLANE, GROUP_W))
        cum = jnp.concatenate(tiles, axis=0)
        total = cum[end_row:end_row + 1]
        w_cum = jnp.exp2(cum)
        w_rem = jnp.exp2(total - cum)
        w_end = jnp.exp2(total)
        st = st_ref[s, direction]
        inter = _dot_nt((q * w_cum).astype(BF16), st.astype(BF16))

        tile_end = spread_rows(cum, edge, SUBLANE)
        k_far = k * jnp.exp2(tile_end - cum)
        pairs = [(j, i) for j in range(n_tiles)
                 for i in (range(j + 1, n_tiles) if direction == 0 else range(j))]
        tile = lambda a, i: a[i * SUBLANE:(i + 1) * SUBLANE, :]
        q_far = jnp.concatenate([tile(q, i) * jnp.exp2(tile(cum, i) - tile(tile_end, j)) for j, i in pairs], axis=0)
        far = _dot_nt(q_far.astype(BF16), block_diag(k_far))
        acc = [jnp.zeros((SUBLANE, GROUP_W), F32)] * n_tiles
        for p, (j, i) in enumerate(pairs):
            acc[i] = acc[i] + jnp.where(col_tile == j, tile(far, p), 0.0)
        scores = jnp.concatenate(acc, axis=0)

        for lvl in range(HG_TILE_LEVELS):
            upper = (pos & (1 << lvl)) != 0
            if lvl == 0:
                wl = jnp.where(upper, f, 1.0)
            else:
                ref = boundary(cum, lvl, direction)
                wl = jnp.exp2(jnp.where(upper, cum - ref, ref - cum))
            ql = jnp.where(upper, q * wl, 0.0).astype(BF16)
            kl = jnp.where(upper, 0.0, k * wl)
            scores = scores + bmask_ref[lvl] * _dot_nt(ql, block_diag(kl))
        diag = _dot((q * k).astype(BF16), hmask_bf)
        intra = _dot(scores.astype(BF16), block_diag(v)) + diag * v
        upd = _dot_tn(v.astype(BF16), (k * w_rem).astype(BF16))
        st_ref[s, direction] = st * w_end + jnp.where(hmask, upd, 0.0)
        dst_ref[rows, :] = inter + intra

    st_ref[...] = s0_ref[...] if has_state else jnp.zeros(st_ref.shape, F32)

    def both(i, carry):
        for s in range(n_seqs):
            chunk_step(s, 0, i, zff_ref, o_ref)
            chunk_step(s, 1, n_chunks - 1 - i, zfb_ref, ob_ref)
        return carry

    lax.fori_loop(0, n_chunks, both, 0)

    pick = (lax.broadcasted_iota(jnp.int32, (GROUP_W, HG_DK), 0) % HG_DK
            == lax.broadcasted_iota(jnp.int32, (GROUP_W, HG_DK), 1))
    pick_bf = jnp.where(pick, 1.0, 0.0).astype(BF16)
    for s in range(n_seqs):
        for direction in range(2):
            hi, lo = _split2(st_ref[s, direction])
            final = _dot_tn(hi, pick_bf) + _dot_tn(lo, pick_bf)
            if state_layer is None:
                s_out_ref[s, direction] = final
            else:
                s_out_ref[s, state_layer, direction] = final
    if state_layer is not None:
        for other in range(DEPTH):
            if other != state_layer:
                s_out_ref[:, other] = jnp.zeros((n_seqs, 2, GROUP_W, HG_DK), F32)

    def finish(i, carry):
        rows = pl.ds(pl.multiple_of(i * HG_NORM_TILE, HG_NORM_TILE), HG_NORM_TILE)
        o = o_ref[rows, :] + ob_ref[rows, :]
        hi, mid = _split2(o * o)
        ms = (_dot(hi, hmask_bf) + _dot(mid, hmask_bf)) * (1.0 / HG_DK)
        y = o * lax.rsqrt(ms + EPS) * ng_ref[...]
        o_ref[rows, :] = y * _silu(zg_ref[rows, :])
        return carry

    lax.fori_loop(0, n_seqs * seq // HG_NORM_TILE, finish, 0)


HG_ROWS_PER_STEP = 1024
HG_MIN_SEQS_PER_STEP = 2


def _hgrn_call(z, lb, norm_g, s0t, states, layer, n_seq, seq):
    per_step = min(n_seq, max(HG_MIN_SEQS_PER_STEP, HG_ROWS_PER_STEP // seq))
    assert n_seq % per_step == 0
    rows = per_step * seq
    mode = pl.Buffered(1) if rows > HG_ROWS_PER_STEP else None

    def zcol(j):
        return pl.BlockSpec((rows, GROUP_W), lambda s: (s, j), pipeline_mode=mode)

    full = lambda shape: pl.BlockSpec(shape, lambda s: (0,) * len(shape))
    has_state = s0t is not None
    state_specs = [pl.BlockSpec((per_step, 2, GROUP_W, GROUP_W), lambda s: (s, 0, 0, 0))] if has_state else []
    state_args = (s0t,) if has_state else ()
    in_specs = [zcol(4), zcol(5), zcol(6), zcol(7), zcol(8),
                pl.BlockSpec((None, 2, GROUP_W), lambda s: (layer, 0, 0)),
                pl.BlockSpec((None, 1, GROUP_W), lambda s: (layer, 0, 0)),
                full((HG_TILE_LEVELS, HG_CHUNK, GROUP_W))] + state_specs
    if states is None:
        aliases, state_layer = {}, layer
        state_out = pl.BlockSpec((per_step, DEPTH, 2, GROUP_W, HG_DK), lambda s: (s, 0, 0, 0, 0))
    else:
        aliases, state_layer = {len(in_specs): 1}, None
        in_specs = in_specs + [pl.BlockSpec(memory_space=pl.ANY)]
        state_args = state_args + (states,)
        state_out = pl.BlockSpec((per_step, None, 2, GROUP_W, HG_DK), lambda s: (s, layer, 0, 0, 0))
    return pl.pallas_call(
        functools.partial(_hgrn_kernel, seq=seq, n_seqs=per_step, has_state=has_state, state_layer=state_layer),
        grid=(n_seq // per_step,),
        in_specs=in_specs,
        out_specs=[pl.BlockSpec((rows, GROUP_W), lambda s: (s, 0)), state_out],
        out_shape=[jax.ShapeDtypeStruct((n_seq * seq, GROUP_W), F32),
                   jax.ShapeDtypeStruct((n_seq, DEPTH, 2, GROUP_W, HG_DK), F32)],
        scratch_shapes=[pltpu.VMEM((rows, GROUP_W), F32), pltpu.VMEM((per_step, 2, GROUP_W, GROUP_W), F32)],
        input_output_aliases=aliases,
        compiler_params=_params(1),
        name="hgrn",
    )(z, z, z, z, z, lb, norm_g, jnp.asarray(_HG_BMASK), *state_args)


def _hgrn_state_to_kernel(s):
    st = jnp.swapaxes(s.astype(F32), -1, -2)
    eye = jnp.eye(HG_HEADS, dtype=F32)
    full = jnp.einsum('ndhvk,hg->ndhvgk', st, eye)
    return full.reshape(s.shape[0], 2, GROUP_W, GROUP_W)


def _hgrn_state_from_kernel(st):
    return st.reshape(st.shape[:3] + (HG_HEADS, HG_DK, HG_DK))


S5_BLOCK = 8
S5_ROW_W = S5_BLOCK * GROUP_W
S5_HALVES = GROUP_W // LANE
S5_HALF_W = S5_BLOCK * LANE
S5_GROUPS_PER_HALF = S5_GROUPS // S5_HALVES
S5_PARTS = 4
S5_XW = S5_PARTS * S5_STATE
S5_PART_HALF = S5_STATE // S5_HALVES
S5_STEPS = (1, 2, 4)
S5_TABLES = 2 * (len(S5_STEPS) + 1)


def _s5_col(half, token):
    return half * S5_HALF_W + token * LANE


def _s5_in_kernel(u_ref, w_ref, x_ref):
    x_ref[...] = _dot(u_ref[...], w_ref[...])


def _s5_in_call(ub, wx, layer):
    n_blk = ub.shape[0]
    return pl.pallas_call(
        _s5_in_kernel,
        grid=(S5_HALVES, S5_PARTS),
        in_specs=[pl.BlockSpec((n_blk, S5_HALF_W), lambda h, p: (0, h)),
                  pl.BlockSpec((None, None, S5_HALF_W, S5_PART_HALF), lambda h, p: (layer, h, 0, p))],
        out_specs=pl.BlockSpec((n_blk, S5_PART_HALF), lambda h, p: (0, p * S5_HALVES + h)),
        out_shape=jax.ShapeDtypeStruct((n_blk, S5_XW), F32),
        compiler_params=_params(2),
        name="s5_in",
    )(ub, wx)


def _s5_scan_kernel(x_ref, tab_ref, s0re_ref, s0im_ref, h_ref, hre_ref, him_ref, *, n_blk):
    n_pairs = n_blk // (2 * SUBLANE)
    row = lax.broadcasted_iota(jnp.int32, (SUBLANE, S5_STATE), 0)
    n_steps = len(S5_STEPS)

    for direction in range(2):
        c_re = 2 * direction * S5_STATE
        c_im = c_re + S5_STATE
        edge = 0 if direction == 0 else SUBLANE - 1
        last = SUBLANE - 1 - edge
        unit = 0 if direction == 0 else SUBLANE - 1

        def tile(rows, carry):
            hc_r, hc_i, px_r, px_i = carry
            xr = x_ref[rows, c_re:c_re + S5_STATE]
            xi = x_ref[rows, c_im:c_im + S5_STATE]
            shift = 1 if direction == 0 else SUBLANE - 1
            hr = jnp.where(row == edge, px_r, pltpu.roll(xr, shift, 0))
            hi = jnp.where(row == edge, px_i, pltpu.roll(xi, shift, 0))
            for s, step in enumerate(S5_STEPS):
                sh = step if direction == 0 else SUBLANE - step
                rr = pltpu.roll(hr, sh, 0)
                ri = pltpu.roll(hi, sh, 0)
                ar = tab_ref[direction, 2 * s]
                ai = tab_ref[direction, 2 * s + 1]
                hr, hi = hr + ar * rr - ai * ri, hi + ar * ri + ai * rr
            pr = tab_ref[direction, 2 * n_steps]
            pi = tab_ref[direction, 2 * n_steps + 1]
            hr, hi = hr + pr * hc_r - pi * hc_i, hi + pr * hc_i + pi * hc_r
            bc = lambda a: jnp.broadcast_to(a[last:last + 1, :], (SUBLANE, S5_STATE))
            return hr, hi, (bc(hr), bc(hi), bc(xr), bc(xi))

        def pair(j, carry):
            jj = j if direction == 0 else n_pairs - 1 - j
            base = pl.multiple_of(jj * 2 * SUBLANE, 2 * SUBLANE)
            offs = (0, SUBLANE) if direction == 0 else (SUBLANE, 0)
            out = {}
            for off in offs:
                hr, hi, carry = tile(pl.ds(base + off, SUBLANE), carry)
                out[off] = (hr, hi)
            both = pl.ds(base, 2 * SUBLANE)
            h_ref[both, c_re:c_re + S5_STATE] = jnp.concatenate([out[0][0], out[SUBLANE][0]], 0).astype(BF16)
            h_ref[both, c_im:c_im + S5_STATE] = jnp.concatenate([out[0][1], out[SUBLANE][1]], 0).astype(BF16)
            return carry

        zero = jnp.zeros((SUBLANE, S5_STATE), F32)
        init = (zero, zero,
                jnp.broadcast_to(s0re_ref[direction], (SUBLANE, S5_STATE)),
                jnp.broadcast_to(s0im_ref[direction], (SUBLANE, S5_STATE)))
        hc_r, hc_i, px_r, px_i = lax.fori_loop(0, n_pairs, pair, init)
        a_r = tab_ref[direction, 2 * n_steps][unit:unit + 1, :]
        a_i = tab_ref[direction, 2 * n_steps + 1][unit:unit + 1, :]
        hre_ref[direction] = a_r * hc_r[0:1, :] - a_i * hc_i[0:1, :] + px_r[0:1, :]
        him_ref[direction] = a_r * hc_i[0:1, :] + a_i * hc_r[0:1, :] + px_i[0:1, :]


def _s5_scan_call(x, tab, s0re, s0im, layer, n_seq, n_blk):
    state = pl.BlockSpec((None, 2, 1, S5_STATE), lambda s: (s, 0, 0, 0))
    return pl.pallas_call(
        functools.partial(_s5_scan_kernel, n_blk=n_blk),
        grid=(n_seq,),
        in_specs=[pl.BlockSpec((n_blk, S5_XW), lambda s: (s, 0)),
                  pl.BlockSpec((None, 2, S5_TABLES, SUBLANE, S5_STATE), lambda s: (layer, 0, 0, 0, 0)),
                  state, state],
        out_specs=[pl.BlockSpec((n_blk, S5_XW), lambda s: (s, 0)), state, state],
        out_shape=[jax.ShapeDtypeStruct((n_seq * n_blk, S5_XW), BF16),
                   jax.ShapeDtypeStruct((n_seq, 2, 1, S5_STATE), F32),
                   jax.ShapeDtypeStruct((n_seq, 2, 1, S5_STATE), F32)],
        compiler_params=_params(1),
        name="s5_scan",
    )(x, tab, s0re, s0im)


def _s5_out_kernel(u_ref, *refs):
    h_refs, (wt_ref, wc_ref, y_ref) = refs[:S5_PARTS], refs[S5_PARTS:]
    y = _dot(u_ref[...], wt_ref[...])
    for p in range(S5_PARTS):
        y = y + _dot(h_refs[p][...], wc_ref[p * S5_PART_HALF:(p + 1) * S5_PART_HALF, :])
    y_ref[...] = y


def _s5_out_call(ub, h, wt, wc, layer):
    n_blk = ub.shape[0]

    def state_part(p):
        return pl.BlockSpec((n_blk, S5_PART_HALF), lambda j: (0, p * S5_HALVES + j))

    return pl.pallas_call(
        _s5_out_kernel,
        grid=(S5_HALVES,),
        in_specs=[pl.BlockSpec((n_blk, S5_HALF_W), lambda j: (0, j))]
                 + [state_part(p) for p in range(S5_PARTS)]
                 + [pl.BlockSpec((None, None, S5_HALF_W, S5_HALF_W), lambda j: (layer, j, 0, 0)),
                    pl.BlockSpec((None, None, S5_PARTS * S5_PART_HALF, S5_HALF_W), lambda j: (layer, j, 0, 0))],
        out_specs=pl.BlockSpec((n_blk, S5_HALF_W), lambda j: (0, j)),
        out_shape=jax.ShapeDtypeStruct((n_blk, S5_ROW_W), F32),
        compiler_params=_params(1),
        name="s5_out",
    )(ub, *([h] * S5_PARTS), wt, wc)


def _s5_mix(ub, consts, s0re, s0im, layer, n_seq, seq):
    wx, wt, wc, tab = consts
    n_blk = seq // S5_BLOCK
    x = _s5_in_call(ub, wx, layer)
    h, h_re, h_im = _s5_scan_call(x, tab, s0re, s0im, layer, n_seq, n_blk)
    y = _s5_out_call(ub, h, wt, wc, layer)
    return y, h_re, h_im


S5_EXPAND_ROWS = 512


def _expand_kernel(c_ref, t_ref, o_ref, keep_ref, *, row_group, col_group):
    rows, cols = o_ref.shape

    @pl.when((pl.program_id(0) == 0) & (pl.program_id(1) == 0))
    def _():
        r = lax.broadcasted_iota(jnp.int32, (rows, cols), 0)
        col = lax.broadcasted_iota(jnp.int32, (rows, cols), 1)
        group = lambda i, period_width: (i % period_width[0]) // period_width[1]
        keep_ref[...] = jnp.where(group(r, row_group) == group(col, col_group), 1.0, 0.0)

    o_ref[...] = (_dot(c_ref[...], t_ref[...]) * keep_ref[...]).astype(BF16)


def _expand_call(compact, spread, row_group, col_group):
    n_ops, n_rows, k = compact.shape
    n_cols = spread.shape[1]
    assert S5_EXPAND_ROWS % row_group[0] == 0
    return pl.pallas_call(
        functools.partial(_expand_kernel, row_group=row_group, col_group=col_group),
        grid=(n_ops, n_rows // S5_EXPAND_ROWS),
        in_specs=[pl.BlockSpec((None, S5_EXPAND_ROWS, k), lambda l, i: (l, i, 0)),
                  pl.BlockSpec((k, n_cols), lambda l, i: (0, 0))],
        out_specs=pl.BlockSpec((None, S5_EXPAND_ROWS, n_cols), lambda l, i: (l, i, 0)),
        out_shape=jax.ShapeDtypeStruct((n_ops, n_rows, n_cols), BF16),
        scratch_shapes=[pltpu.VMEM((S5_EXPAND_ROWS, n_cols), F32)],
        compiler_params=_params(2),
        name="s5_expand",
    )(compact, spread)


def _s5_constants(a_re, a_im, log_dt, b_re, b_im, c_re, c_im):
    n = S5_BLOCK
    a = lax.complex(a_re.astype(F32), a_im.astype(F32))
    dt = jnp.exp(log_dt.astype(F32))[..., None]
    a_bar = jnp.exp(a * dt)
    b_bar = ((a_bar - 1.0) / a)[..., None] * lax.complex(b_re.astype(F32), b_im.astype(F32))
    c = lax.complex(c_re.astype(F32), c_im.astype(F32))
    steps = jnp.arange(n + 1, dtype=F32)[None, None, :, None, None]
    pw = jnp.exp((a * dt)[:, :, None] * steps)
    gh = S5_GROUPS_PER_HALF
    kron = lambda *m: functools.reduce(np.kron, m)
    spread_state = jnp.asarray(kron(np.eye(S5_PARTS), np.ones((1, gh)), np.eye(S5_P)), BF16)
    spread_token = jnp.asarray(kron(np.eye(n), np.ones((1, gh)), np.eye(S5_IN)), BF16)
    token_group = (LANE, S5_IN)
    state_group = (S5_PART_HALF, S5_P)
    halves = lambda a: a.reshape(a.shape[:3] + (S5_HALVES, gh) + a.shape[4:])

    w_f = pw[:, 0, ::-1][:, 1:, :, :, None] * b_bar[:, 0, None]
    w_b = pw[:, 1, :n, :, :, None] * b_bar[:, 1, None]
    w4 = jnp.stack([jnp.real(w_f), jnp.imag(w_f), jnp.real(w_b), jnp.imag(w_b)], axis=1)
    wx_c = jnp.transpose(halves(w4), (0, 3, 2, 4, 6, 1, 5))
    wx_c = wx_c.reshape(DEPTH * S5_HALVES, S5_HALF_W, S5_PARTS * S5_P)
    wx = _expand_call(wx_c.astype(BF16), spread_state, token_group, state_group)
    wx = wx.reshape(DEPTH, S5_HALVES, S5_HALF_W, S5_PARTS * S5_PART_HALF)

    o_f = c[:, 0, None] * pw[:, 0, 1:, :, None, :]
    o_b = c[:, 1, None] * pw[:, 1, ::-1][:, :n, :, None, :]
    o4 = jnp.stack([jnp.real(o_f), -jnp.imag(o_f), jnp.real(o_b), -jnp.imag(o_b)], axis=1)
    wc_c = jnp.transpose(halves(o4), (0, 3, 1, 4, 6, 2, 5))
    wc_c = wc_c.reshape(DEPTH * S5_HALVES, S5_PARTS * S5_PART_HALF, n * S5_IN)
    wc = _expand_call(wc_c.astype(BF16), spread_token, state_group, token_group)
    wc = wc.reshape(DEPTH, S5_HALVES, S5_PARTS * S5_PART_HALF, S5_HALF_W)

    k_f = jnp.real(jnp.einsum('lgcp,ltgp,lgpd->lgdtc', c[:, 0], pw[:, 0, :n], b_bar[:, 0]))
    k_b = jnp.real(jnp.einsum('lgcp,ltgp,lgpd->lgdtc', c[:, 1], pw[:, 1, :n], b_bar[:, 1]))
    tt, jj, ii = np.meshgrid(np.arange(n), np.arange(n), np.arange(n), indexing='ij')
    place = lambda hit: jnp.asarray(np.kron(hit.reshape(n, n * n).astype(np.float32), np.eye(S5_IN)), F32)
    flat = lambda k: k.reshape(DEPTH * S5_GROUPS * S5_IN, n * S5_IN)
    t_all = (jnp.dot(flat(k_f), place(ii - jj == tt), precision=lax.Precision.HIGHEST)
             + jnp.dot(flat(k_b), place(jj - ii == tt), precision=lax.Precision.HIGHEST))
    wt_c = t_all.reshape(DEPTH, S5_HALVES, gh, S5_IN, n, n * S5_IN)
    wt_c = jnp.transpose(wt_c, (0, 1, 4, 2, 3, 5)).reshape(DEPTH * S5_HALVES, S5_HALF_W, n * S5_IN)
    wt = _expand_call(wt_c.astype(BF16), spread_token, token_group, token_group)
    wt = wt.reshape(DEPTH, S5_HALVES, S5_HALF_W, S5_HALF_W)

    block_steps = n * jnp.arange(1, SUBLANE + 1, dtype=F32)[None, None, :, None]
    pows_all = jnp.exp((a * dt).reshape(DEPTH, 2, 1, S5_STATE) * block_steps)
    pows = [pows_all[:, :, i] for i in range(SUBLANE)]
    row = jnp.arange(SUBLANE)
    tabs = []
    for direction in range(2):
        t = []
        for step in S5_STEPS:
            live = (row >= step) if direction == 0 else (row < SUBLANE - step)
            t.append(jnp.where(live[None, :, None], pows[step - 1][:, direction, None, :], 0.0))
        order = range(SUBLANE) if direction == 0 else range(SUBLANE - 1, -1, -1)
        t.append(jnp.stack([pows[i][:, direction] for i in order], axis=1))
        planes = []
        for x in t:
            planes += [jnp.real(x), jnp.imag(x)]
        tabs.append(jnp.stack(planes, axis=1))
    tab = jnp.stack(tabs, axis=1).astype(F32)
    return wx, wt, wc, tab


OUT_TILE = 256
FF_BLOCK = 1024


def _outproj_kernel(x_ref, oa_ref, ob_ref, oc_ref, ys_ref, zs_ref, sd_ref, gw_ref, gb_ref,
                    ada_ref, g2_ref, wo_ref, w1_ref, w2_ref, gf_ref, *rest, final):
    out_refs, tok_ref = rest[:-1], rest[-1]
    ada = ada_ref[...]
    chunk = lambda j: ada[:, j * D_MODEL:(j + 1) * D_MODEL]
    gate1, sh2, sc2, gate2 = chunk(2), chunk(3), chunk(4), chunk(5)
    for j in range(S5_BLOCK):
        for h in range(GROUP_W // LANE):
            c0 = _s5_col(h, j)
            tok_ref[h, pl.ds(j, OUT_TILE // S5_BLOCK, stride=S5_BLOCK), :] = ys_ref[:, c0:c0 + LANE]
    ssm = jnp.concatenate([tok_ref[h] for h in range(GROUP_W // LANE)], axis=-1)
    ys = jax.nn.gelu(ssm + sd_ref[...] * zs_ref[...])
    od = ys * _sigmoid(_dot(ys.astype(BF16), gw_ref[...]) + gb_ref[...])
    mixed = jnp.concatenate([oa_ref[...], ob_ref[...], oc_ref[...], od], axis=-1).astype(BF16)
    x = x_ref[...] + gate1 * _dot(mixed, wo_ref[...])
    ms = jnp.mean(x * x, axis=-1, keepdims=True)
    h = (x * lax.rsqrt(ms + EPS) * (g2_ref[...] * (1.0 + sc2)) + sh2).astype(BF16)
    acc = jnp.zeros((OUT_TILE, D_MODEL), F32)
    for j in range(D_FF // FF_BLOCK):
        cols = slice(j * FF_BLOCK, (j + 1) * FF_BLOCK)
        a = jnp.maximum(_dot(h, w1_ref[:, cols]), 0.0)
        acc = acc + _dot((a * a).astype(BF16), w2_ref[cols, :])
    x = x + gate2 * acc
    out_refs[0][...] = x
    if final:
        ms = jnp.mean(x * x, axis=-1, keepdims=True)
        out_refs[1][...] = x * lax.rsqrt(ms + EPS) * gf_ref[...]


def _outproj_call(x, mix, s5_par, ada4, g2, w_out, w1, w2, gf, layer, cond_row, final):
    n_tok = x.shape[0]
    tok = pl.BlockSpec((OUT_TILE, D_MODEL), lambda i: (i, 0))
    grp = pl.BlockSpec((OUT_TILE, GROUP_W), lambda i: (i, 0))
    lay = lambda shape: pl.BlockSpec((None,) + shape, lambda i: (layer,) + (0,) * len(shape))
    n_out = 2 if final else 1
    res = pl.pallas_call(
        functools.partial(_outproj_kernel, final=final),
        grid=(n_tok // OUT_TILE,),
        in_specs=[tok, grp, grp, grp,
                  pl.BlockSpec((OUT_TILE // S5_BLOCK, S5_ROW_W), lambda i: (i, 0)), grp,
                  lay((1, GROUP_W)), lay((GROUP_W, GROUP_W)), lay((1, GROUP_W)),
                  pl.BlockSpec((None, None, 1, ADA_CHUNKS * D_MODEL),
                               lambda i: (layer, cond_row(i * OUT_TILE), 0, 0)),
                  lay((1, D_MODEL)), lay((D_MODEL, D_MODEL)), lay((D_MODEL, D_FF)), lay((D_FF, D_MODEL)),
                  pl.BlockSpec((1, D_MODEL), lambda i: (0, 0))],
        out_specs=[tok] * n_out,
        out_shape=[jax.ShapeDtypeStruct((n_tok, D_MODEL), F32)] * n_out,
        scratch_shapes=[pltpu.VMEM((GROUP_W // LANE, OUT_TILE, LANE), F32)],
        compiler_params=_params(1),
        name="outproj",
    )(x, *mix, *s5_par, ada4, g2, w_out, w1, w2, gf)
    return res


def _grid_pos_embed(n_tokens):
    rows = n_tokens // GRID_W
    r, col = jnp.meshgrid(jnp.arange(rows, dtype=F32), jnp.arange(GRID_W, dtype=F32), indexing='ij')
    r = r.reshape(-1)
    col = col.reshape(-1)
    quarter = D_MODEL // 4
    freq = jnp.exp(-math.log(POS_BASE) * jnp.arange(quarter, dtype=F32) / quarter)
    ar = r[:, None] * freq
    ac = col[:, None] * freq
    return jnp.concatenate([jnp.sin(ar), jnp.cos(ar), jnp.sin(ac), jnp.cos(ac)], axis=-1)


def _layer(x, l, n_seq, seq, cond_row, hg_s0, hg_states, s5_s0re, s5_s0im, P, final):
    z, zs, zs_blk = _inproj_call(x, P['ada4'], P['norm1_g'], P['w_in'], l, cond_row)
    o_conv = _conv_call(z, P['conv_w'], P['conv_b'], P['conv_ln_g'], P['conv_ln_b'], l, n_seq, seq)
    o_gm = _gmlp_call(z, P['gmlp_norm_g'], P['gmlp_ws'], P['gmlp_bs'], l)
    o_hg, hg_end = _hgrn_call(z, P['hgrn_lb'], P['hgrn_norm_g'], hg_s0, hg_states, l, n_seq, seq)
    y_s5, h_re, h_im = _s5_mix(zs_blk, P['s5'], s5_s0re, s5_s0im, l, n_seq, seq)
    res = _outproj_call(x, (o_conv, o_gm, o_hg, y_s5, zs), (P['s5_d'], P['s5_glu_w'], P['s5_glu_b']),
                        P['ada4'], P['norm2_g'], P['w_out'],
                        P['mlp_w1'], P['mlp_w2'], P['final_norm_g'], l, cond_row, final)
    return res, hg_end, h_re, h_im


def _prepare(w, hgrn_lb):
    vec = lambda a: a.astype(F32).reshape(DEPTH, 1, -1)
    hd = GROUP_W // GM_HEADS
    return {
        'norm1_g': vec(w['norm1_g']), 'norm2_g': vec(w['norm2_g']),
        'w_in': w['w_in'].astype(BF16), 'w_out': w['w_out'].astype(BF16),
        'mlp_w1': w['mlp_w1'].astype(BF16), 'mlp_w2': w['mlp_w2'].astype(BF16),
        'conv_w': jnp.broadcast_to(w['conv_w'].astype(F32)[:, :, None, :], (DEPTH, CONV_K, SUBLANE, GROUP_W)),
        'conv_b': vec(w['conv_b']), 'conv_ln_g': vec(w['conv_ln_g']), 'conv_ln_b': vec(w['conv_ln_b']),
        'gmlp_norm_g': vec(w['gmlp_norm_g']),
        'gmlp_ws': jnp.transpose(w['gmlp_ws'], (0, 2, 1, 3)).reshape(
            DEPTH, GM_CHUNK, GM_HEADS * GM_CHUNK).astype(BF16),
        'gmlp_bs': jnp.repeat(jnp.transpose(w['gmlp_bs'].astype(F32), (0, 2, 1)), hd, axis=2),
        'hgrn_lb': jnp.transpose(hgrn_lb, (1, 0, 2)), 'hgrn_norm_g': vec(w['hgrn_norm_g']),
        's5': _s5_constants(w['s5_a_re'], w['s5_a_im'], w['s5_log_dt'], w['s5_b_re'], w['s5_b_im'],
                            w['s5_c_re'], w['s5_c_im']),
        's5_d': vec(w['s5_d']), 's5_glu_w': w['s5_glu_w'].astype(BF16), 's5_glu_b': vec(w['s5_glu_b']),
        'final_norm_g': w['final_norm_g'].astype(F32).reshape(1, D_MODEL),
    }


def _run_stream(x, n_seq, seq, cond_row, hg_s0, s5_s0re, s5_s0im, P):
    hg_states, s5_re, s5_im = None, [], []
    y = None
    for l in range(DEPTH):
        final = l == DEPTH - 1
        res, hg_states, h_re, h_im = _layer(x, l, n_seq, seq, cond_row, hg_s0[l], hg_states,
                                            s5_s0re[l], s5_s0im[l], P, final)
        x = res[0]
        if final:
            y = res[1]
        s5_re.append(h_re)
        s5_im.append(h_im)
    return y, hg_states, s5_re, s5_im


def kernel(x_prompt, x_sample, state_hgrn, state_s5_re, state_s5_im, c, c_ctx, norm1_g, norm2_g, ada_w, ada_b, w_in, conv_w, conv_b, conv_ln_g, conv_ln_b, gmlp_norm_g, gmlp_ws, gmlp_bs, hgrn_lb_logits, hgrn_norm_g, s5_a_re, s5_a_im, s5_log_dt, s5_b_re, s5_b_im, s5_c_re, s5_c_im, s5_d, s5_glu_w, s5_glu_b, w_out, mlp_w1, mlp_w2, final_norm_g):
    n_ctx, seq_ctx, _ = x_prompt.shape
    n_lat, seq_lat, _ = x_sample.shape
    assert n_lat + 1 <= COND_ROWS
    assert seq_ctx % IN_TILE == 0 or IN_TILE % seq_ctx == 0

    lb_soft = jax.nn.softmax(hgrn_lb_logits.astype(F32), axis=1)
    hgrn_lb = jnp.cumsum(lb_soft, axis=1) - lb_soft[:, :1]
    P = _prepare(dict(
        norm1_g=norm1_g, norm2_g=norm2_g, w_in=w_in, w_out=w_out, mlp_w1=mlp_w1, mlp_w2=mlp_w2,
        conv_w=conv_w, conv_b=conv_b, conv_ln_g=conv_ln_g, conv_ln_b=conv_ln_b,
        gmlp_norm_g=gmlp_norm_g, gmlp_ws=gmlp_ws, gmlp_bs=gmlp_bs, hgrn_norm_g=hgrn_norm_g,
        s5_a_re=s5_a_re, s5_a_im=s5_a_im, s5_log_dt=s5_log_dt, s5_b_re=s5_b_re, s5_b_im=s5_b_im,
        s5_c_re=s5_c_re, s5_c_im=s5_c_im, s5_d=s5_d, s5_glu_w=s5_glu_w, s5_glu_b=s5_glu_b,
        final_norm_g=final_norm_g), hgrn_lb)
    cond =jnp.zeros((COND_ROWS, D_MODEL), F32).at[0].set(c_ctx.astype(F32)).at[1:1 + n_lat].set(c.astype(F32))
    ada = _ada_call(cond, ada_w.astype(F32), ada_b.astype(F32))
    P['ada4'] = ada.reshape(DEPTH, COND_ROWS, 1, ADA_CHUNKS * D_MODEL)

    zeros_s5 = jnp.zeros((n_ctx, 2, 1, S5_STATE), F32)
    y_ctx, hg_ctx, re_ctx, im_ctx = _run_stream(
        x_prompt.astype(F32).reshape(n_ctx * seq_ctx, D_MODEL), n_ctx, seq_ctx, lambda r: 0,
        [None] * DEPTH, [zeros_s5] * DEPTH, [zeros_s5] * DEPTH, P)

    xs = x_sample.astype(F32) + _grid_pos_embed(seq_lat)[None]
    hg0 = [_hgrn_state_to_kernel(state_hgrn[:, l]) for l in range(DEPTH)]
    re0 = [state_s5_re[:, l].astype(F32).reshape(n_lat, 2, 1, S5_STATE) for l in range(DEPTH)]
    im0 = [state_s5_im[:, l].astype(F32).reshape(n_lat, 2, 1, S5_STATE) for l in range(DEPTH)]
    y_lat, _, _, _ = _run_stream(xs.reshape(n_lat * seq_lat, D_MODEL), n_lat, seq_lat,
                                 lambda r: 1 + r // seq_lat, hg0, re0, im0, P)

    dt = x_prompt.dtype
    new_hg = _hgrn_state_from_kernel(hg_ctx).astype(dt)
    new_re = jnp.stack([s.reshape(n_ctx, 2, S5_GROUPS, S5_P) for s in re_ctx], axis=1).astype(dt)
    new_im = jnp.stack([s.reshape(n_ctx, 2, S5_GROUPS, S5_P) for s in im_ctx], axis=1).astype(dt)
    return (y_ctx.reshape(n_ctx, seq_ctx, D_MODEL).astype(dt),
            y_lat.reshape(n_lat, seq_lat, D_MODEL).astype(x_sample.dtype), new_hg, new_re, new_im)
```

```python
import functools
import math

import numpy as np
import jax
import jax.numpy as jnp
from jax import lax
from jax.experimental import pallas as pl
from jax.experimental.pallas import tpu as pltpu

D_MODEL = 1024
DEPTH = 4
GRID_W = 64
GROUP_W = 256
N_IN_BLOCKS = 10
D_FF = 4 * D_MODEL
EPS = 1e-6
F_MIN = 1e-30
ADA_CHUNKS = 6
CONV_K = 31
CONV_PAD = CONV_K // 2
GM_CHUNK = 128
GM_HEADS = 4
HG_HEADS = 4
HG_DK = 64
HG_CHUNK = 64
HG_TILE_LEVELS = 3
S5_IN = 16
S5_GROUPS = 16
S5_P = 64
S5_STATE = S5_GROUPS * S5_P
POS_BASE = 10000.0

SUBLANE = 8
LANE = 128
COND_ROWS = 8
VMEM_LIMIT = 56 * 1024 * 1024

F32 = jnp.float32
BF16 = jnp.bfloat16


def _sigmoid(x):
    return 1.0 / (1.0 + jnp.exp(-x))


def _silu(x):
    return x * _sigmoid(x)


def _params(n_parallel=1):
    return pltpu.CompilerParams(dimension_semantics=("arbitrary",) * n_parallel,
                                vmem_limit_bytes=VMEM_LIMIT)


def _split2(x):
    hi = x.astype(BF16)
    lo = (x - hi.astype(F32)).astype(BF16)
    return hi, lo


def _dot(a, b):
    return jnp.dot(a, b, preferred_element_type=F32)


def _dot_nt(a, b):
    return lax.dot_general(a, b, (((1,), (1,)), ((), ())), preferred_element_type=F32)


def _dot_tn(a, b):
    return lax.dot_general(a, b, (((0,), (0,)), ((), ())), preferred_element_type=F32)


ADA_BLOCK = 1536


def _ada_kernel(cond_ref, w_ref, b_ref, o_ref):
    s = _silu(cond_ref[...])
    o_ref[...] = _dot(s.astype(BF16), w_ref[...].astype(BF16)) + b_ref[...]


def _ada_call(cond, ada_w, ada_b):
    n_out = ADA_CHUNKS * D_MODEL
    return pl.pallas_call(
        _ada_kernel,
        grid=(DEPTH, n_out // ADA_BLOCK),
        in_specs=[pl.BlockSpec((COND_ROWS, D_MODEL), lambda l, j: (0, 0)),
                  pl.BlockSpec((None, D_MODEL, ADA_BLOCK), lambda l, j: (l, 0, j)),
                  pl.BlockSpec((None, 1, ADA_BLOCK), lambda l, j: (l, 0, j))],
        out_specs=pl.BlockSpec((None, COND_ROWS, ADA_BLOCK), lambda l, j: (l, 0, j)),
        out_shape=jax.ShapeDtypeStruct((DEPTH, COND_ROWS, n_out), F32),
        compiler_params=_params(2),
        name="ada",
    )(cond, ada_w, ada_b.reshape(DEPTH, 1, n_out))


IN_TILE = 1024
Z_MAIN_W = (N_IN_BLOCKS - 1) * GROUP_W


def _inproj_kernel(x_ref, ada_ref, g_ref, w_ref, z_ref, zs_ref, zsb_ref, half_ref):
    x = x_ref[...]
    ada = ada_ref[...]
    sh1 = ada[:, 0:D_MODEL]
    sc1 = ada[:, D_MODEL:2 * D_MODEL]
    ms = jnp.mean(x * x, axis=-1, keepdims=True)
    h = x * lax.rsqrt(ms + EPS) * (g_ref[...] * (1.0 + sc1)) + sh1
    z = _dot(h.astype(BF16), w_ref[...])
    z_ref[...] = z[:, 0:Z_MAIN_W]
    zs_ref[...] = z[:, Z_MAIN_W:]
    for h in range(GROUP_W // LANE):
        half_ref[h] = z[:, Z_MAIN_W + h * LANE:Z_MAIN_W + (h + 1) * LANE]
    for j in range(S5_BLOCK):
        for h in range(GROUP_W // LANE):
            c0 = _s5_col(h, j)
            zsb_ref[:, c0:c0 + LANE] = (
                half_ref[h, pl.ds(j, IN_TILE // S5_BLOCK, stride=S5_BLOCK), :].astype(BF16))


def _inproj_call(x, ada4, g1, w_in, layer, cond_row):
    n_tok = x.shape[0]
    zw = N_IN_BLOCKS * GROUP_W
    tile = lambda w: pl.BlockSpec((IN_TILE, w), lambda i: (i, 0))
    return pl.pallas_call(
        _inproj_kernel,
        grid=(n_tok // IN_TILE,),
        in_specs=[pl.BlockSpec((IN_TILE, D_MODEL), lambda i: (i, 0)),
                  pl.BlockSpec((None, None, 1, ADA_CHUNKS * D_MODEL),
                               lambda i: (layer, cond_row(i * IN_TILE), 0, 0)),
                  pl.BlockSpec((None, 1, D_MODEL), lambda i: (layer, 0, 0)),
                  pl.BlockSpec((None, D_MODEL, zw), lambda i: (layer, 0, 0))],
        out_specs=[tile(Z_MAIN_W), tile(GROUP_W),
                   pl.BlockSpec((IN_TILE // S5_BLOCK, S5_ROW_W), lambda i: (i, 0))],
        out_shape=[jax.ShapeDtypeStruct((n_tok, Z_MAIN_W), F32),
                   jax.ShapeDtypeStruct((n_tok, GROUP_W), F32),
                   jax.ShapeDtypeStruct((n_tok // S5_BLOCK, S5_ROW_W), BF16)],
        scratch_shapes=[pltpu.VMEM((GROUP_W // LANE, IN_TILE, LANE), F32)],
        compiler_params=_params(1),
        name="inproj",
    )(x, ada4, g1, w_in)


CONV_TILE = 64
CONV_HALO = 16
CONV_FILL = 128
CONV_NORM_TILE = 256


def _conv_kernel(z_ref, w_ref, b_ref, lng_ref, lnb_ref, o_ref, pad_ref, *, seq):
    tail = seq + 2 * CONV_HALO - (seq + SUBLANE)
    for r in range(SUBLANE):
        pad_ref[r, 0:CONV_HALO, :] = jnp.zeros((CONV_HALO, GROUP_W), F32)
        pad_ref[r, seq + SUBLANE:seq + 2 * CONV_HALO, :] = jnp.zeros((tail, GROUP_W), F32)
    for t0 in range(0, seq, CONV_FILL):
        u = z_ref[t0:t0 + CONV_FILL, 0:GROUP_W] * _sigmoid(z_ref[t0:t0 + CONV_FILL, GROUP_W:2 * GROUP_W])
        for r in range(SUBLANE):
            pad_ref[r, CONV_HALO - r + t0:CONV_HALO - r + t0 + CONV_FILL, :] = u
    first = CONV_HALO - CONV_PAD

    def tile(i, carry):
        r0 = pl.multiple_of(i * CONV_TILE, CONV_TILE)
        acc = jnp.zeros((CONV_TILE, GROUP_W), F32)
        for r in range(SUBLANE):
            taps = [k for k in range(CONV_K) if (first + k) % SUBLANE == r]
            lo = first + taps[0] - r
            hi = first + taps[-1] - r
            win = pad_ref[r, pl.ds(r0 + lo, CONV_TILE + hi - lo), :]
            for k in taps:
                off = first + k - r - lo
                wk = jnp.concatenate([w_ref[k]] * (CONV_TILE // SUBLANE), axis=0)
                acc = acc + wk * win[off:off + CONV_TILE, :]
        o_ref[pl.ds(r0, CONV_TILE), :] = acc + b_ref[...]
        return carry

    lax.fori_loop(0, seq // CONV_TILE, tile, 0)

    def norm(i, carry):
        rows = pl.ds(pl.multiple_of(i * CONV_NORM_TILE, CONV_NORM_TILE), CONV_NORM_TILE)
        c = o_ref[rows, :]
        mu = jnp.mean(c, axis=-1, keepdims=True)
        cc = c - mu
        var = jnp.mean(cc * cc, axis=-1, keepdims=True)
        y = cc * lax.rsqrt(var + EPS) * lng_ref[...] + lnb_ref[...]
        o_ref[rows, :] = _silu(y)
        return carry

    lax.fori_loop(0, seq // CONV_NORM_TILE, norm, 0)


def _conv_call(z, conv_w, conv_b, ln_g, ln_b, layer, n_seq, seq):
    vec = pl.BlockSpec((None, 1, GROUP_W), lambda s: (layer, 0, 0))
    return pl.pallas_call(
        functools.partial(_conv_kernel, seq=seq),
        grid=(n_seq,),
        in_specs=[pl.BlockSpec((seq, 2 * GROUP_W), lambda s: (s, 0)),
                  pl.BlockSpec((None, CONV_K, SUBLANE, GROUP_W), lambda s: (layer, 0, 0, 0)),
                  vec, vec, vec],
        out_specs=pl.BlockSpec((seq, GROUP_W), lambda s: (s, 0)),
        out_shape=jax.ShapeDtypeStruct((n_seq * seq, GROUP_W), F32),
        scratch_shapes=[pltpu.VMEM((SUBLANE, seq + 2 * CONV_HALO, GROUP_W), F32)],
        compiler_params=_params(1),
        name="conv",
    )(z, conv_w, conv_b, ln_g, ln_b)


def _head_mask(rows_per_head, cols_per_head, n_rows, n_cols):
    r = lax.broadcasted_iota(jnp.int32, (n_rows, n_cols), 0) // rows_per_head
    c = lax.broadcasted_iota(jnp.int32, (n_rows, n_cols), 1) // cols_per_head
    return r == c


GM_TILE = 1024


def _gmlp_kernel(z_ref, g_ref, ws_ref, bs_ref, o_ref):
    hd = GROUP_W // GM_HEADS
    mask = _head_mask(GM_CHUNK, hd, GM_HEADS * GM_CHUNK, GROUP_W)
    for r0 in range(0, GM_TILE, GM_CHUNK):
        u = z_ref[r0:r0 + GM_CHUNK, 0:GROUP_W]
        v = z_ref[r0:r0 + GM_CHUNK, GROUP_W:2 * GROUP_W]
        vn = v * lax.rsqrt(jnp.mean(v * v, axis=-1, keepdims=True) + EPS) * g_ref[...]
        stack = jnp.where(mask, jnp.concatenate([vn] * GM_HEADS, axis=0), 0.0).astype(BF16)
        sv = _dot(ws_ref[...], stack) + bs_ref[...]
        o_ref[r0:r0 + GM_CHUNK, :] = u * sv


def _gmlp_call(z, norm_g, ws_cat, bs_full, layer):
    n_tok = z.shape[0]
    return pl.pallas_call(
        _gmlp_kernel,
        grid=(n_tok // GM_TILE,),
        in_specs=[pl.BlockSpec((GM_TILE, 2 * GROUP_W), lambda i: (i, 1)),
                  pl.BlockSpec((None, 1, GROUP_W), lambda i: (layer, 0, 0)),
                  pl.BlockSpec((None, GM_CHUNK, GM_HEADS * GM_CHUNK), lambda i: (layer, 0, 0)),
                  pl.BlockSpec((None, GM_CHUNK, GROUP_W), lambda i: (layer, 0, 0))],
        out_specs=pl.BlockSpec((GM_TILE, GROUP_W), lambda i: (i, 0)),
        out_shape=jax.ShapeDtypeStruct((n_tok, GROUP_W), F32),
        compiler_params=_params(1),
        name="gmlp",
    )(z, norm_g, ws_cat, bs_full)


def _hgrn_block_masks():
    t = np.arange(HG_CHUNK)
    blk = np.zeros((HG_TILE_LEVELS, HG_CHUNK, HG_CHUNK), np.float32)
    for lvl in range(HG_TILE_LEVELS):
        b = t >> (lvl + 1)
        blk[lvl] = (b[:, None] == b[None, :])
    return np.tile(blk, (1, 1, HG_HEADS))


_HG_BMASK = _hgrn_block_masks()
HG_NORM_TILE = 256


def _hgrn_kernel(zq_ref, zi_ref, zg_ref, zff_ref, zfb_ref, lb_ref, ng_ref, bmask_ref, *rest,
                 seq, n_seqs, has_state, state_layer):
    s0_ref = rest[0] if has_state else None
    o_ref, s_out_ref, ob_ref, st_ref = rest[-4:]
    n = HG_CHUNK
    n_chunks = seq // n
    hmask = _head_mask(HG_DK, HG_DK, GROUP_W, GROUP_W)
    hmask_bf = jnp.where(hmask, 1.0, 0.0).astype(BF16)
    row = lax.broadcasted_iota(jnp.int32, (n, GROUP_W), 0)
    n_tiles = n // SUBLANE
    col_tile = (lax.broadcasted_iota(jnp.int32, (SUBLANE, GROUP_W), 1) % n) // SUBLANE
    sub3 = lax.broadcasted_iota(jnp.int32, (n_tiles, SUBLANE, GROUP_W), 1)

    def block_diag(a):
        return jnp.where(hmask, jnp.concatenate([a] * HG_HEADS, axis=0), 0.0).astype(BF16)

    def spread_rows(a, first, period):
        return jnp.concatenate([jnp.broadcast_to(a[r:r + 1, :], (period, GROUP_W))
                                for r in range(first, n, period)], axis=0)

    def boundary(cum, lvl, direction):
        m = 1 << lvl
        at = m - 1 if direction == 0 else m
        if 2 * m >= SUBLANE:
            return spread_rows(cum, at, 2 * m)
        lo = spread_rows(cum, at, SUBLANE)
        hi = spread_rows(cum, at + 2 * m, SUBLANE)
        return jnp.where((row & (2 * m)) == 0, lo, hi)

    def chunk_step(s, direction, c, zf_ref, dst_ref):
        lb = lb_ref[direction:direction + 1, :]
        end_row = n - 1 if direction == 0 else 0
        pos = row if direction == 0 else n - 1 - row
        rows = pl.ds(pl.multiple_of(s * seq + c * n, n), n)
        q = _silu(zq_ref[rows, :])
        v = zi_ref[rows, :]
        f = lb + (1.0 - lb) * _sigmoid(zf_ref[rows, :])
        k = 1.0 - f
        f = jnp.maximum(f, F_MIN)
        cum3 = jnp.log2(f).reshape(n_tiles, SUBLANE, GROUP_W)
        for d in (1, 2, 4):
            shifted = pltpu.roll(cum3, d if direction == 0 else SUBLANE - d, 1)
            live = sub3 >= d if direction == 0 else sub3 < SUBLANE - d
            cum3 = cum3 + jnp.where(live, shifted, 0.0)
        edge = SUBLANE - 1 if direction == 0 else 0
        order = range(n_tiles) if direction == 0 else range(n_tiles - 1, -1, -1)
        tiles, run = [None] * n_tiles, None
        for t in order:
            tiles[t] = cum3[t] if run is None else cum3[t] + run
            run = jnp.broadcast_to(tiles[t][edge:edge + 1, :], (SUBLANE, GROUP_W))
        cum = jnp.concatenate(tiles, axis=0)
        total = cum[end_row:end_row + 1]
        w_cum = jnp.exp2(cum)
        w_rem = jnp.exp2(total - cum)
        w_end = jnp.exp2(total)
        st = st_ref[s, direction]
        inter = _dot_nt((q * w_cum).astype(BF16), st.astype(BF16))

        tile_end = spread_rows(cum, edge, SUBLANE)
        k_far = k * jnp.exp2(tile_end - cum)
        pairs = [(j, i) for j in range(n_tiles)
                 for i in (range(j + 1, n_tiles) if direction == 0 else range(j))]
        tile = lambda a, i: a[i * SUBLANE:(i + 1) * SUBLANE, :]
        q_far = jnp.concatenate([tile(q, i) * jnp.exp2(tile(cum, i) - tile(tile_end, j)) for j, i in pairs], axis=0)
        far = _dot_nt(q_far.astype(BF16), block_diag(k_far))
        acc = [jnp.zeros((SUBLANE, GROUP_W), F32)] * n_tiles
        for p, (j, i) in enumerate(pairs):
            acc[i] = acc[i] + jnp.where(col_tile == j, tile(far, p), 0.0)
        scores = jnp.concatenate(acc, axis=0)

        for lvl in range(HG_TILE_LEVELS):
            upper = (pos & (1 << lvl)) != 0
            if lvl == 0:
                wl = jnp.where(upper, f, 1.0)
            else:
                ref = boundary(cum, lvl, direction)
                wl = jnp.exp2(jnp.where(upper, cum - ref, ref - cum))
            ql = jnp.where(upper, q * wl, 0.0).astype(BF16)
            kl = jnp.where(upper, 0.0, k * wl)
            scores = scores + bmask_ref[lvl] * _dot_nt(ql, block_diag(kl))
        diag = _dot((q * k).astype(BF16), hmask_bf)
        intra = _dot(scores.astype(BF16), block_diag(v)) + diag * v
        upd = _dot_tn(v.astype(BF16), (k * w_rem).astype(BF16))
        st_ref[s, direction] = st * w_end + jnp.where(hmask, upd, 0.0)
        dst_ref[rows, :] = inter + intra

    st_ref[...] = s0_ref[...] if has_state else jnp.zeros(st_ref.shape, F32)

    def both(i, carry):
        for s in range(n_seqs):
            chunk_step(s, 0, i, zff_ref, o_ref)
            chunk_step(s, 1, n_chunks - 1 - i, zfb_ref, ob_ref)
        return carry

    lax.fori_loop(0, n_chunks, both, 0)

    pick = (lax.broadcasted_iota(jnp.int32, (GROUP_W, HG_DK), 0) % HG_DK
            == lax.broadcasted_iota(jnp.int32, (GROUP_W, HG_DK), 1))
    pick_bf = jnp.where(pick, 1.0, 0.0).astype(BF16)
    for s in range(n_seqs):
        for direction in range(2):
            hi, lo = _split2(st_ref[s, direction])
            final = _dot_tn(hi, pick_bf) + _dot_tn(lo, pick_bf)
            if state_layer is None:
                s_out_ref[s, direction] = final
            else:
                s_out_ref[s, state_layer, direction] = final
    if state_layer is not None:
        for other in range(DEPTH):
            if other != state_layer:
                s_out_ref[:, other] = jnp.zeros((n_seqs, 2, GROUP_W, HG_DK), F32)

    def finish(i, carry):
        rows = pl.ds(pl.multiple_of(i * HG_NORM_TILE, HG_NORM_TILE), HG_NORM_TILE)
        o = o_ref[rows, :] + ob_ref[rows, :]
        hi, mid = _split2(o * o)
        ms = (_dot(hi, hmask_bf) + _dot(mid, hmask_bf)) * (1.0 / HG_DK)
        y = o * lax.rsqrt(ms + EPS) * ng_ref[...]
        o_ref[rows, :] = y * _silu(zg_ref[rows, :])
        return carry

    lax.fori_loop(0, n_seqs * seq // HG_NORM_TILE, finish, 0)


HG_ROWS_PER_STEP = 1024
HG_MIN_SEQS_PER_STEP = 2


def _hgrn_call(z, lb, norm_g, s0t, states, layer, n_seq, seq):
    per_step = min(n_seq, max(HG_MIN_SEQS_PER_STEP, HG_ROWS_PER_STEP // seq))
    assert n_seq % per_step == 0
    rows = per_step * seq
    mode = pl.Buffered(1) if rows > HG_ROWS_PER_STEP else None

    def zcol(j):
        return pl.BlockSpec((rows, GROUP_W), lambda s: (s, j), pipeline_mode=mode)

    full = lambda shape: pl.BlockSpec(shape, lambda s: (0,) * len(shape))
    has_state = s0t is not None
    state_specs = [pl.BlockSpec((per_step, 2, GROUP_W, GROUP_W), lambda s: (s, 0, 0, 0))] if has_state else []
    state_args = (s0t,) if has_state else ()
    in_specs = [zcol(4), zcol(5), zcol(6), zcol(7), zcol(8),
                pl.BlockSpec((None, 2, GROUP_W), lambda s: (layer, 0, 0)),
                pl.BlockSpec((None, 1, GROUP_W), lambda s: (layer, 0, 0)),
                full((HG_TILE_LEVELS, HG_CHUNK, GROUP_W))] + state_specs
    if states is None:
        aliases, state_layer = {}, layer
        state_out = pl.BlockSpec((per_step, DEPTH, 2, GROUP_W, HG_DK), lambda s: (s, 0, 0, 0, 0))
    else:
        aliases, state_layer = {len(in_specs): 1}, None
        in_specs = in_specs + [pl.BlockSpec(memory_space=pl.ANY)]
        state_args = state_args + (states,)
        state_out = pl.BlockSpec((per_step, None, 2, GROUP_W, HG_DK), lambda s: (s, layer, 0, 0, 0))
    return pl.pallas_call(
        functools.partial(_hgrn_kernel, seq=seq, n_seqs=per_step, has_state=has_state, state_layer=state_layer),
        grid=(n_seq // per_step,),
        in_specs=in_specs,
        out_specs=[pl.BlockSpec((rows, GROUP_W), lambda s: (s, 0)), state_out],
        out_shape=[jax.ShapeDtypeStruct((n_seq * seq, GROUP_W), F32),
                   jax.ShapeDtypeStruct((n_seq, DEPTH, 2, GROUP_W, HG_DK), F32)],
        scratch_shapes=[pltpu.VMEM((rows, GROUP_W), F32), pltpu.VMEM((per_step, 2, GROUP_W, GROUP_W), F32)],
        input_output_aliases=aliases,
        compiler_params=_params(1),
        name="hgrn",
    )(z, z, z, z, z, lb, norm_g, jnp.asarray(_HG_BMASK), *state_args)


def _hgrn_state_to_kernel(s):
    st = jnp.swapaxes(s.astype(F32), -1, -2)
    eye = jnp.eye(HG_HEADS, dtype=F32)
    full = jnp.einsum('ndhvk,hg->ndhvgk', st, eye)
    return full.reshape(s.shape[0], 2, GROUP_W, GROUP_W)


def _hgrn_state_from_kernel(st):
    return st.reshape(st.shape[:3] + (HG_HEADS, HG_DK, HG_DK))


S5_BLOCK = 8
S5_ROW_W = S5_BLOCK * GROUP_W
S5_HALVES = GROUP_W // LANE
S5_HALF_W = S5_BLOCK * LANE
S5_GROUPS_PER_HALF = S5_GROUPS // S5_HALVES
S5_PARTS = 4
S5_XW = S5_PARTS * S5_STATE
S5_PART_HALF = S5_STATE // S5_HALVES
S5_STEPS = (1, 2, 4)
S5_TABLES = 2 * (len(S5_STEPS) + 1)


def _s5_col(half, token):
    return half * S5_HALF_W + token * LANE


def _s5_in_kernel(u_ref, w_ref, x_ref):
    x_ref[...] = _dot(u_ref[...], w_ref[...])


def _s5_in_call(ub, wx, layer):
    n_blk = ub.shape[0]
    return pl.pallas_call(
        _s5_in_kernel,
        grid=(S5_HALVES, S5_PARTS),
        in_specs=[pl.BlockSpec((n_blk, S5_HALF_W), lambda h, p: (0, h)),
                  pl.BlockSpec((None, None, S5_HALF_W, S5_PART_HALF), lambda h, p: (layer, h, 0, p))],
        out_specs=pl.BlockSpec((n_blk, S5_PART_HALF), lambda h, p: (0, p * S5_HALVES + h)),
        out_shape=jax.ShapeDtypeStruct((n_blk, S5_XW), F32),
        compiler_params=_params(2),
        name="s5_in",
    )(ub, wx)


def _s5_scan_kernel(x_ref, tab_ref, s0re_ref, s0im_ref, h_ref, hre_ref, him_ref, *, n_blk):
    n_pairs = n_blk // (2 * SUBLANE)
    row = lax.broadcasted_iota(jnp.int32, (SUBLANE, S5_STATE), 0)
    n_steps = len(S5_STEPS)

    for direction in range(2):
        c_re = 2 * direction * S5_STATE
        c_im = c_re + S5_STATE
        edge = 0 if direction == 0 else SUBLANE - 1
        last = SUBLANE - 1 - edge
        unit = 0 if direction == 0 else SUBLANE - 1

        def tile(rows, carry):
            hc_r, hc_i, px_r, px_i = carry
            xr = x_ref[rows, c_re:c_re + S5_STATE]
            xi = x_ref[rows, c_im:c_im + S5_STATE]
            shift = 1 if direction == 0 else SUBLANE - 1
            hr = jnp.where(row == edge, px_r, pltpu.roll(xr, shift, 0))
            hi = jnp.where(row == edge, px_i, pltpu.roll(xi, shift, 0))
            for s, step in enumerate(S5_STEPS):
                sh = step if direction == 0 else SUBLANE - step
                rr = pltpu.roll(hr, sh, 0)
                ri = pltpu.roll(hi, sh, 0)
                ar = tab_ref[direction, 2 * s]
                ai = tab_ref[direction, 2 * s + 1]
                hr, hi = hr + ar * rr - ai * ri, hi + ar * ri + ai * rr
            pr = tab_ref[direction, 2 * n_steps]
            pi = tab_ref[direction, 2 * n_steps + 1]
            hr, hi = hr + pr * hc_r - pi * hc_i, hi + pr * hc_i + pi * hc_r
            bc = lambda a: jnp.broadcast_to(a[last:last + 1, :], (SUBLANE, S5_STATE))
            return hr, hi, (bc(hr), bc(hi), bc(xr), bc(xi))

        def pair(j, carry):
            jj = j if direction == 0 else n_pairs - 1 - j
            base = pl.multiple_of(jj * 2 * SUBLANE, 2 * SUBLANE)
            offs = (0, SUBLANE) if direction == 0 else (SUBLANE, 0)
            out = {}
            for off in offs:
                hr, hi, carry = tile(pl.ds(base + off, SUBLANE), carry)
                out[off] = (hr, hi)
            both = pl.ds(base, 2 * SUBLANE)
            h_ref[both, c_re:c_re + S5_STATE] = jnp.concatenate([out[0][0], out[SUBLANE][0]], 0).astype(BF16)
            h_ref[both, c_im:c_im + S5_STATE] = jnp.concatenate([out[0][1], out[SUBLANE][1]], 0).astype(BF16)
            return carry

        zero = jnp.zeros((SUBLANE, S5_STATE), F32)
        init = (zero, zero,
                jnp.broadcast_to(s0re_ref[direction], (SUBLANE, S5_STATE)),
                jnp.broadcast_to(s0im_ref[direction], (SUBLANE, S5_STATE)))
        hc_r, hc_i, px_r, px_i = lax.fori_loop(0, n_pairs, pair, init)
        a_r = tab_ref[direction, 2 * n_steps][unit:unit + 1, :]
        a_i = tab_ref[direction, 2 * n_steps + 1][unit:unit + 1, :]
        hre_ref[direction] = a_r * hc_r[0:1, :] - a_i * hc_i[0:1, :] + px_r[0:1, :]
        him_ref[direction] = a_r * hc_i[0:1, :] + a_i * hc_r[0:1, :] + px_i[0:1, :]


def _s5_scan_call(x, tab, s0re, s0im, layer, n_seq, n_blk):
    state = pl.BlockSpec((None, 2, 1, S5_STATE), lambda s: (s, 0, 0, 0))
    return pl.pallas_call(
        functools.partial(_s5_scan_kernel, n_blk=n_blk),
        grid=(n_seq,),
        in_specs=[pl.BlockSpec((n_blk, S5_XW), lambda s: (s, 0)),
                  pl.BlockSpec((None, 2, S5_TABLES, SUBLANE, S5_STATE), lambda s: (layer, 0, 0, 0, 0)),
                  state, state],
        out_specs=[pl.BlockSpec((n_blk, S5_XW), lambda s: (s, 0)), state, state],
        out_shape=[jax.ShapeDtypeStruct((n_seq * n_blk, S5_XW), BF16),
                   jax.ShapeDtypeStruct((n_seq, 2, 1, S5_STATE), F32),
                   jax.ShapeDtypeStruct((n_seq, 2, 1, S5_STATE), F32)],
        compiler_params=_params(1),
        name="s5_scan",
    )(x, tab, s0re, s0im)


def _s5_out_kernel(u_ref, *refs):
    h_refs, (wt_ref, wc_ref, y_ref) = refs[:S5_PARTS], refs[S5_PARTS:]
    y = _dot(u_ref[...], wt_ref[...])
    for p in range(S5_PARTS):
        y = y + _dot(h_refs[p][...], wc_ref[p * S5_PART_HALF:(p + 1) * S5_PART_HALF, :])
    y_ref[...] = y


def _s5_out_call(ub, h, wt, wc, layer):
    n_blk = ub.shape[0]

    def state_part(p):
        return pl.BlockSpec((n_blk, S5_PART_HALF), lambda j: (0, p * S5_HALVES + j))

    return pl.pallas_call(
        _s5_out_kernel,
        grid=(S5_HALVES,),
        in_specs=[pl.BlockSpec((n_blk, S5_HALF_W), lambda j: (0, j))]
                 + [state_part(p) for p in range(S5_PARTS)]
                 + [pl.BlockSpec((None, None, S5_HALF_W, S5_HALF_W), lambda j: (layer, j, 0, 0)),
                    pl.BlockSpec((None, None, S5_PARTS * S5_PART_HALF, S5_HALF_W), lambda j: (layer, j, 0, 0))],
        out_specs=pl.BlockSpec((n_blk, S5_HALF_W), lambda j: (0, j)),
        out_shape=jax.ShapeDtypeStruct((n_blk, S5_ROW_W), F32),
        compiler_params=_params(1),
        name="s5_out",
    )(ub, *([h] * S5_PARTS), wt, wc)


def _s5_mix(ub, consts, s0re, s0im, layer, n_seq, seq):
    wx, wt, wc, tab = consts
    n_blk = seq // S5_BLOCK
    x = _s5_in_call(ub, wx, layer)
    h, h_re, h_im = _s5_scan_call(x, tab, s0re, s0im, layer, n_seq, n_blk)
    y = _s5_out_call(ub, h, wt, wc, layer)
    return y, h_re, h_im


S5_EXPAND_ROWS = 512


def _expand_kernel(c_ref, t_ref, o_ref, keep_ref, *, row_group, col_group):
    rows, cols = o_ref.shape

    @pl.when((pl.program_id(0) == 0) & (pl.program_id(1) == 0))
    def _():
        r = lax.broadcasted_iota(jnp.int32, (rows, cols), 0)
        col = lax.broadcasted_iota(jnp.int32, (rows, cols), 1)
        group = lambda i, period_width: (i % period_width[0]) // period_width[1]
        keep_ref[...] = jnp.where(group(r, row_group) == group(col, col_group), 1.0, 0.0)

    o_ref[...] = (_dot(c_ref[...], t_ref[...]) * keep_ref[...]).astype(BF16)


def _expand_call(compact, spread, row_group, col_group):
    n_ops, n_rows, k = compact.shape
    n_cols = spread.shape[1]
    assert S5_EXPAND_ROWS % row_group[0] == 0
    return pl.pallas_call(
        functools.partial(_expand_kernel, row_group=row_group, col_group=col_group),
        grid=(n_ops, n_rows // S5_EXPAND_ROWS),
        in_specs=[pl.BlockSpec((None, S5_EXPAND_ROWS, k), lambda l, i: (l, i, 0)),
                  pl.BlockSpec((k, n_cols), lambda l, i: (0, 0))],
        out_specs=pl.BlockSpec((None, S5_EXPAND_ROWS, n_cols), lambda l, i: (l, i, 0)),
        out_shape=jax.ShapeDtypeStruct((n_ops, n_rows, n_cols), BF16),
        scratch_shapes=[pltpu.VMEM((S5_EXPAND_ROWS, n_cols), F32)],
        compiler_params=_params(2),
        name="s5_expand",
    )(compact, spread)


def _s5_constants(a_re, a_im, log_dt, b_re, b_im, c_re, c_im):
    n = S5_BLOCK
    a = lax.complex(a_re.astype(F32), a_im.astype(F32))
    dt = jnp.exp(log_dt.astype(F32))[..., None]
    a_bar = jnp.exp(a * dt)
    b_bar = ((a_bar - 1.0) / a)[..., None] * lax.complex(b_re.astype(F32), b_im.astype(F32))
    c = lax.complex(c_re.astype(F32), c_im.astype(F32))
    steps = jnp.arange(n + 1, dtype=F32)[None, None, :, None, None]
    pw = jnp.exp((a * dt)[:, :, None] * steps)
    gh = S5_GROUPS_PER_HALF
    kron = lambda *m: functools.reduce(np.kron, m)
    spread_state = jnp.asarray(kron(np.eye(S5_PARTS), np.ones((1, gh)), np.eye(S5_P)), BF16)
    spread_token = jnp.asarray(kron(np.eye(n), np.ones((1, gh)), np.eye(S5_IN)), BF16)
    token_group = (LANE, S5_IN)
    state_group = (S5_PART_HALF, S5_P)
    halves = lambda a: a.reshape(a.shape[:3] + (S5_HALVES, gh) + a.shape[4:])

    w_f = pw[:, 0, ::-1][:, 1:, :, :, None] * b_bar[:, 0, None]
    w_b = pw[:, 1, :n, :, :, None] * b_bar[:, 1, None]
    w4 = jnp.stack([jnp.real(w_f), jnp.imag(w_f), jnp.real(w_b), jnp.imag(w_b)], axis=1)
    wx_c = jnp.transpose(halves(w4), (0, 3, 2, 4, 6, 1, 5))
    wx_c = wx_c.reshape(DEPTH * S5_HALVES, S5_HALF_W, S5_PARTS * S5_P)
    wx = _expand_call(wx_c.astype(BF16), spread_state, token_group, state_group)
    wx = wx.reshape(DEPTH, S5_HALVES, S5_HALF_W, S5_PARTS * S5_PART_HALF)

    o_f = c[:, 0, None] * pw[:, 0, 1:, :, None, :]
    o_b = c[:, 1, None] * pw[:, 1, ::-1][:, :n, :, None, :]
    o4 = jnp.stack([jnp.real(o_f), -jnp.imag(o_f), jnp.real(o_b), -jnp.imag(o_b)], axis=1)
    wc_c = jnp.transpose(halves(o4), (0, 3, 1, 4, 6, 2, 5))
    wc_c = wc_c.reshape(DEPTH * S5_HALVES, S5_PARTS * S5_PART_HALF, n * S5_IN)
    wc = _expand_call(wc_c.astype(BF16), spread_token, state_group, token_group)
    wc = wc.reshape(DEPTH, S5_HALVES, S5_PARTS * S5_PART_HALF, S5_HALF_W)

    k_f = jnp.real(jnp.einsum('lgcp,ltgp,lgpd->lgdtc', c[:, 0], pw[:, 0, :n], b_bar[:, 0]))
    k_b = jnp.real(jnp.einsum('lgcp,ltgp,lgpd->lgdtc', c[:, 1], pw[:, 1, :n], b_bar[:, 1]))
    tt, jj, ii = np.meshgrid(np.arange(n), np.arange(n), np.arange(n), indexing='ij')
    place = lambda hit: jnp.asarray(np.kron(hit.reshape(n, n * n).astype(np.float32), np.eye(S5_IN)), F32)
    flat = lambda k: k.reshape(DEPTH * S5_GROUPS * S5_IN, n * S5_IN)
    t_all = (jnp.dot(flat(k_f), place(ii - jj == tt), precision=lax.Precision.HIGHEST)
             + jnp.dot(flat(k_b), place(jj - ii == tt), precision=lax.Precision.HIGHEST))
    wt_c = t_all.reshape(DEPTH, S5_HALVES, gh, S5_IN, n, n * S5_IN)
    wt_c = jnp.transpose(wt_c, (0, 1, 4, 2, 3, 5)).reshape(DEPTH * S5_HALVES, S5_HALF_W, n * S5_IN)
    wt = _expand_call(wt_c.astype(BF16), spread_token, token_group, token_group)
    wt = wt.reshape(DEPTH, S5_HALVES, S5_HALF_W, S5_HALF_W)

    block_steps = n * jnp.arange(1, SUBLANE + 1, dtype=F32)[None, None, :, None]
    pows_all = jnp.exp((a * dt).reshape(DEPTH, 2, 1, S5_STATE) * block_steps)
    pows = [pows_all[:, :, i] for i in range(SUBLANE)]
    row = jnp.arange(SUBLANE)
    tabs = []
    for direction in range(2):
        t = []
        for step in S5_STEPS:
            live = (row >= step) if direction == 0 else (row < SUBLANE - step)
            t.append(jnp.where(live[None, :, None], pows[step - 1][:, direction, None, :], 0.0))
        order = range(SUBLANE) if direction == 0 else range(SUBLANE - 1, -1, -1)
        t.append(jnp.stack([pows[i][:, direction] for i in order], axis=1))
        planes = []
        for x in t:
            planes += [jnp.real(x), jnp.imag(x)]
        tabs.append(jnp.stack(planes, axis=1))
    tab = jnp.stack(tabs, axis=1).astype(F32)
    return wx, wt, wc, tab


OUT_TILE = 512
FF_BLOCK = 1024


def _outproj_kernel(x_ref, oa_ref, ob_ref, oc_ref, ys_ref, zs_ref, sd_ref, gw_ref, gb_ref,
                    ada_ref, g2_ref, wo_ref, w1_ref, w2_ref, gf_ref, *rest, final):
    out_refs, tok_ref = rest[:-1], rest[-1]
    ada = ada_ref[...]
    chunk = lambda j: ada[:, j * D_MODEL:(j + 1) * D_MODEL]
    gate1, sh2, sc2, gate2 = chunk(2), chunk(3), chunk(4), chunk(5)
    for j in range(S5_BLOCK):
        for h in range(GROUP_W // LANE):
            c0 = _s5_col(h, j)
            tok_ref[h, pl.ds(j, OUT_TILE // S5_BLOCK, stride=S5_BLOCK), :] = ys_ref[:, c0:c0 + LANE]
    ssm = jnp.concatenate([tok_ref[h] for h in range(GROUP_W // LANE)], axis=-1)
    ys = jax.nn.gelu(ssm + sd_ref[...] * zs_ref[...])
    od = ys * _sigmoid(_dot(ys.astype(BF16), gw_ref[...]) + gb_ref[...])
    mixed = jnp.concatenate([oa_ref[...], ob_ref[...], oc_ref[...], od], axis=-1).astype(BF16)
    x = x_ref[...] + gate1 * _dot(mixed, wo_ref[...])
    ms = jnp.mean(x * x, axis=-1, keepdims=True)
    h = (x * lax.rsqrt(ms + EPS) * (g2_ref[...] * (1.0 + sc2)) + sh2).astype(BF16)
    acc = jnp.zeros((OUT_TILE, D_MODEL), F32)
    for j in range(D_FF // FF_BLOCK):
        cols = slice(j * FF_BLOCK, (j + 1) * FF_BLOCK)
        a = jnp.maximum(_dot(h, w1_ref[:, cols]), 0.0)
        acc = acc + _dot((a * a).astype(BF16), w2_ref[cols, :])
    x = x + gate2 * acc
    out_refs[0][...] = x
    if final:
        ms = jnp.mean(x * x, axis=-1, keepdims=True)
        out_refs[1][...] = x * lax.rsqrt(ms + EPS) * gf_ref[...]


def _outproj_call(x, mix, s5_par, ada4, g2, w_out, w1, w2, gf, layer, cond_row, final):
    n_tok = x.shape[0]
    tok = pl.BlockSpec((OUT_TILE, D_MODEL), lambda i: (i, 0))
    grp = pl.BlockSpec((OUT_TILE, GROUP_W), lambda i: (i, 0))
    lay = lambda shape: pl.BlockSpec((None,) + shape, lambda i: (layer,) + (0,) * len(shape))
    n_out = 2 if final else 1
    res = pl.pallas_call(
        functools.partial(_outproj_kernel, final=final),
        grid=(n_tok // OUT_TILE,),
        in_specs=[tok, grp, grp, grp,
                  pl.BlockSpec((OUT_TILE // S5_BLOCK, S5_ROW_W), lambda i: (i, 0)), grp,
                  lay((1, GROUP_W)), lay((GROUP_W, GROUP_W)), lay((1, GROUP_W)),
                  pl.BlockSpec((None, None, 1, ADA_CHUNKS * D_MODEL),
                               lambda i: (layer, cond_row(i * OUT_TILE), 0, 0)),
                  lay((1, D_MODEL)), lay((D_MODEL, D_MODEL)), lay((D_MODEL, D_FF)), lay((D_FF, D_MODEL)),
                  pl.BlockSpec((1, D_MODEL), lambda i: (0, 0))],
        out_specs=[tok] * n_out,
        out_shape=[jax.ShapeDtypeStruct((n_tok, D_MODEL), F32)] * n_out,
        scratch_shapes=[pltpu.VMEM((GROUP_W // LANE, OUT_TILE, LANE), F32)],
        compiler_params=_params(1),
        name="outproj",
    )(x, *mix, *s5_par, ada4, g2, w_out, w1, w2, gf)
    return res


def _grid_pos_embed(n_tokens):
    rows = n_tokens // GRID_W
    r, col = jnp.meshgrid(jnp.arange(rows, dtype=F32), jnp.arange(GRID_W, dtype=F32), indexing='ij')
    r = r.reshape(-1)
    col = col.reshape(-1)
    quarter = D_MODEL // 4
    freq = jnp.exp(-math.log(POS_BASE) * jnp.arange(quarter, dtype=F32) / quarter)
    ar = r[:, None] * freq
    ac = col[:, None] * freq
    return jnp.concatenate([jnp.sin(ar), jnp.cos(ar), jnp.sin(ac), jnp.cos(ac)], axis=-1)


def _layer(x, l, n_seq, seq, cond_row, hg_s0, hg_states, s5_s0re, s5_s0im, P, final):
    z, zs, zs_blk = _inproj_call(x, P['ada4'], P['norm1_g'], P['w_in'], l, cond_row)
    o_conv = _conv_call(z, P['conv_w'], P['conv_b'], P['conv_ln_g'], P['conv_ln_b'], l, n_seq, seq)
    o_gm = _gmlp_call(z, P['gmlp_norm_g'], P['gmlp_ws'], P['gmlp_bs'], l)
    o_hg, hg_end = _hgrn_call(z, P['hgrn_lb'], P['hgrn_norm_g'], hg_s0, hg_states, l, n_seq, seq)
    y_s5, h_re, h_im = _s5_mix(zs_blk, P['s5'], s5_s0re, s5_s0im, l, n_seq, seq)
    res = _outproj_call(x, (o_conv, o_gm, o_hg, y_s5, zs), (P['s5_d'], P['s5_glu_w'], P['s5_glu_b']),
                        P['ada4'], P['norm2_g'], P['w_out'],
                        P['mlp_w1'], P['mlp_w2'], P['final_norm_g'], l, cond_row, final)
    return res, hg_end, h_re, h_im


def _prepare(w, hgrn_lb):
    vec = lambda a: a.astype(F32).reshape(DEPTH, 1, -1)
    hd = GROUP_W // GM_HEADS
    return {
        'norm1_g': vec(w['norm1_g']), 'norm2_g': vec(w['norm2_g']),
        'w_in': w['w_in'].astype(BF16), 'w_out': w['w_out'].astype(BF16),
        'mlp_w1': w['mlp_w1'].astype(BF16), 'mlp_w2': w['mlp_w2'].astype(BF16),
        'conv_w': jnp.broadcast_to(w['conv_w'].astype(F32)[:, :, None, :], (DEPTH, CONV_K, SUBLANE, GROUP_W)),
        'conv_b': vec(w['conv_b']), 'conv_ln_g': vec(w['conv_ln_g']), 'conv_ln_b': vec(w['conv_ln_b']),
        'gmlp_norm_g': vec(w['gmlp_norm_g']),
        'gmlp_ws': jnp.transpose(w['gmlp_ws'], (0, 2, 1, 3)).reshape(
            DEPTH, GM_CHUNK, GM_HEADS * GM_CHUNK).astype(BF16),
        'gmlp_bs': jnp.repeat(jnp.transpose(w['gmlp_bs'].astype(F32), (0, 2, 1)), hd, axis=2),
        'hgrn_lb': jnp.transpose(hgrn_lb, (1, 0, 2)), 'hgrn_norm_g': vec(w['hgrn_norm_g']),
        's5': _s5_constants(w['s5_a_re'], w['s5_a_im'], w['s5_log_dt'], w['s5_b_re'], w['s5_b_im'],
                            w['s5_c_re'], w['s5_c_im']),
        's5_d': vec(w['s5_d']), 's5_glu_w': w['s5_glu_w'].astype(BF16), 's5_glu_b': vec(w['s5_glu_b']),
        'final_norm_g': w['final_norm_g'].astype(F32).reshape(1, D_MODEL),
    }


def _run_stream(x, n_seq, seq, cond_row, hg_s0, s5_s0re, s5_s0im, P):
    hg_states, s5_re, s5_im = None, [], []
    y = None
    for l in range(DEPTH):
        final = l == DEPTH - 1
        res, hg_states, h_re, h_im = _layer(x, l, n_seq, seq, cond_row, hg_s0[l], hg_states,
                                            s5_s0re[l], s5_s0im[l], P, final)
        x = res[0]
        if final:
            y = res[1]
        s5_re.append(h_re)
        s5_im.append(h_im)
    return y, hg_states, s5_re, s5_im


def kernel(x_prompt, x_sample, state_hgrn, state_s5_re, state_s5_im, c, c_ctx, norm1_g, norm2_g, ada_w, ada_b, w_in, conv_w, conv_b, conv_ln_g, conv_ln_b, gmlp_norm_g, gmlp_ws, gmlp_bs, hgrn_lb_logits, hgrn_norm_g, s5_a_re, s5_a_im, s5_log_dt, s5_b_re, s5_b_im, s5_c_re, s5_c_im, s5_d, s5_glu_w, s5_glu_b, w_out, mlp_w1, mlp_w2, final_norm_g):
    n_ctx, seq_ctx, _ = x_prompt.shape
    n_lat, seq_lat, _ = x_sample.shape
    assert n_lat + 1 <= COND_ROWS
    assert seq_ctx % IN_TILE == 0 or IN_TILE % seq_ctx == 0

    lb_soft = jax.nn.softmax(hgrn_lb_logits.astype(F32), axis=1)
    hgrn_lb = jnp.cumsum(lb_soft, axis=1) - lb_soft[:, :1]
    P = _prepare(dict(
        norm1_g=norm1_g, norm2_g=norm2_g, w_in=w_in, w_out=w_out, mlp_w1=mlp_w1, mlp_w2=mlp_w2,
        conv_w=conv_w, conv_b=conv_b, conv_ln_g=conv_ln_g, conv_ln_b=conv_ln_b,
        gmlp_norm_g=gmlp_norm_g, gmlp_ws=gmlp_ws, gmlp_bs=gmlp_bs, hgrn_norm_g=hgrn_norm_g,
        s5_a_re=s5_a_re, s5_a_im=s5_a_im, s5_log_dt=s5_log_dt, s5_b_re=s5_b_re, s5_b_im=s5_b_im,
        s5_c_re=s5_c_re, s5_c_im=s5_c_im, s5_d=s5_d, s5_glu_w=s5_glu_w, s5_glu_b=s5_glu_b,
        final_norm_g=final_norm_g), hgrn_lb)
    cond =jnp.zeros((COND_ROWS, D_MODEL), F32).at[0].set(c_ctx.astype(F32)).at[1:1 + n_lat].set(c.astype(F32))
    ada = _ada_call(cond, ada_w.astype(F32), ada_b.astype(F32))
    P['ada4'] = ada.reshape(DEPTH, COND_ROWS, 1, ADA_CHUNKS * D_MODEL)

    zeros_s5 = jnp.zeros((n_ctx, 2, 1, S5_STATE), F32)
    y_ctx, hg_ctx, re_ctx, im_ctx = _run_stream(
        x_prompt.astype(F32).reshape(n_ctx * seq_ctx, D_MODEL), n_ctx, seq_ctx, lambda r: 0,
        [None] * DEPTH, [zeros_s5] * DEPTH, [zeros_s5] * DEPTH, P)

    xs = x_sample.astype(F32) + _grid_pos_embed(seq_lat)[None]
    hg0 = [_hgrn_state_to_kernel(state_hgrn[:, l]) for l in range(DEPTH)]
    re0 = [state_s5_re[:, l].astype(F32).reshape(n_lat, 2, 1, S5_STATE) for l in range(DEPTH)]
    im0 = [state_s5_im[:, l].astype(F32).reshape(n_lat, 2, 1, S5_STATE) for l in range(DEPTH)]
    y_lat, _, _, _ = _run_stream(xs.reshape(n_lat * seq_lat, D_MODEL), n_lat, seq_lat,
                                 lambda r: 1 + r // seq_lat, hg0, re0, im0, P)

    dt = x_prompt.dtype
    new_hg = _hgrn_state_from_kernel(hg_ctx).astype(dt)
    new_re = jnp.stack([s.reshape(n_ctx, 2, S5_GROUPS, S5_P) for s in re_ctx], axis=1).astype(dt)
    new_im = jnp.stack([s.reshape(n_ctx, 2, S5_GROUPS, S5_P) for s in im_ctx], axis=1).astype(dt)
    return (y_ctx.reshape(n_ctx, seq_ctx, D_MODEL).astype(dt),
            y_lat.reshape(n_lat, seq_lat, D_MODEL).astype(x_sample.dtype), new_hg, new_re, new_im)
```
